```python
import jax, jax.numpy as jnp
from jax import lax
import numpy as np

D_MODEL = 2048
BATCH = 4
SEQ = 4096
DEPTH = 2

N_HEADS_MLA = 8
QK_NOPE_DIM = 128
QK_ROPE_DIM = 64
V_HEAD_DIM = 128
Q_LORA_RANK = 512
KV_LORA_RANK = 256
D_MLA = N_HEADS_MLA * V_HEAD_DIM
ROPE_THETA = 10000.0
Q_BLOCK = 128
POOL_WINDOWS = (2, 4, 8, 16)
N_POOL_GROUPS = 4
POOL_GROUP_DIM = 128
D_POOL = N_POOL_GROUPS * POOL_GROUP_DIM
N_CONV_HEADS = 4
CONV_HEAD_DIM = 128
D_CONV = N_CONV_HEADS * CONV_HEAD_DIM
CONV_WIDTH = 3
D_MIX = D_MLA + D_POOL + D_CONV
SPLIT_SIZES = (Q_LORA_RANK, KV_LORA_RANK, QK_ROPE_DIM, D_MLA, D_POOL, D_POOL, D_CONV, D_CONV, D_CONV, D_CONV)
D_IN_PROJ = sum(SPLIT_SIZES)
LN_EPS = 1e-5
RMS_EPS = 1e-6
DEEPNORM_ALPHA = (2 * DEPTH) ** 0.25
DEEPNORM_BETA = (8 * DEPTH) ** -0.25

kernel_name = "hybrid_mla_pool_shortconv_deepnorm"


def layernorm(x, g, b):
    xf = x.astype(jnp.float32)
    mu = jnp.mean(xf, axis=-1, keepdims=True)
    var = jnp.mean(jnp.square(xf - mu), axis=-1, keepdims=True)
    y = (xf - mu) * lax.rsqrt(var + LN_EPS) * g.astype(jnp.float32) + b.astype(jnp.float32)
    return y.astype(x.dtype)


def rmsnorm(x, g):
    xf = x.astype(jnp.float32)
    y = xf * lax.rsqrt(jnp.mean(jnp.square(xf), axis=-1, keepdims=True) + RMS_EPS) * g.astype(jnp.float32)
    return y.astype(x.dtype)


def apply_rope(x, cos, sin):
    half = QK_ROPE_DIM // 2
    xf = x.astype(jnp.float32)
    x1, x2 = xf[..., :half], xf[..., half:]
    return jnp.concatenate([x1 * cos - x2 * sin, x2 * cos + x1 * sin], axis=-1).astype(x.dtype)


def mla_mixer(q_lat, kv_lat, k_rope, positions, q_norm_g, kv_norm_g, w_uq, w_ukv):
    Bn, S, _ = q_lat.shape
    H = N_HEADS_MLA
    q = (rmsnorm(q_lat, q_norm_g) @ w_uq).reshape(Bn, S, H, QK_NOPE_DIM + QK_ROPE_DIM)
    q_nope, q_rope = q[..., :QK_NOPE_DIM], q[..., QK_NOPE_DIM:]
    kv = (rmsnorm(kv_lat, kv_norm_g) @ w_ukv).reshape(Bn, S, H, QK_NOPE_DIM + V_HEAD_DIM)
    k_nope, v = kv[..., :QK_NOPE_DIM], kv[..., QK_NOPE_DIM:]
    half = QK_ROPE_DIM // 2
    inv_freq = ROPE_THETA ** (-jnp.arange(half, dtype=jnp.float32) / half)
    ang = positions.astype(jnp.float32)[..., None] * inv_freq
    cos, sin = jnp.cos(ang), jnp.sin(ang)
    q_rope = apply_rope(q_rope, cos[:, :, None, :], sin[:, :, None, :])
    k_rope = apply_rope(k_rope, cos, sin)
    scale = (QK_NOPE_DIM + QK_ROPE_DIM) ** -0.5
    nb = S // Q_BLOCK
    qn_blocks = q_nope.reshape(Bn, nb, Q_BLOCK, H, QK_NOPE_DIM).transpose(1, 0, 2, 3, 4)
    qr_blocks = q_rope.reshape(Bn, nb, Q_BLOCK, H, QK_ROPE_DIM).transpose(1, 0, 2, 3, 4)
    key_idx = jnp.arange(S)

    def attend(args):
        qn, qr, blk = args
        s = (jnp.einsum('bqhd,bkhd->bhqk', qn, k_nope).astype(jnp.float32)
             + jnp.einsum('bqhr,bkr->bhqk', qr, k_rope).astype(jnp.float32)) * scale
        q_idx = blk * Q_BLOCK + jnp.arange(Q_BLOCK)
        causal = key_idx[None, :] <= q_idx[:, None]
        s = jnp.where(causal[None, None], s, -jnp.inf)
        p = jax.nn.softmax(s, axis=-1).astype(v.dtype)
        return jnp.einsum('bhqk,bkhd->bqhd', p, v)

    out = lax.map(attend, (qn_blocks, qr_blocks, jnp.arange(nb)))
    return out.transpose(1, 0, 2, 3, 4).reshape(Bn, S, D_MLA)


def pool_mixer(h, w_pool, pool_scale):
    Bn, S, _ = h.shape
    hg = h.reshape(Bn, S, N_POOL_GROUPS, POOL_GROUP_DIM).astype(jnp.float32)
    cs = jnp.cumsum(hg, axis=1)
    t1 = jnp.arange(1, S + 1, dtype=jnp.float32)
    means = []
    for g, w in enumerate(POOL_WINDOWS):
        c = cs[:, :, g]
        lag = jnp.pad(c, ((0, 0), (w, 0), (0, 0)))[:, :S]
        means.append((c - lag) / jnp.minimum(t1, float(w))[None, :, None])
    pooled = (jnp.stack(means, axis=2) - hg).astype(h.dtype)
    y = jnp.einsum('bsgc,gcd->bsgd', pooled, w_pool).reshape(Bn, S, D_POOL)
    return y * pool_scale


def conv_mixer(h, b_gate, c_gate, conv_w):
    u = c_gate * h
    y = lax.conv_general_dilated(u, conv_w[:, None, :], window_strides=(1,),
                                 padding=((CONV_WIDTH - 1, 0),),
                                 dimension_numbers=('NWC', 'WIO', 'NWC'),
                                 feature_group_count=D_CONV)
    return b_gate * y


def hybrid_layer(x, positions, w_in, q_norm_g, kv_norm_g, w_uq, w_ukv, w_pool, pool_scale,
                 conv_w, w_out, b_out, ln_g, ln_b):
    split_idx = np.cumsum(SPLIT_SIZES)[:-1].tolist()
    proj = x @ w_in
    (q_lat, kv_lat, k_rope, g_mla, p_in, g_pool, c_h, c_b, c_c, g_conv) = jnp.split(proj, split_idx, axis=-1)
    y_mla = mla_mixer(q_lat, kv_lat, k_rope, positions, q_norm_g, kv_norm_g, w_uq, w_ukv) * jax.nn.silu(g_mla)
    y_pool = pool_mixer(p_in, w_pool, pool_scale) * jax.nn.silu(g_pool)
    y_conv = conv_mixer(c_h, c_b, c_c, conv_w) * jax.nn.silu(g_conv)
    mix = jnp.concatenate([y_mla, y_pool, y_conv], axis=-1)
    out = mix @ w_out + b_out
    return layernorm(DEEPNORM_ALPHA * x + out, ln_g, ln_b)


def setup_inputs(seed: int = 0) -> dict:
    key = jax.random.key(seed)
    ks = jax.random.split(key, 16)
    f32 = jnp.float32
    nrm = lambda k, shape, s: jax.random.normal(k, shape, f32) * s
    x = jax.random.normal(ks[0], (BATCH, SEQ, D_MODEL), f32)
    positions = jnp.broadcast_to(jnp.arange(SEQ, dtype=jnp.int32), (BATCH, SEQ))
    return {
        "x": x,
        "positions": positions,
        "emb_ln_g": 1.0 + nrm(ks[1], (D_MODEL,), 0.01),
        "emb_ln_b": nrm(ks[2], (D_MODEL,), 0.01),
        "w_in": nrm(ks[3], (DEPTH, D_MODEL, D_IN_PROJ), D_MODEL ** -0.5),
        "q_norm_g": 1.0 + nrm(ks[4], (DEPTH, Q_LORA_RANK), 0.01),
        "kv_norm_g": 1.0 + nrm(ks[5], (DEPTH, KV_LORA_RANK), 0.01),
        "w_uq": nrm(ks[6], (DEPTH, Q_LORA_RANK, N_HEADS_MLA * (QK_NOPE_DIM + QK_ROPE_DIM)), Q_LORA_RANK ** -0.5),
        "w_ukv": nrm(ks[7], (DEPTH, KV_LORA_RANK, N_HEADS_MLA * (QK_NOPE_DIM + V_HEAD_DIM)), KV_LORA_RANK ** -0.5),
        "w_pool": nrm(ks[8], (DEPTH, N_POOL_GROUPS, POOL_GROUP_DIM, POOL_GROUP_DIM), POOL_GROUP_DIM ** -0.5),
        "pool_scale": 1.0 + nrm(ks[9], (DEPTH, D_POOL), 0.1),
        "conv_w": nrm(ks[10], (DEPTH, CONV_WIDTH, D_CONV), CONV_WIDTH ** -0.5),
        "w_out": nrm(ks[11], (DEPTH, D_MIX, D_MODEL), DEEPNORM_BETA * D_MIX ** -0.5),
        "b_out": nrm(ks[12], (DEPTH, D_MODEL), 0.01),
        "ln_g": 1.0 + nrm(ks[13], (DEPTH, D_MODEL), 0.01),
        "ln_b": nrm(ks[14], (DEPTH, D_MODEL), 0.01),
    }


def reference(x, positions, emb_ln_g, emb_ln_b, w_in, q_norm_g, kv_norm_g, w_uq, w_ukv, w_pool,
              pool_scale, conv_w, w_out, b_out, ln_g, ln_b):
    h = layernorm(x, emb_ln_g, emb_ln_b)
    for l in range(DEPTH):
        h = hybrid_layer(h, positions, w_in[l], q_norm_g[l], kv_norm_g[l], w_uq[l], w_ukv[l], w_pool[l],
                         pool_scale[l], conv_w[l], w_out[l], b_out[l], ln_g[l], ln_b[l])
    return h
```

```python
import functools
import math

import jax
import jax.numpy as jnp
import numpy as np
from jax.experimental import pallas as pl
from jax.experimental.pallas import tpu as pltpu

D_MODEL = 2048
DEPTH = 2
N_HEADS = 8
NOPE = 128
ROPE = 64
V_DIM = 128
Q_LORA = 512
KV_LORA = 256
D_MLA = N_HEADS * V_DIM
ROPE_THETA = 10000.0
POOL_WINDOWS = (2, 4, 8, 16)
POOL_GROUP = 128
D_POOL = 512
D_CONV = 512
CONV_WIDTH = 3
LN_EPS = 1e-5
RMS_EPS = 1e-6
DEEPNORM_ALPHA = (2 * DEPTH) ** 0.25

LANES = 128
SUBLANES = 8
VMEM_LIMIT_BYTES = 56 * 1024 * 1024

PROJ_TM = 256
ATT_TQ = 512
ATT_TK = 512
OUT_TM = 512
POOL_HALO = 16
CONV_HALO = 8

HEAD_W = 2 * LANES

C_QLAT = 0
C_KVLAT = C_QLAT + Q_LORA
C_KR = C_KVLAT + KV_LORA
C_GMLA = C_KR + LANES
C_PIN = C_GMLA + D_MLA
C_GPOOL = C_PIN + D_POOL
C_CH = C_GPOOL + D_POOL
C_CB = C_CH + D_CONV
C_CC = C_CB + D_CONV
C_GCONV = C_CC + D_CONV
C_END = C_GCONV + D_CONV

Q_PRESCALE = (NOPE + ROPE) ** -0.5 * math.log2(math.e)
NEG_BIG = -1e30


def _silu(g):
    return g * (1.0 / (1.0 + jnp.exp(-g)))


def _layernorm(x, g, b):
    mu = jnp.mean(x, axis=-1, keepdims=True)
    xc = x - mu
    var = jnp.mean(xc * xc, axis=-1, keepdims=True)
    return xc * jax.lax.rsqrt(var + LN_EPS) * g + b


def _rmsnorm(x, g):
    return x * jax.lax.rsqrt(jnp.mean(x * x, axis=-1, keepdims=True) + RMS_EPS) * g


def _dot(a, b):
    return jnp.dot(a, b, preferred_element_type=jnp.float32)


def _proj_kernel(first, *refs):
    if first:
        (x_ref, pos_ref, eg_ref, eb_ref, w_in_ref, qg_ref, kvg_ref, w_uq_ref, w_ukv_ref,
         w_pool_ref, pscale_ref, convw_ref, invf_ref,
         q_ref, k_ref, v_ref, gm_ref, pc_ref, h_ref, pool_ext, conv_ext) = refs
    else:
        (x_ref, pos_ref, w_in_ref, qg_ref, kvg_ref, w_uq_ref, w_ukv_ref,
         w_pool_ref, pscale_ref, convw_ref, invf_ref,
         q_ref, k_ref, v_ref, gm_ref, pc_ref, pool_ext, conv_ext) = refs
    tm = PROJ_TM
    i = pl.program_id(1)

    @pl.when(i == 0)
    def _():
        pool_ext[0:POOL_HALO, :] = jnp.zeros((POOL_HALO, D_POOL), jnp.float32)
        conv_ext[0:CONV_HALO, :] = jnp.zeros((CONV_HALO, D_CONV), jnp.float32)

    @pl.when(i > 0)
    def _():
        pool_ext[0:POOL_HALO, :] = pool_ext[tm:tm + POOL_HALO, :]
        conv_ext[0:CONV_HALO, :] = conv_ext[tm:tm + CONV_HALO, :]

    x = x_ref[0]
    if first:
        x = _layernorm(x, eg_ref[...], eb_ref[...])
        h_ref[0] = x
    hb = x.astype(jnp.bfloat16)

    lat = _dot(hb, w_in_ref[:, C_QLAT:C_GMLA])
    qn = _rmsnorm(lat[:, C_QLAT:C_KVLAT], qg_ref[...]).astype(jnp.bfloat16)
    kvn = _rmsnorm(lat[:, C_KVLAT:C_KR], kvg_ref[...]).astype(jnp.bfloat16)
    kr = lat[:, C_KR:C_GMLA]

    ang = pos_ref[0].astype(jnp.float32) * invf_ref[...]
    lane = jax.lax.broadcasted_iota(jnp.int32, (tm, LANES), 1)
    cs = jnp.where(lane < ROPE, jnp.cos(ang), jnp.sin(ang))

    def rope(xr):
        t = xr * cs
        return t + pltpu.roll(t, ROPE, axis=1)

    kroped = jnp.where(lane < ROPE, rope(kr), 0.0).astype(jnp.bfloat16)

    q = _dot(qn, w_uq_ref[...])
    kv = _dot(kvn, w_ukv_ref[...])
    for hh in range(N_HEADS):
        c0 = hh * HEAD_W
        qh = jnp.concatenate([q[:, c0:c0 + NOPE], rope(q[:, c0 + NOPE:c0 + HEAD_W])], axis=1)
        q_ref[0, hh] = (qh * Q_PRESCALE).astype(jnp.bfloat16)
        k_ref[0, hh] = jnp.concatenate([kv[:, c0:c0 + NOPE].astype(jnp.bfloat16), kroped], axis=1)
        v_ref[0, hh] = kv[:, c0 + NOPE:c0 + HEAD_W].astype(jnp.bfloat16)

    gm_ref[0] = _silu(_dot(hb, w_in_ref[:, C_GMLA:C_PIN])).astype(jnp.bfloat16)

    pin = _dot(hb, w_in_ref[:, C_PIN:C_GPOOL])
    pool_ext[POOL_HALO:POOL_HALO + tm, :] = pin
    t1 = (i * tm + 1 + jax.lax.broadcasted_iota(jnp.int32, (tm, 1), 0)).astype(jnp.float32)
    ys = []
    for g, w in enumerate(POOL_WINDOWS):
        lo = g * POOL_GROUP
        acc = pin[:, lo:lo + POOL_GROUP]
        for s in range(1, w):
            acc = acc + pool_ext[POOL_HALO - s:POOL_HALO - s + tm, lo:lo + POOL_GROUP]
        pooled = acc / jnp.minimum(t1, float(w)) - pin[:, lo:lo + POOL_GROUP]
        ys.append(_dot(pooled.astype(jnp.bfloat16), w_pool_ref[g]))
    y_pool = jnp.concatenate(ys, axis=1) * pscale_ref[...]
    y_pool = y_pool * _silu(_dot(hb, w_in_ref[:, C_GPOOL:C_CH]))

    cv = _dot(hb, w_in_ref[:, C_CH:C_END])
    u = cv[:, 2 * D_CONV:3 * D_CONV] * cv[:, 0:D_CONV]
    conv_ext[CONV_HALO:CONV_HALO + tm, :] = u
    yc = convw_ref[2:3, :] * u
    yc = yc + convw_ref[1:2, :] * conv_ext[CONV_HALO - 1:CONV_HALO - 1 + tm, :]
    yc = yc + convw_ref[0:1, :] * conv_ext[CONV_HALO - 2:CONV_HALO - 2 + tm, :]
    y_conv = cv[:, D_CONV:2 * D_CONV] * yc * _silu(cv[:, 3 * D_CONV:4 * D_CONV])

    pc_ref[0] = jnp.concatenate([y_pool, y_conv], axis=1).astype(jnp.bfloat16)


def _const_spec(shape):
    nd = len(shape)
    return pl.BlockSpec(shape, lambda b, i: (0,) * nd, pipeline_mode=pl.Buffered(1))


def _proj_call(first, x, pos, emb_g, emb_b, w_in, qg, kvg, w_uq, w_ukv, w_pool, pscale, convw, invf):
    B, S, D = x.shape
    tm = PROJ_TM
    grid = (B, S // tm)
    tok = lambda w: pl.BlockSpec((1, tm, w), lambda b, i: (b, i, 0))
    head = lambda w: pl.BlockSpec((1, N_HEADS, tm, w), lambda b, i: (b, 0, i, 0))
    in_specs = [tok(D), tok(1)]
    args = [x, pos]
    if first:
        in_specs += [_const_spec((1, D)), _const_spec((1, D))]
        args += [emb_g, emb_b]
    consts = [w_in, qg, kvg, w_uq, w_ukv, w_pool, pscale, convw, invf]
    in_specs += [_const_spec(c.shape) for c in consts]
    args += consts
    bf = jnp.bfloat16
    out_shape = [
        jax.ShapeDtypeStruct((B, N_HEADS, S, HEAD_W), bf),
        jax.ShapeDtypeStruct((B, N_HEADS, S, HEAD_W), bf),
        jax.ShapeDtypeStruct((B, N_HEADS, S, V_DIM), bf),
        jax.ShapeDtypeStruct((B, S, D_MLA), bf),
        jax.ShapeDtypeStruct((B, S, D_POOL + D_CONV), bf),
    ]
    out_specs = [head(HEAD_W), head(HEAD_W), head(V_DIM), tok(D_MLA), tok(D_POOL + D_CONV)]
    if first:
        out_shape.append(jax.ShapeDtypeStruct((B, S, D), jnp.float32))
        out_specs.append(tok(D))
    return pl.pallas_call(
        functools.partial(_proj_kernel, first),
        grid=grid,
        in_specs=in_specs,
        out_specs=out_specs,
        out_shape=out_shape,
        scratch_shapes=[
            pltpu.VMEM((POOL_HALO + tm, D_POOL), jnp.float32),
            pltpu.VMEM((CONV_HALO + tm, D_CONV), jnp.float32),
        ],
        compiler_params=pltpu.CompilerParams(
            dimension_semantics=("arbitrary", "arbitrary"),
            vmem_limit_bytes=VMEM_LIMIT_BYTES),
        name="proj_first" if first else "proj",
    )(*args)


def _attn_kernel(q_ref, k_ref, v_ref, gm_ref, o_ref, m_sc, l_sc, acc_sc):
    tq, tk = ATT_TQ, ATT_TK
    i = pl.program_id(2)
    q = q_ref[0, 0]

    def scores(start):
        k = k_ref[0, 0, pl.ds(start, tk), :]
        return jax.lax.dot_general(q, k, (((1,), (1,)), ((), ())),
                                   preferred_element_type=jnp.float32)

    d0 = pl.multiple_of(i * tq, tq)
    s = scores(d0)
    row = jax.lax.broadcasted_iota(jnp.int32, (tq, tk), 0)
    col = jax.lax.broadcasted_iota(jnp.int32, (tq, tk), 1)
    s = jnp.where(col <= row, s, -jnp.inf)
    m0 = jnp.max(s, axis=-1, keepdims=True)
    p = jnp.exp2(s - m0)
    m_sc[...] = m0
    l_sc[...] = jnp.sum(p, axis=-1, keepdims=True)
    acc_sc[...] = _dot(p.astype(jnp.bfloat16), v_ref[0, 0, pl.ds(d0, tk), :])

    def body(j, carry):
        start = pl.multiple_of(j * tk, tk)
        s = scores(start)
        m_prev = m_sc[...]
        m_new = jnp.maximum(m_prev, jnp.max(s, axis=-1, keepdims=True))
        alpha = jnp.exp2(m_prev - m_new)
        p = jnp.exp2(s - m_new)
        l_sc[...] = alpha * l_sc[...] + jnp.sum(p, axis=-1, keepdims=True)
        acc_sc[...] = alpha * acc_sc[...] + _dot(p.astype(jnp.bfloat16),
                                                 v_ref[0, 0, pl.ds(start, tk), :])
        m_sc[...] = m_new
        return carry

    jax.lax.fori_loop(0, i * (tq // tk), body, 0)

    out = acc_sc[...] / l_sc[...] * gm_ref[0].astype(jnp.float32)
    o_ref[0] = out.astype(o_ref.dtype)


def _attn_call(q, k, v, gm):
    B, H, S, _ = q.shape
    tq = ATT_TQ
    assert ATT_TQ == ATT_TK
    grid = (B, H, S // tq)
    return pl.pallas_call(
        _attn_kernel,
        grid=grid,
        in_specs=[
            pl.BlockSpec((1, 1, tq, HEAD_W), lambda b, h, i: (b, h, i, 0)),
            pl.BlockSpec((1, 1, S, HEAD_W), lambda b, h, i: (b, h, 0, 0)),
            pl.BlockSpec((1, 1, S, V_DIM), lambda b, h, i: (b, h, 0, 0)),
            pl.BlockSpec((1, tq, V_DIM), lambda b, h, i: (b, i, h)),
        ],
        out_specs=pl.BlockSpec((1, tq, V_DIM), lambda b, h, i: (b, i, h)),
        out_shape=jax.ShapeDtypeStruct((B, S, D_MLA), jnp.bfloat16),
        scratch_shapes=[
            pltpu.VMEM((tq, 1), jnp.float32),
            pltpu.VMEM((tq, 1), jnp.float32),
            pltpu.VMEM((tq, V_DIM), jnp.float32),
        ],
        compiler_params=pltpu.CompilerParams(
            dimension_semantics=("arbitrary", "arbitrary", "arbitrary"),
            vmem_limit_bytes=VMEM_LIMIT_BYTES),
        name="attn",
    )(q, k, v, gm)


def _out_kernel(o_ref, pc_ref, h_ref, w_ref, b_ref, g_ref, beta_ref, y_ref):
    y = _dot(o_ref[0], w_ref[0:D_MLA, :]) + _dot(pc_ref[0], w_ref[D_MLA:, :])
    y = y + b_ref[...] + DEEPNORM_ALPHA * h_ref[0]
    y_ref[0] = _layernorm(y, g_ref[...], beta_ref[...])


def _out_call(o, pc, h, w_out, b_out, ln_g, ln_b):
    B, S, D = h.shape
    tm = OUT_TM
    tok = lambda w: pl.BlockSpec((1, tm, w), lambda b, i: (b, i, 0))
    return pl.pallas_call(
        _out_kernel,
        grid=(B, S // tm),
        in_specs=[tok(D_MLA), tok(D_POOL + D_CONV), tok(D),
                  _const_spec(w_out.shape), _const_spec((1, D)), _const_spec((1, D)),
                  _const_spec((1, D))],
        out_specs=tok(D),
        out_shape=jax.ShapeDtypeStruct((B, S, D), jnp.float32),
        compiler_params=pltpu.CompilerParams(
            dimension_semantics=("arbitrary", "arbitrary"),
            vmem_limit_bytes=VMEM_LIMIT_BYTES),
        name="out",
    )(o, pc, h, w_out, b_out, ln_g, ln_b)


def _rotate_half_cols(w):
    half = ROPE // 2
    return jnp.concatenate([-w[..., half:], w[..., :half]], axis=-1)


def _prep_w_in(w):
    o = np.cumsum((0, Q_LORA, KV_LORA, ROPE)).tolist()
    kr = w[:, o[2]:o[3]]
    return jnp.concatenate([w[:, :o[3]], _rotate_half_cols(kr), w[:, o[3]:]], axis=1).astype(jnp.bfloat16)


def _prep_w_uq(w):
    w = w.reshape(Q_LORA, N_HEADS, NOPE + ROPE)
    r = w[..., NOPE:]
    return jnp.concatenate([w, _rotate_half_cols(r)], axis=-1).reshape(Q_LORA, N_HEADS * HEAD_W).astype(jnp.bfloat16)


def kernel(x, positions, emb_ln_g, emb_ln_b, w_in, q_norm_g, kv_norm_g, w_uq, w_ukv, w_pool,
           pool_scale, conv_w, w_out, b_out, ln_g, ln_b):
    B, S, D = x.shape
    bf = jnp.bfloat16
    pos = positions.reshape(B, S, 1)
    half = ROPE // 2
    inv_freq = ROPE_THETA ** (-jnp.arange(half, dtype=jnp.float32) / half)
    invf = jnp.tile(inv_freq, LANES // half).reshape(1, LANES)
    row = lambda a: a.reshape(1, -1)

    h = x
    for l in range(DEPTH):
        first = l == 0
        outs = _proj_call(
            first, h, pos, row(emb_ln_g), row(emb_ln_b), _prep_w_in(w_in[l]), row(q_norm_g[l]),
            row(kv_norm_g[l]), _prep_w_uq(w_uq[l]), w_ukv[l].astype(bf), w_pool[l].astype(bf),
            row(pool_scale[l]), conv_w[l], invf)
        if first:
            q, k, v, gm, pc, h = outs
        else:
            q, k, v, gm, pc = outs
        o = _attn_call(q, k, v, gm)
        h = _out_call(o, pc, h, w_out[l].astype(bf), row(b_out[l]), row(ln_g[l]), row(ln_b[l]))
    return h
```

```python
import functools
import math

import jax
import jax.numpy as jnp
import numpy as np
from jax.experimental import pallas as pl
from jax.experimental.pallas import tpu as pltpu

D_MODEL = 2048
DEPTH = 2
N_HEADS = 8
NOPE = 128
ROPE = 64
V_DIM = 128
Q_LORA = 512
KV_LORA = 256
D_MLA = N_HEADS * V_DIM
ROPE_THETA = 10000.0
POOL_WINDOWS = (2, 4, 8, 16)
POOL_GROUP = 128
D_POOL = 512
D_CONV = 512
CONV_WIDTH = 3
LN_EPS = 1e-5
RMS_EPS = 1e-6
DEEPNORM_ALPHA = (2 * DEPTH) ** 0.25

LANES = 128
SUBLANES = 8
VMEM_LIMIT_BYTES = 56 * 1024 * 1024

PROJ_TM = 256
ATT_TQ = 512
ATT_TK = 512
ATT_G = 2
OUT_TM = 512
POOL_HALO = 16
CONV_HALO = 8

KT = PROJ_TM
HEAD_W = 2 * LANES
HALF = ROPE // 2

C_QLAT = 0
C_KVLAT = C_QLAT + Q_LORA
C_KR = C_KVLAT + KV_LORA
C_GMLA = C_KR + LANES
C_PIN = C_GMLA + D_MLA
C_GPOOL = C_PIN + D_POOL
C_CH = C_GPOOL + D_POOL
C_CB = C_CH + D_CONV
C_CC = C_CB + D_CONV
C_GCONV = C_CC + D_CONV
C_END = C_GCONV + D_CONV

Q_PRESCALE = (NOPE + ROPE) ** -0.5 * math.log2(math.e)
M_INIT = -1e30


def _silu(g):
    return g * (1.0 / (1.0 + jnp.exp(-g)))


def _layernorm(x, g, b):
    mu = jnp.mean(x, axis=-1, keepdims=True)
    xc = x - mu
    var = jnp.mean(xc * xc, axis=-1, keepdims=True)
    return xc * jax.lax.rsqrt(var + LN_EPS) * g + b


def _rmsnorm(x, g):
    return x * jax.lax.rsqrt(jnp.mean(x * x, axis=-1, keepdims=True) + RMS_EPS) * g


def _dot(a, b):
    return jnp.dot(a, b, preferred_element_type=jnp.float32)


def _dot_nt(a, b):
    return jax.lax.dot_general(a, b, (((1,), (1,)), ((), ())), preferred_element_type=jnp.float32)


def _proj_kernel(first, *refs):
    if first:
        (x_ref, pos_ref, posr_ref, eg_ref, eb_ref, w_in_ref, qg_ref, kvg_ref, w_uqt_ref, w_uk_ref,
         w_uvt_ref, w_pool_ref, pscale_ref, convw_ref, invf_ref, invfc_ref,
         qt_ref, k_ref, vt_ref, gm_ref, pc_ref, h_ref, pool_ext, conv_ext) = refs
    else:
        (x_ref, pos_ref, posr_ref, w_in_ref, qg_ref, kvg_ref, w_uqt_ref, w_uk_ref,
         w_uvt_ref, w_pool_ref, pscale_ref, convw_ref, invf_ref, invfc_ref,
         qt_ref, k_ref, vt_ref, gm_ref, pc_ref, pool_ext, conv_ext) = refs
    tm = PROJ_TM
    i = pl.program_id(1)

    @pl.when(i == 0)
    def _():
        pool_ext[0:POOL_HALO, :] = jnp.zeros((POOL_HALO, D_POOL), jnp.float32)
        conv_ext[0:CONV_HALO, :] = jnp.zeros((CONV_HALO, D_CONV), jnp.float32)

    @pl.when(i > 0)
    def _():
        pool_ext[0:POOL_HALO, :] = pool_ext[tm:tm + POOL_HALO, :]
        conv_ext[0:CONV_HALO, :] = conv_ext[tm:tm + CONV_HALO, :]

    x = x_ref[0]
    if first:
        x = _layernorm(x, eg_ref[...], eb_ref[...])
        h_ref[0] = x
    hb = x.astype(jnp.bfloat16)

    lat = _dot(hb, w_in_ref[:, C_QLAT:C_GMLA])
    qn = _rmsnorm(lat[:, C_QLAT:C_KVLAT], qg_ref[...]).astype(jnp.bfloat16)
    kvn = _rmsnorm(lat[:, C_KVLAT:C_KR], kvg_ref[...]).astype(jnp.bfloat16)
    kr = lat[:, C_KR:C_GMLA]

    ang = pos_ref[0].astype(jnp.float32) * invf_ref[...]
    lane = jax.lax.broadcasted_iota(jnp.int32, (tm, LANES), 1)
    t = kr * jnp.where(lane < ROPE, jnp.cos(ang), jnp.sin(ang))
    kroped = jnp.where(lane < ROPE, t + pltpu.roll(t, ROPE, axis=1), 0.0).astype(jnp.bfloat16)
    kn = _dot(kvn, w_uk_ref[...])
    for hh in range(N_HEADS):
        k_ref[0, hh] = jnp.concatenate(
            [kn[:, hh * NOPE:(hh + 1) * NOPE].astype(jnp.bfloat16), kroped], axis=1)

    vt = _dot_nt(w_uvt_ref[...], kvn)
    for hh in range(N_HEADS):
        vt_ref[0, hh, 0] = vt[hh * V_DIM:(hh + 1) * V_DIM, :].astype(jnp.bfloat16)

    qt = _dot_nt(w_uqt_ref[...], qn)
    angt = invfc_ref[...] * posr_ref[0].astype(jnp.float32)
    cos_t, sin_t = jnp.cos(angt), jnp.sin(angt)
    cos_t = jnp.concatenate([cos_t, cos_t], axis=0)
    sin_t = jnp.concatenate([sin_t, sin_t], axis=0)
    zeros_t = jnp.zeros((HEAD_W - NOPE - ROPE, tm), jnp.float32)
    for hh in range(N_HEADS):
        r0 = hh * HEAD_W
        roped = qt[r0 + NOPE:r0 + NOPE + ROPE, :] * cos_t + qt[r0 + NOPE + ROPE:r0 + HEAD_W, :] * sin_t
        qh = jnp.concatenate([qt[r0:r0 + NOPE, :], roped, zeros_t], axis=0)
        qt_ref[0, hh] = (qh * Q_PRESCALE).astype(jnp.bfloat16)

    gm_ref[0] = _silu(_dot(hb, w_in_ref[:, C_GMLA:C_PIN])).astype(jnp.bfloat16)

    pin = _dot(hb, w_in_ref[:, C_PIN:C_GPOOL])
    pool_ext[POOL_HALO:POOL_HALO + tm, :] = pin
    t1 = (i * tm + 1 + jax.lax.broadcasted_iota(jnp.int32, (tm, 1), 0)).astype(jnp.float32)
    ys = []
    for g, w in enumerate(POOL_WINDOWS):
        lo = g * POOL_GROUP
        acc = pin[:, lo:lo + POOL_GROUP]
        for s in range(1, w):
            acc = acc + pool_ext[POOL_HALO - s:POOL_HALO - s + tm, lo:lo + POOL_GROUP]
        pooled = acc / jnp.minimum(t1, float(w)) - pin[:, lo:lo + POOL_GROUP]
        ys.append(_dot(pooled.astype(jnp.bfloat16), w_pool_ref[g]))
    y_pool = jnp.concatenate(ys, axis=1) * pscale_ref[...]
    y_pool = y_pool * _silu(_dot(hb, w_in_ref[:, C_GPOOL:C_CH]))

    cv = _dot(hb, w_in_ref[:, C_CH:C_END])
    u = cv[:, 2 * D_CONV:3 * D_CONV] * cv[:, 0:D_CONV]
    conv_ext[CONV_HALO:CONV_HALO + tm, :] = u
    yc = convw_ref[2:3, :] * u
    yc = yc + convw_ref[1:2, :] * conv_ext[CONV_HALO - 1:CONV_HALO - 1 + tm, :]
    yc = yc + convw_ref[0:1, :] * conv_ext[CONV_HALO - 2:CONV_HALO - 2 + tm, :]
    y_conv = cv[:, D_CONV:2 * D_CONV] * yc * _silu(cv[:, 3 * D_CONV:4 * D_CONV])

    pc_ref[0] = jnp.concatenate([y_pool, y_conv], axis=1).astype(jnp.bfloat16)


def _const_spec(shape):
    nd = len(shape)
    return pl.BlockSpec(shape, lambda b, i: (0,) * nd, pipeline_mode=pl.Buffered(1))


def _proj_call(first, x, pos, posr, emb_g, emb_b, consts):
    B, S, D = x.shape
    tm = PROJ_TM
    grid = (B, S // tm)
    tok = lambda w: pl.BlockSpec((1, tm, w), lambda b, i: (b, i, 0))
    in_specs = [tok(D), tok(1), pl.BlockSpec((1, 1, tm), lambda b, i: (b, 0, i))]
    args = [x, pos, posr]
    if first:
        in_specs += [_const_spec((1, D)), _const_spec((1, D))]
        args += [emb_g, emb_b]
    in_specs += [_const_spec(c.shape) for c in consts]
    args += list(consts)
    bf = jnp.bfloat16
    out_shape = [
        jax.ShapeDtypeStruct((B, N_HEADS, HEAD_W, S), bf),
        jax.ShapeDtypeStruct((B, N_HEADS, S, HEAD_W), bf),
        jax.ShapeDtypeStruct((B, N_HEADS, S // KT, V_DIM, KT), bf),
        jax.ShapeDtypeStruct((B, S, D_MLA), bf),
        jax.ShapeDtypeStruct((B, S, D_POOL + D_CONV), bf),
    ]
    out_specs = [
        pl.BlockSpec((1, N_HEADS, HEAD_W, tm), lambda b, i: (b, 0, 0, i)),
        pl.BlockSpec((1, N_HEADS, tm, HEAD_W), lambda b, i: (b, 0, i, 0)),
        pl.BlockSpec((1, N_HEADS, 1, V_DIM, KT), lambda b, i: (b, 0, i, 0, 0)),
        tok(D_MLA), tok(D_POOL + D_CONV)]
    if first:
        out_shape.append(jax.ShapeDtypeStruct((B, S, D), jnp.float32))
        out_specs.append(tok(D))
    return pl.pallas_call(
        functools.partial(_proj_kernel, first),
        grid=grid,
        in_specs=in_specs,
        out_specs=out_specs,
        out_shape=out_shape,
        scratch_shapes=[
            pltpu.VMEM((POOL_HALO + tm, D_POOL), jnp.float32),
            pltpu.VMEM((CONV_HALO + tm, D_CONV), jnp.float32),
        ],
        compiler_params=pltpu.CompilerParams(
            dimension_semantics=("arbitrary", "arbitrary"),
            vmem_limit_bytes=VMEM_LIMIT_BYTES),
        name="proj_first" if first else "proj",
    )(*args)


def _attn_kernel(qt_ref, k_ref, vt_ref, gm_ref, o_ref, s0_sc, s1_sc, m_sc, l_sc, acc_sc):
    tq, tk, G = ATT_TQ, ATT_TK, ATT_G
    sub = tk // KT
    i = pl.program_id(2)

    def scores_into(s_ref, n):
        start = pl.multiple_of(n * tk, tk)
        for g in range(G):
            s_ref[g] = _dot(k_ref[0, g, pl.ds(start, tk), :], qt_ref[0, g])

    def pv(g, n, p):
        out = _dot(vt_ref[0, g, n * sub], p[0:KT])
        for c in range(1, sub):
            out = out + _dot(vt_ref[0, g, n * sub + c], p[c * KT:(c + 1) * KT])
        return out

    def update(g, n, s):
        m_prev = m_sc[g]
        m_new = jnp.maximum(m_prev, jnp.max(s, axis=0, keepdims=True))
        alpha = jnp.exp2(m_prev - m_new)
        p = jnp.exp2(s - m_new)
        l_sc[g] = alpha * l_sc[g] + jnp.sum(p, axis=0, keepdims=True)
        acc_sc[g] = alpha * acc_sc[g] + pv(g, n, p.astype(jnp.bfloat16))
        m_sc[g] = m_new

    m_sc[...] = jnp.full(m_sc.shape, M_INIT, jnp.float32)
    l_sc[...] = jnp.zeros(l_sc.shape, jnp.float32)
    acc_sc[...] = jnp.zeros(acc_sc.shape, jnp.float32)
    scores_into(s0_sc, 0)

    def full_tile(cur_ref, nxt_ref, n):
        scores_into(nxt_ref, n + 1)
        for g in range(G):
            update(g, n, cur_ref[g])

    def body(pair, carry):
        n = 2 * pair
        full_tile(s0_sc, s1_sc, n)

        @pl.when(n + 1 < i)
        def _():
            full_tile(s1_sc, s0_sc, n + 1)
        return carry

    jax.lax.fori_loop(0, (i + 1) // 2, body, 0)

    def diag_tile(cur_ref):
        key = jax.lax.broadcasted_iota(jnp.int32, (tk, tq), 0)
        qry = jax.lax.broadcasted_iota(jnp.int32, (tk, tq), 1)
        for g in range(G):
            update(g, i, jnp.where(key <= qry, cur_ref[g], -jnp.inf))

    @pl.when(i % 2 == 0)
    def _():
        diag_tile(s0_sc)

    @pl.when(i % 2 == 1)
    def _():
        diag_tile(s1_sc)

    for g in range(G):
        out_t = acc_sc[g] / l_sc[g]
        out = out_t.T * gm_ref[0, :, g * V_DIM:(g + 1) * V_DIM].astype(jnp.float32)
        o_ref[0, :, g * V_DIM:(g + 1) * V_DIM] = out.astype(o_ref.dtype)


def _attn_call(qt, k, vt, gm):
    B, H, S, _ = k.shape
    tq, G = ATT_TQ, ATT_G
    assert ATT_TQ == ATT_TK and ATT_TK % KT == 0
    grid = (B, H // G, S // tq)
    return pl.pallas_call(
        _attn_kernel,
        grid=grid,
        in_specs=[
            pl.BlockSpec((1, G, HEAD_W, tq), lambda b, h, i: (b, h, 0, i)),
            pl.BlockSpec((1, G, S, HEAD_W), lambda b, h, i: (b, h, 0, 0)),
            pl.BlockSpec((1, G, S // KT, V_DIM, KT), lambda b, h, i: (b, h, 0, 0, 0)),
            pl.BlockSpec((1, tq, G * V_DIM), lambda b, h, i: (b, i, h)),
        ],
        out_specs=pl.BlockSpec((1, tq, G * V_DIM), lambda b, h, i: (b, i, h)),
        out_shape=jax.ShapeDtypeStruct((B, S, D_MLA), jnp.bfloat16),
        scratch_shapes=[
            pltpu.VMEM((G, ATT_TK, tq), jnp.float32),
            pltpu.VMEM((G, ATT_TK, tq), jnp.float32),
            pltpu.VMEM((G, 1, tq), jnp.float32),
            pltpu.VMEM((G, 1, tq), jnp.float32),
            pltpu.VMEM((G, V_DIM, tq), jnp.float32),
        ],
        compiler_params=pltpu.CompilerParams(
            dimension_semantics=("arbitrary", "arbitrary", "arbitrary"),
            vmem_limit_bytes=VMEM_LIMIT_BYTES),
        name="attn",
    )(qt, k, vt, gm)


def _out_kernel(o_ref, pc_ref, h_ref, w_ref, b_ref, g_ref, beta_ref, y_ref):
    y = _dot(o_ref[0], w_ref[0:D_MLA, :]) + _dot(pc_ref[0], w_ref[D_MLA:, :])
    y = y + b_ref[...] + DEEPNORM_ALPHA * h_ref[0]
    y_ref[0] = _layernorm(y, g_ref[...], beta_ref[...])


def _out_call(o, pc, h, w_out, b_out, ln_g, ln_b):
    B, S, D = h.shape
    tm = OUT_TM
    tok = lambda w: pl.BlockSpec((1, tm, w), lambda b, i: (b, i, 0))
    return pl.pallas_call(
        _out_kernel,
        grid=(B, S // tm),
        in_specs=[tok(D_MLA), tok(D_POOL + D_CONV), tok(D),
                  _const_spec(w_out.shape), _const_spec((1, D)), _const_spec((1, D)),
                  _const_spec((1, D))],
        out_specs=tok(D),
        out_shape=jax.ShapeDtypeStruct((B, S, D), jnp.float32),
        compiler_params=pltpu.CompilerParams(
            dimension_semantics=("arbitrary", "arbitrary"),
            vmem_limit_bytes=VMEM_LIMIT_BYTES),
        name="out",
    )(o, pc, h, w_out, b_out, ln_g, ln_b)


def _rotate_half_cols(w):
    return jnp.concatenate([-w[..., HALF:], w[..., :HALF]], axis=-1)


def _prep_w_in(w):
    o = np.cumsum((0, Q_LORA, KV_LORA, ROPE)).tolist()
    kr = w[:, o[2]:o[3]]
    return jnp.concatenate([w[:, :o[3]], _rotate_half_cols(kr), w[:, o[3]:]], axis=1).astype(jnp.bfloat16)


def _prep_w_uq_t(w):
    w = w.reshape(Q_LORA, N_HEADS, NOPE + ROPE)
    w = jnp.concatenate([w, _rotate_half_cols(w[..., NOPE:])], axis=-1)
    return w.reshape(Q_LORA, N_HEADS * HEAD_W).T.astype(jnp.bfloat16)


def _prep_w_ukv(w):
    w = w.reshape(KV_LORA, N_HEADS, NOPE + V_DIM)
    wk = w[..., :NOPE].reshape(KV_LORA, N_HEADS * NOPE)
    wv = w[..., NOPE:].reshape(KV_LORA, N_HEADS * V_DIM)
    return wk.astype(jnp.bfloat16), wv.T.astype(jnp.bfloat16)


def kernel(x, positions, emb_ln_g, emb_ln_b, w_in, q_norm_g, kv_norm_g, w_uq, w_ukv, w_pool,
           pool_scale, conv_w, w_out, b_out, ln_g, ln_b):
    B, S, D = x.shape
    bf = jnp.bfloat16
    pos = positions.reshape(B, S, 1)
    posr = positions.reshape(B, 1, S)
    inv_freq = ROPE_THETA ** (-jnp.arange(HALF, dtype=jnp.float32) / HALF)
    invf = jnp.tile(inv_freq, LANES // HALF).reshape(1, LANES)
    invfc = inv_freq.reshape(HALF, 1)
    row = lambda a: a.reshape(1, -1)

    h = x
    for l in range(DEPTH):
        first = l == 0
        w_uk, w_uvt = _prep_w_ukv(w_ukv[l])
        consts = (_prep_w_in(w_in[l]), row(q_norm_g[l]), row(kv_norm_g[l]), _prep_w_uq_t(w_uq[l]),
                  w_uk, w_uvt, w_pool[l].astype(bf), row(pool_scale[l]), conv_w[l], invf, invfc)
        outs = _proj_call(first, h, pos, posr, row(emb_ln_g), row(emb_ln_b), consts)
        if first:
            qt, k, vt, gm, pc, h = outs
        else:
            qt, k, vt, gm, pc = outs
        o = _attn_call(qt, k, vt, gm)
        h = _out_call(o, pc, h, w_out[l].astype(bf), row(b_out[l]), row(ln_g[l]), row(ln_b[l]))
    return h
```

```python
import functools
import math

import jax
import jax.numpy as jnp
import numpy as np
from jax.experimental import pallas as pl
from jax.experimental.pallas import tpu as pltpu

D_MODEL = 2048
DEPTH = 2
N_HEADS = 8
NOPE = 128
ROPE = 64
V_DIM = 128
Q_LORA = 512
KV_LORA = 256
D_MLA = N_HEADS * V_DIM
ROPE_THETA = 10000.0
POOL_WINDOWS = (2, 4, 8, 16)
POOL_GROUP = 128
D_POOL = 512
D_CONV = 512
CONV_WIDTH = 3
LN_EPS = 1e-5
RMS_EPS = 1e-6
DEEPNORM_ALPHA = (2 * DEPTH) ** 0.25

LANES = 128
SUBLANES = 8
VMEM_LIMIT_BYTES = 56 * 1024 * 1024

PROJ_TM = 256
ATT_TQ = 512
ATT_TK = 512
ATT_G = 2
ATT_CH = 64
OUT_TM = 512
POOL_HALO = 16
CONV_HALO = 8

KT = PROJ_TM
HEAD_W = 2 * LANES
HALF = ROPE // 2

C_QLAT = 0
C_KVLAT = C_QLAT + Q_LORA
C_KR = C_KVLAT + KV_LORA
C_LAT_END = C_KR + LANES
R_GMLA = 0
R_PIN = R_GMLA + D_MLA
R_GPOOL = R_PIN + D_POOL
R_CH = R_GPOOL + D_POOL
R_END = R_CH + 4 * D_CONV

Q_PRESCALE = (NOPE + ROPE) ** -0.5 * math.log2(math.e)
M_INIT = -1e30


def _silu(g):
    return g * (1.0 / (1.0 + jnp.exp(-g)))


def _layernorm(x, g, b):
    mu = jnp.mean(x, axis=-1, keepdims=True)
    xc = x - mu
    var = jnp.mean(xc * xc, axis=-1, keepdims=True)
    return xc * jax.lax.rsqrt(var + LN_EPS) * g + b


def _rmsnorm(x, g):
    return x * jax.lax.rsqrt(jnp.mean(x * x, axis=-1, keepdims=True) + RMS_EPS) * g


def _dot(a, b):
    return jnp.dot(a, b, preferred_element_type=jnp.float32)


def _dot_nt(a, b):
    return jax.lax.dot_general(a, b, (((1,), (1,)), ((), ())), preferred_element_type=jnp.float32)


def _proj_kernel(first, *refs):
    if first:
        (x_ref, pos_ref, posr_ref, eg_ref, eb_ref, w_lat_ref, w_mix_ref, qg_ref, kvg_ref, w_uqt_ref, w_uk_ref,
         w_uvt_ref, w_pool_ref, pscale_ref, convw_ref, invf_ref, invfc_ref,
         qt_ref, k_ref, vt_ref, gm_ref, pc_ref, h_ref, pool_ext, conv_ext) = refs
    else:
        (x_ref, pos_ref, posr_ref, w_lat_ref, w_mix_ref, qg_ref, kvg_ref, w_uqt_ref, w_uk_ref,
         w_uvt_ref, w_pool_ref, pscale_ref, convw_ref, invf_ref, invfc_ref,
         qt_ref, k_ref, vt_ref, gm_ref, pc_ref, pool_ext, conv_ext) = refs
    tm = PROJ_TM
    i = pl.program_id(1)

    @pl.when(i == 0)
    def _():
        pool_ext[0:POOL_HALO, :] = jnp.zeros((POOL_HALO, D_POOL), jnp.float32)
        conv_ext[0:CONV_HALO, :] = jnp.zeros((CONV_HALO, D_CONV), jnp.float32)

    @pl.when(i > 0)
    def _():
        pool_ext[0:POOL_HALO, :] = pool_ext[tm:tm + POOL_HALO, :]
        conv_ext[0:CONV_HALO, :] = conv_ext[tm:tm + CONV_HALO, :]

    x = x_ref[0]
    if first:
        x = _layernorm(x, eg_ref[...], eb_ref[...])
        h_ref[0] = x
    hb = x.astype(jnp.bfloat16)

    lat = _dot(hb, w_lat_ref[...])
    qn = _rmsnorm(lat[:, C_QLAT:C_KVLAT], qg_ref[...]).astype(jnp.bfloat16)
    kvn = _rmsnorm(lat[:, C_KVLAT:C_KR], kvg_ref[...]).astype(jnp.bfloat16)
    kr = lat[:, C_KR:C_LAT_END]

    ang = pos_ref[0].astype(jnp.float32) * invf_ref[...]
    lane = jax.lax.broadcasted_iota(jnp.int32, (tm, LANES), 1)
    t = kr * jnp.where(lane < ROPE, jnp.cos(ang), jnp.sin(ang))
    kroped = jnp.where(lane < ROPE, t + pltpu.roll(t, ROPE, axis=1), 0.0).astype(jnp.bfloat16)
    kn = _dot(kvn, w_uk_ref[...])
    for hh in range(N_HEADS):
        k_ref[0, hh] = jnp.concatenate(
            [kn[:, hh * NOPE:(hh + 1) * NOPE].astype(jnp.bfloat16), kroped], axis=1)

    vt = _dot_nt(w_uvt_ref[...], kvn)
    for hh in range(N_HEADS):
        vt_ref[0, hh, 0] = vt[hh * V_DIM:(hh + 1) * V_DIM, :].astype(jnp.bfloat16)

    qt = _dot_nt(w_uqt_ref[...], qn)
    angt = invfc_ref[...] * posr_ref[0].astype(jnp.float32)
    cos_t, sin_t = jnp.cos(angt), jnp.sin(angt)
    cos_t = jnp.concatenate([cos_t, cos_t], axis=0)
    sin_t = jnp.concatenate([sin_t, sin_t], axis=0)
    zeros_t = jnp.zeros((HEAD_W - NOPE - ROPE, tm), jnp.float32)
    for hh in range(N_HEADS):
        r0 = hh * HEAD_W
        roped = qt[r0 + NOPE:r0 + NOPE + ROPE, :] * cos_t + qt[r0 + NOPE + ROPE:r0 + HEAD_W, :] * sin_t
        qh = jnp.concatenate([qt[r0:r0 + NOPE, :], roped, zeros_t], axis=0)
        qt_ref[0, hh] = (qh * Q_PRESCALE).astype(jnp.bfloat16)

    gm_ref[0] = _silu(_dot(hb, w_mix_ref[:, R_GMLA:R_PIN])).astype(jnp.bfloat16)

    pin = _dot(hb, w_mix_ref[:, R_PIN:R_GPOOL])
    pool_ext[POOL_HALO:POOL_HALO + tm, :] = pin
    t1 = (i * tm + 1 + jax.lax.broadcasted_iota(jnp.int32, (tm, 1), 0)).astype(jnp.float32)
    ys = []
    for g, w in enumerate(POOL_WINDOWS):
        lo = g * POOL_GROUP
        acc = pin[:, lo:lo + POOL_GROUP]
        for s in range(1, w):
            acc = acc + pool_ext[POOL_HALO - s:POOL_HALO - s + tm, lo:lo + POOL_GROUP]
        pooled = acc / jnp.minimum(t1, float(w)) - pin[:, lo:lo + POOL_GROUP]
        ys.append(_dot(pooled.astype(jnp.bfloat16), w_pool_ref[g]))
    y_pool = jnp.concatenate(ys, axis=1) * pscale_ref[...]
    y_pool = y_pool * _silu(_dot(hb, w_mix_ref[:, R_GPOOL:R_CH]))

    cv = _dot(hb, w_mix_ref[:, R_CH:R_END])
    u = cv[:, 2 * D_CONV:3 * D_CONV] * cv[:, 0:D_CONV]
    conv_ext[CONV_HALO:CONV_HALO + tm, :] = u
    yc = convw_ref[2:3, :] * u
    yc = yc + convw_ref[1:2, :] * conv_ext[CONV_HALO - 1:CONV_HALO - 1 + tm, :]
    yc = yc + convw_ref[0:1, :] * conv_ext[CONV_HALO - 2:CONV_HALO - 2 + tm, :]
    y_conv = cv[:, D_CONV:2 * D_CONV] * yc * _silu(cv[:, 3 * D_CONV:4 * D_CONV])

    pc_ref[0] = jnp.concatenate([y_pool, y_conv], axis=1).astype(jnp.bfloat16)


def _const_spec(shape):
    nd = len(shape)
    return pl.BlockSpec(shape, lambda b, i: (0,) * nd, pipeline_mode=pl.Buffered(1))


def _proj_call(first, x, pos, posr, emb_g, emb_b, consts):
    B, S, D = x.shape
    tm = PROJ_TM
    grid = (B, S // tm)
    tok = lambda w: pl.BlockSpec((1, tm, w), lambda b, i: (b, i, 0))
    in_specs = [tok(D), tok(1), pl.BlockSpec((1, 1, tm), lambda b, i: (b, 0, i))]
    args = [x, pos, posr]
    if first:
        in_specs += [_const_spec((1, D)), _const_spec((1, D))]
        args += [emb_g, emb_b]
    in_specs += [_const_spec(c.shape) for c in consts]
    args += list(consts)
    bf = jnp.bfloat16
    out_shape = [
        jax.ShapeDtypeStruct((B, N_HEADS, HEAD_W, S), bf),
        jax.ShapeDtypeStruct((B, N_HEADS, S, HEAD_W), bf),
        jax.ShapeDtypeStruct((B, N_HEADS, S // KT, V_DIM, KT), bf),
        jax.ShapeDtypeStruct((B, S, D_MLA), bf),
        jax.ShapeDtypeStruct((B, S, D_POOL + D_CONV), bf),
    ]
    out_specs = [
        pl.BlockSpec((1, N_HEADS, HEAD_W, tm), lambda b, i: (b, 0, 0, i)),
        pl.BlockSpec((1, N_HEADS, tm, HEAD_W), lambda b, i: (b, 0, i, 0)),
        pl.BlockSpec((1, N_HEADS, 1, V_DIM, KT), lambda b, i: (b, 0, i, 0, 0)),
        tok(D_MLA), tok(D_POOL + D_CONV)]
    if first:
        out_shape.append(jax.ShapeDtypeStruct((B, S, D), jnp.float32))
        out_specs.append(tok(D))
    return pl.pallas_call(
        functools.partial(_proj_kernel, first),
        grid=grid,
        in_specs=in_specs,
        out_specs=out_specs,
        out_shape=out_shape,
        scratch_shapes=[
            pltpu.VMEM((POOL_HALO + tm, D_POOL), jnp.float32),
            pltpu.VMEM((CONV_HALO + tm, D_CONV), jnp.float32),
        ],
        compiler_params=pltpu.CompilerParams(
            dimension_semantics=("arbitrary", "arbitrary"),
            vmem_limit_bytes=VMEM_LIMIT_BYTES),
        name="proj_first" if first else "proj",
    )(*args)


def _rows8_reduce(x, op):
    acc = x[0:SUBLANES]
    for r in range(SUBLANES, x.shape[0], SUBLANES):
        acc = op(acc, x[r:r + SUBLANES])
    return acc


def _attn_kernel(qt_ref, k_ref, vt_ref, gm_ref, o_ref,
                 s0_sc, s1_sc, mt0_sc, mt1_sc, p_sc, m_sc, l_sc, acc_sc):
    tq, tk, G, ch = ATT_TQ, ATT_TK, ATT_G, ATT_CH
    sub = tk // KT
    i = pl.program_id(2)

    def scores_into(s_ref, mt_ref, n):
        start = pl.multiple_of(n * tk, tk)
        for g in range(G):
            s = _dot(k_ref[0, g, pl.ds(start, tk), :], qt_ref[0, g])
            s_ref[g] = s
            mt_ref[g] = _rows8_reduce(s, jnp.maximum)

    def update(g, n, s_ref, m_tile8, mask):
        m_prev = m_sc[g]
        m_new = jnp.maximum(m_prev, jnp.max(m_tile8, axis=0, keepdims=True))
        alpha = jnp.exp2(m_prev - m_new)
        lsum = jnp.zeros((SUBLANES, tq), jnp.float32)
        for c in range(tk // ch):
            s_c = s_ref[g, c * ch:(c + 1) * ch, :]
            if mask is not None:
                s_c = jnp.where(mask(c), s_c, -jnp.inf)
            p_c = jnp.exp2(s_c - m_new)
            lsum = lsum + _rows8_reduce(p_c, jnp.add)
            p_sc[g, c * ch:(c + 1) * ch, :] = p_c.astype(jnp.bfloat16)
        l_sc[g] = alpha * l_sc[g] + jnp.sum(lsum, axis=0, keepdims=True)
        vt = jnp.concatenate([vt_ref[0, g, n * sub + c] for c in range(sub)], axis=1)
        acc_sc[g] = alpha * acc_sc[g] + _dot(vt, p_sc[g])
        m_sc[g] = m_new

    m_sc[...] = jnp.full(m_sc.shape, M_INIT, jnp.float32)
    l_sc[...] = jnp.zeros(l_sc.shape, jnp.float32)
    acc_sc[...] = jnp.zeros(acc_sc.shape, jnp.float32)
    scores_into(s0_sc, mt0_sc, 0)

    def full_tile(cur, nxt, n):
        scores_into(nxt[0], nxt[1], n + 1)
        for g in range(G):
            update(g, n, cur[0], cur[1][g], None)

    even, odd = (s0_sc, mt0_sc), (s1_sc, mt1_sc)

    def body(pair, carry):
        n = 2 * pair
        full_tile(even, odd, n)

        @pl.when(n + 1 < i)
        def _():
            full_tile(odd, even, n + 1)
        return carry

    jax.lax.fori_loop(0, (i + 1) // 2, body, 0)

    def diag_tile(cur):
        def mask(c):
            key = c * ch + jax.lax.broadcasted_iota(jnp.int32, (ch, tq), 0)
            return key <= jax.lax.broadcasted_iota(jnp.int32, (ch, tq), 1)
        for g in range(G):
            m8 = jnp.full((SUBLANES, tq), -jnp.inf, jnp.float32)
            for c in range(tk // ch):
                s_c = jnp.where(mask(c), cur[0][g, c * ch:(c + 1) * ch, :], -jnp.inf)
                m8 = jnp.maximum(m8, _rows8_reduce(s_c, jnp.maximum))
            update(g, i, cur[0], m8, mask)

    @pl.when(i % 2 == 0)
    def _():
        diag_tile(even)

    @pl.when(i % 2 == 1)
    def _():
        diag_tile(odd)

    for g in range(G):
        out_t = acc_sc[g] / l_sc[g]
        out = out_t.T * gm_ref[0, :, g * V_DIM:(g + 1) * V_DIM].astype(jnp.float32)
        o_ref[0, :, g * V_DIM:(g + 1) * V_DIM] = out.astype(o_ref.dtype)


def _attn_call(qt, k, vt, gm):
    B, H, S, _ = k.shape
    tq, G = ATT_TQ, ATT_G
    assert ATT_TQ == ATT_TK and ATT_TK % KT == 0
    grid = (B, H // G, S // tq)
    return pl.pallas_call(
        _attn_kernel,
        grid=grid,
        in_specs=[
            pl.BlockSpec((1, G, HEAD_W, tq), lambda b, h, i: (b, h, 0, i)),
            pl.BlockSpec((1, G, S, HEAD_W), lambda b, h, i: (b, h, 0, 0)),
            pl.BlockSpec((1, G, S // KT, V_DIM, KT), lambda b, h, i: (b, h, 0, 0, 0)),
            pl.BlockSpec((1, tq, G * V_DIM), lambda b, h, i: (b, i, h)),
        ],
        out_specs=pl.BlockSpec((1, tq, G * V_DIM), lambda b, h, i: (b, i, h)),
        out_shape=jax.ShapeDtypeStruct((B, S, D_MLA), jnp.bfloat16),
        scratch_shapes=[
            pltpu.VMEM((G, ATT_TK, tq), jnp.float32),
            pltpu.VMEM((G, ATT_TK, tq), jnp.float32),
            pltpu.VMEM((G, SUBLANES, tq), jnp.float32),
            pltpu.VMEM((G, SUBLANES, tq), jnp.float32),
            pltpu.VMEM((G, ATT_TK, tq), jnp.bfloat16),
            pltpu.VMEM((G, 1, tq), jnp.float32),
            pltpu.VMEM((G, 1, tq), jnp.float32),
            pltpu.VMEM((G, V_DIM, tq), jnp.float32),
        ],
        compiler_params=pltpu.CompilerParams(
            dimension_semantics=("arbitrary", "arbitrary", "arbitrary"),
            vmem_limit_bytes=VMEM_LIMIT_BYTES),
        name="attn",
    )(qt, k, vt, gm)


def _out_kernel(o_ref, pc_ref, h_ref, w_ref, b_ref, g_ref, beta_ref, y_ref):
    y = _dot(o_ref[0], w_ref[0:D_MLA, :]) + _dot(pc_ref[0], w_ref[D_MLA:, :])
    y = y + b_ref[...] + DEEPNORM_ALPHA * h_ref[0]
    y_ref[0] = _layernorm(y, g_ref[...], beta_ref[...])


def _out_call(o, pc, h, w_out, b_out, ln_g, ln_b):
    B, S, D = h.shape
    tm = OUT_TM
    tok = lambda w: pl.BlockSpec((1, tm, w), lambda b, i: (b, i, 0))
    return pl.pallas_call(
        _out_kernel,
        grid=(B, S // tm),
        in_specs=[tok(D_MLA), tok(D_POOL + D_CONV), tok(D),
                  _const_spec(w_out.shape), _const_spec((1, D)), _const_spec((1, D)),
                  _const_spec((1, D))],
        out_specs=tok(D),
        out_shape=jax.ShapeDtypeStruct((B, S, D), jnp.float32),
        compiler_params=pltpu.CompilerParams(
            dimension_semantics=("arbitrary", "arbitrary"),
            vmem_limit_bytes=VMEM_LIMIT_BYTES),
        name="out",
    )(o, pc, h, w_out, b_out, ln_g, ln_b)


def _rotate_half_cols(w):
    return jnp.concatenate([-w[..., HALF:], w[..., :HALF]], axis=-1)


def _prep_w_in(w):
    n_lat = Q_LORA + KV_LORA + ROPE
    kr = w[..., n_lat - ROPE:n_lat]
    w_lat = jnp.concatenate([w[..., :n_lat], _rotate_half_cols(kr)], axis=-1).astype(jnp.bfloat16)
    return w_lat, w[..., n_lat:].astype(jnp.bfloat16)


def _prep_w_uq_t(w):
    L = w.shape[0]
    w = w.reshape(L, Q_LORA, N_HEADS, NOPE + ROPE)
    w = jnp.concatenate([w, _rotate_half_cols(w[..., NOPE:])], axis=-1)
    return jnp.swapaxes(w.reshape(L, Q_LORA, N_HEADS * HEAD_W), 1, 2).astype(jnp.bfloat16)


def _prep_w_ukv(w):
    L = w.shape[0]
    w = w.reshape(L, KV_LORA, N_HEADS, NOPE + V_DIM)
    wk = w[..., :NOPE].reshape(L, KV_LORA, N_HEADS * NOPE)
    wv = w[..., NOPE:].reshape(L, KV_LORA, N_HEADS * V_DIM)
    return wk.astype(jnp.bfloat16), jnp.swapaxes(wv, 1, 2).astype(jnp.bfloat16)


def kernel(x, positions, emb_ln_g, emb_ln_b, w_in, q_norm_g, kv_norm_g, w_uq, w_ukv, w_pool,
           pool_scale, conv_w, w_out, b_out, ln_g, ln_b):
    B, S, D = x.shape
    bf = jnp.bfloat16
    pos = positions.reshape(B, S, 1)
    posr = positions.reshape(B, 1, S)
    inv_freq = ROPE_THETA ** (-jnp.arange(HALF, dtype=jnp.float32) / HALF)
    invf = jnp.tile(inv_freq, LANES // HALF).reshape(1, LANES)
    invfc = inv_freq.reshape(HALF, 1)
    row = lambda a: a.reshape(1, -1)

    w_lat, w_mix = _prep_w_in(w_in)
    w_uqt = _prep_w_uq_t(w_uq)
    w_uk, w_uvt = _prep_w_ukv(w_ukv)
    w_pool_b = w_pool.astype(bf)
    w_out_b = w_out.astype(bf)

    h = x
    for l in range(DEPTH):
        first = l == 0
        consts = (w_lat[l], w_mix[l], row(q_norm_g[l]), row(kv_norm_g[l]), w_uqt[l],
                  w_uk[l], w_uvt[l], w_pool_b[l], row(pool_scale[l]), conv_w[l], invf, invfc)
        outs = _proj_call(first, h, pos, posr, row(emb_ln_g), row(emb_ln_b), consts)
        if first:
            qt, k, vt, gm, pc, h = outs
        else:
            qt, k, vt, gm, pc = outs
        o = _attn_call(qt, k, vt, gm)
        h = _out_call(o, pc, h, w_out_b[l], row(b_out[l]), row(ln_g[l]), row(ln_b[l]))
    return h
```

```python
import functools
import math

import jax
import jax.numpy as jnp
import numpy as np
from jax.experimental import pallas as pl
from jax.experimental.pallas import tpu as pltpu

D_MODEL = 2048
DEPTH = 2
N_HEADS = 8
NOPE = 128
ROPE = 64
V_DIM = 128
Q_LORA = 512
KV_LORA = 256
D_MLA = N_HEADS * V_DIM
ROPE_THETA = 10000.0
POOL_WINDOWS = (2, 4, 8, 16)
POOL_GROUP = 128
D_POOL = 512
D_CONV = 512
CONV_WIDTH = 3
LN_EPS = 1e-5
RMS_EPS = 1e-6
DEEPNORM_ALPHA = (2 * DEPTH) ** 0.25

LANES = 128
SUBLANES = 8
VMEM_LIMIT_BYTES = 56 * 1024 * 1024

PROJ_TM = 256
ATT_TQ = 512
ATT_TK = 512
ATT_G = 2
ATT_CH = 64
OUT_TM = 512
OUT_CH = 128
POOL_HALO = 16
CONV_HALO = 8

KT = PROJ_TM
HEAD_W = 2 * LANES
HALF = ROPE // 2

C_QLAT = 0
C_KVLAT = C_QLAT + Q_LORA
C_KR = C_KVLAT + KV_LORA
C_LAT_END = C_KR + LANES
R_GMLA = 0
R_PIN = R_GMLA + D_MLA
R_GPOOL = R_PIN + D_POOL
R_CH = R_GPOOL + D_POOL
R_END = R_CH + 4 * D_CONV

Q_PRESCALE = (NOPE + ROPE) ** -0.5 * math.log2(math.e)
M_INIT = -1e30


def _silu(g):
    return g * (1.0 / (1.0 + jnp.exp(-g)))


def _layernorm(x, g, b):
    mu = jnp.mean(x, axis=-1, keepdims=True)
    xc = x - mu
    var = jnp.mean(xc * xc, axis=-1, keepdims=True)
    return xc * jax.lax.rsqrt(var + LN_EPS) * g + b


def _rmsnorm(x, g):
    return x * jax.lax.rsqrt(jnp.mean(x * x, axis=-1, keepdims=True) + RMS_EPS) * g


def _dot(a, b):
    return jnp.dot(a, b, preferred_element_type=jnp.float32)


def _dot_nt(a, b):
    return jax.lax.dot_general(a, b, (((1,), (1,)), ((), ())), preferred_element_type=jnp.float32)


def _proj_kernel(first, *refs):
    if first:
        (x_ref, posr_ref, eg_ref, eb_ref, w_lat_ref, w_mix_ref, qg_ref, kvg_ref, w_uqt_ref, w_uk_ref,
         w_uvt_ref, w_pool_ref, pscale_ref, convw_ref, invfc_ref,
         qt_ref, k_ref, vt_ref, gm_ref, pc_ref, h_ref, pool_ext, conv_ext) = refs
    else:
        (x_ref, posr_ref, w_lat_ref, w_mix_ref, qg_ref, kvg_ref, w_uqt_ref, w_uk_ref,
         w_uvt_ref, w_pool_ref, pscale_ref, convw_ref, invfc_ref,
         qt_ref, k_ref, vt_ref, gm_ref, pc_ref, pool_ext, conv_ext) = refs
    tm = PROJ_TM
    i = pl.program_id(1)

    @pl.when(i == 0)
    def _():
        pool_ext[0:POOL_HALO, :] = jnp.zeros((POOL_HALO, D_POOL), jnp.float32)
        conv_ext[0:CONV_HALO, :] = jnp.zeros((CONV_HALO, D_CONV), jnp.float32)

    @pl.when(i > 0)
    def _():
        pool_ext[0:POOL_HALO, :] = pool_ext[tm:tm + POOL_HALO, :]
        conv_ext[0:CONV_HALO, :] = conv_ext[tm:tm + CONV_HALO, :]

    x = x_ref[0]
    if first:
        x = _layernorm(x, eg_ref[...], eb_ref[...])
        h_ref[0] = x
    hb = x.astype(jnp.bfloat16)

    lat = _dot(hb, w_lat_ref[...])
    pin = _dot(hb, w_mix_ref[:, R_PIN:R_GPOOL])
    cv = _dot(hb, w_mix_ref[:, R_CH:R_END])

    qn = _rmsnorm(lat[:, C_QLAT:C_KVLAT], qg_ref[...]).astype(jnp.bfloat16)
    kvn = _rmsnorm(lat[:, C_KVLAT:C_KR], kvg_ref[...]).astype(jnp.bfloat16)
    kr = lat[:, C_KR:C_LAT_END]
    angt = invfc_ref[...] * posr_ref[0].astype(jnp.float32)
    cos_t, sin_t = jnp.cos(angt), jnp.sin(angt)
    cs = jnp.concatenate([cos_t, cos_t, sin_t, sin_t], axis=0).T
    lane = jax.lax.broadcasted_iota(jnp.int32, (tm, LANES), 1)
    t = kr * cs
    kroped = jnp.where(lane < ROPE, t + pltpu.roll(t, ROPE, axis=1), 0.0).astype(jnp.bfloat16)

    kn = _dot(kvn, w_uk_ref[...])
    vt = _dot_nt(w_uvt_ref[...], kvn)
    qt = _dot_nt(w_uqt_ref[...], qn)
    gpool = _dot(hb, w_mix_ref[:, R_GPOOL:R_CH])

    for hh in range(N_HEADS):
        k_ref[0, hh] = jnp.concatenate(
            [kn[:, hh * NOPE:(hh + 1) * NOPE].astype(jnp.bfloat16), kroped], axis=1)
        vt_ref[0, hh, 0] = vt[hh * V_DIM:(hh + 1) * V_DIM, :].astype(jnp.bfloat16)

    u = cv[:, 2 * D_CONV:3 * D_CONV] * cv[:, 0:D_CONV]
    conv_ext[CONV_HALO:CONV_HALO + tm, :] = u
    yc = convw_ref[2:3, :] * u
    yc = yc + convw_ref[1:2, :] * conv_ext[CONV_HALO - 1:CONV_HALO - 1 + tm, :]
    yc = yc + convw_ref[0:1, :] * conv_ext[CONV_HALO - 2:CONV_HALO - 2 + tm, :]
    y_conv = cv[:, D_CONV:2 * D_CONV] * yc * _silu(cv[:, 3 * D_CONV:4 * D_CONV])
    pc_ref[0, :, D_POOL:] = y_conv.astype(jnp.bfloat16)

    pool_ext[POOL_HALO:POOL_HALO + tm, :] = pin
    t1 = (i * tm + 1 + jax.lax.broadcasted_iota(jnp.int32, (tm, 1), 0)).astype(jnp.float32)
    ys = []
    for g, w in enumerate(POOL_WINDOWS):
        lo = g * POOL_GROUP
        acc = pin[:, lo:lo + POOL_GROUP]
        for s in range(1, w):
            acc = acc + pool_ext[POOL_HALO - s:POOL_HALO - s + tm, lo:lo + POOL_GROUP]
        pooled = acc / jnp.minimum(t1, float(w)) - pin[:, lo:lo + POOL_GROUP]
        ys.append(_dot(pooled.astype(jnp.bfloat16), w_pool_ref[g]))
    gm = _dot(hb, w_mix_ref[:, R_GMLA:R_PIN])
    y_pool = jnp.concatenate(ys, axis=1) * pscale_ref[...] * _silu(gpool)
    pc_ref[0, :, :D_POOL] = y_pool.astype(jnp.bfloat16)

    cos_t = jnp.concatenate([cos_t, cos_t], axis=0)
    sin_t = jnp.concatenate([sin_t, sin_t], axis=0)
    zeros_t = jnp.zeros((HEAD_W - NOPE - ROPE, tm), jnp.float32)
    for hh in range(N_HEADS):
        r0 = hh * HEAD_W
        roped = qt[r0 + NOPE:r0 + NOPE + ROPE, :] * cos_t + qt[r0 + NOPE + ROPE:r0 + HEAD_W, :] * sin_t
        qh = jnp.concatenate([qt[r0:r0 + NOPE, :], roped, zeros_t], axis=0)
        qt_ref[0, hh] = (qh * Q_PRESCALE).astype(jnp.bfloat16)

    gm_ref[0] = _silu(gm).astype(jnp.bfloat16)


def _const_spec(shape):
    nd = len(shape)
    return pl.BlockSpec(shape, lambda b, i: (0,) * nd, pipeline_mode=pl.Buffered(1))


def _proj_call(first, x, posr, emb_g, emb_b, consts):
    B, S, D = x.shape
    tm = PROJ_TM
    grid = (B, S // tm)
    tok = lambda w: pl.BlockSpec((1, tm, w), lambda b, i: (b, i, 0))
    in_specs = [tok(D), pl.BlockSpec((1, 1, tm), lambda b, i: (b, 0, i))]
    args = [x, posr]
    if first:
        in_specs += [_const_spec((1, D)), _const_spec((1, D))]
        args += [emb_g, emb_b]
    in_specs += [_const_spec(c.shape) for c in consts]
    args += list(consts)
    bf = jnp.bfloat16
    out_shape = [
        jax.ShapeDtypeStruct((B, N_HEADS, HEAD_W, S), bf),
        jax.ShapeDtypeStruct((B, N_HEADS, S, HEAD_W), bf),
        jax.ShapeDtypeStruct((B, N_HEADS, S // KT, V_DIM, KT), bf),
        jax.ShapeDtypeStruct((B, S, D_MLA), bf),
        jax.ShapeDtypeStruct((B, S, D_POOL + D_CONV), bf),
    ]
    out_specs = [
        pl.BlockSpec((1, N_HEADS, HEAD_W, tm), lambda b, i: (b, 0, 0, i)),
        pl.BlockSpec((1, N_HEADS, tm, HEAD_W), lambda b, i: (b, 0, i, 0)),
        pl.BlockSpec((1, N_HEADS, 1, V_DIM, KT), lambda b, i: (b, 0, i, 0, 0)),
        tok(D_MLA), tok(D_POOL + D_CONV)]
    if first:
        out_shape.append(jax.ShapeDtypeStruct((B, S, D), jnp.float32))
        out_specs.append(tok(D))
    return pl.pallas_call(
        functools.partial(_proj_kernel, first),
        grid=grid,
        in_specs=in_specs,
        out_specs=out_specs,
        out_shape=out_shape,
        scratch_shapes=[
            pltpu.VMEM((POOL_HALO + tm, D_POOL), jnp.float32),
            pltpu.VMEM((CONV_HALO + tm, D_CONV), jnp.float32),
        ],
        compiler_params=pltpu.CompilerParams(
            dimension_semantics=("arbitrary", "arbitrary"),
            vmem_limit_bytes=VMEM_LIMIT_BYTES),
        name="proj_first" if first else "proj",
    )(*args)


def _rows8_reduce(x, op):
    acc = x[0:SUBLANES]
    for r in range(SUBLANES, x.shape[0], SUBLANES):
        acc = op(acc, x[r:r + SUBLANES])
    return acc


def _attn_kernel(qt_ref, k_ref, vt_ref, gm_ref, o_ref,
                 s0_sc, s1_sc, mt0_sc, mt1_sc, p_sc, m_sc, l_sc, acc_sc):
    tq, tk, G, ch = ATT_TQ, ATT_TK, ATT_G, ATT_CH
    sub = tk // KT
    i = pl.program_id(2)

    def scores_into(s_ref, mt_ref, n):
        start = pl.multiple_of(n * tk, tk)
        for g in range(G):
            s = _dot(k_ref[0, g, pl.ds(start, tk), :], qt_ref[0, g])
            s_ref[g] = s
            mt_ref[g] = _rows8_reduce(s, jnp.maximum)

    def update(g, n, s_ref, m_tile8, mask):
        m_prev = m_sc[g]
        m_new = jnp.maximum(m_prev, jnp.max(m_tile8, axis=0, keepdims=True))
        alpha = jnp.exp2(m_prev - m_new)
        lsum = jnp.zeros((SUBLANES, tq), jnp.float32)
        for c in range(tk // ch):
            s_c = s_ref[g, c * ch:(c + 1) * ch, :]
            if mask is not None:
                s_c = jnp.where(mask(c), s_c, -jnp.inf)
            p_c = jnp.exp2(s_c - m_new)
            lsum = lsum + _rows8_reduce(p_c, jnp.add)
            p_sc[g, c * ch:(c + 1) * ch, :] = p_c.astype(jnp.bfloat16)
        l_sc[g] = alpha * l_sc[g] + jnp.sum(lsum, axis=0, keepdims=True)
        vt = jnp.concatenate([vt_ref[0, g, n * sub + c] for c in range(sub)], axis=1)
        acc_sc[g] = alpha * acc_sc[g] + _dot(vt, p_sc[g])
        m_sc[g] = m_new

    m_sc[...] = jnp.full(m_sc.shape, M_INIT, jnp.float32)
    l_sc[...] = jnp.zeros(l_sc.shape, jnp.float32)
    acc_sc[...] = jnp.zeros(acc_sc.shape, jnp.float32)
    scores_into(s0_sc, mt0_sc, 0)

    def full_tile(cur, nxt, n):
        scores_into(nxt[0], nxt[1], n + 1)
        for g in range(G):
            update(g, n, cur[0], cur[1][g], None)

    even, odd = (s0_sc, mt0_sc), (s1_sc, mt1_sc)

    def body(pair, carry):
        n = 2 * pair
        full_tile(even, odd, n)

        @pl.when(n + 1 < i)
        def _():
            full_tile(odd, even, n + 1)
        return carry

    jax.lax.fori_loop(0, (i + 1) // 2, body, 0)

    def diag_tile(cur):
        def mask(c):
            key = c * ch + jax.lax.broadcasted_iota(jnp.int32, (ch, tq), 0)
            return key <= jax.lax.broadcasted_iota(jnp.int32, (ch, tq), 1)
        for g in range(G):
            m8 = jnp.full((SUBLANES, tq), -jnp.inf, jnp.float32)
            for c in range(tk // ch):
                s_c = jnp.where(mask(c), cur[0][g, c * ch:(c + 1) * ch, :], -jnp.inf)
                m8 = jnp.maximum(m8, _rows8_reduce(s_c, jnp.maximum))
            update(g, i, cur[0], m8, mask)

    @pl.when(i % 2 == 0)
    def _():
        diag_tile(even)

    @pl.when(i % 2 == 1)
    def _():
        diag_tile(odd)

    for g in range(G):
        out_t = acc_sc[g] / l_sc[g]
        out = out_t.T * gm_ref[0, :, g * V_DIM:(g + 1) * V_DIM].astype(jnp.float32)
        o_ref[0, :, g * V_DIM:(g + 1) * V_DIM] = out.astype(o_ref.dtype)


def _attn_call(qt, k, vt, gm):
    B, H, S, _ = k.shape
    tq, G = ATT_TQ, ATT_G
    assert ATT_TQ == ATT_TK and ATT_TK % KT == 0
    grid = (B, H // G, S // tq)
    return pl.pallas_call(
        _attn_kernel,
        grid=grid,
        in_specs=[
            pl.BlockSpec((1, G, HEAD_W, tq), lambda b, h, i: (b, h, 0, i)),
            pl.BlockSpec((1, G, S, HEAD_W), lambda b, h, i: (b, h, 0, 0)),
            pl.BlockSpec((1, G, S // KT, V_DIM, KT), lambda b, h, i: (b, h, 0, 0, 0)),
            pl.BlockSpec((1, tq, G * V_DIM), lambda b, h, i: (b, i, h)),
        ],
        out_specs=pl.BlockSpec((1, tq, G * V_DIM), lambda b, h, i: (b, i, h)),
        out_shape=jax.ShapeDtypeStruct((B, S, D_MLA), jnp.bfloat16),
        scratch_shapes=[
            pltpu.VMEM((G, ATT_TK, tq), jnp.float32),
            pltpu.VMEM((G, ATT_TK, tq), jnp.float32),
            pltpu.VMEM((G, SUBLANES, tq), jnp.float32),
            pltpu.VMEM((G, SUBLANES, tq), jnp.float32),
            pltpu.VMEM((G, ATT_TK, tq), jnp.bfloat16),
            pltpu.VMEM((G, 1, tq), jnp.float32),
            pltpu.VMEM((G, 1, tq), jnp.float32),
            pltpu.VMEM((G, V_DIM, tq), jnp.float32),
        ],
        compiler_params=pltpu.CompilerParams(
            dimension_semantics=("arbitrary", "arbitrary", "arbitrary"),
            vmem_limit_bytes=VMEM_LIMIT_BYTES),
        name="attn",
    )(qt, k, vt, gm)


def _out_kernel(o_ref, pc_ref, h_ref, w_ref, b_ref, g_ref, beta_ref, y_ref):
    for r in range(0, OUT_TM, OUT_CH):
        rows = slice(r, r + OUT_CH)
        mix = jnp.concatenate([o_ref[0, rows, :], pc_ref[0, rows, :]], axis=1)
        y = _dot(mix, w_ref[...]) + b_ref[...] + DEEPNORM_ALPHA * h_ref[0, rows, :]
        y_ref[0, rows, :] = _layernorm(y, g_ref[...], beta_ref[...])


def _out_call(o, pc, h, w_out, b_out, ln_g, ln_b):
    B, S, D = h.shape
    tm = OUT_TM
    tok = lambda w: pl.BlockSpec((1, tm, w), lambda b, i: (b, i, 0))
    return pl.pallas_call(
        _out_kernel,
        grid=(B, S // tm),
        in_specs=[tok(D_MLA), tok(D_POOL + D_CONV), tok(D),
                  _const_spec(w_out.shape), _const_spec((1, D)), _const_spec((1, D)),
                  _const_spec((1, D))],
        out_specs=tok(D),
        out_shape=jax.ShapeDtypeStruct((B, S, D), jnp.float32),
        compiler_params=pltpu.CompilerParams(
            dimension_semantics=("arbitrary", "arbitrary"),
            vmem_limit_bytes=VMEM_LIMIT_BYTES),
        name="out",
    )(o, pc, h, w_out, b_out, ln_g, ln_b)


def _rotate_half_cols(w):
    return jnp.concatenate([-w[..., HALF:], w[..., :HALF]], axis=-1)


def _prep_w_in(w):
    n_lat = Q_LORA + KV_LORA + ROPE
    kr = w[..., n_lat - ROPE:n_lat]
    w_lat = jnp.concatenate([w[..., :n_lat], _rotate_half_cols(kr)], axis=-1).astype(jnp.bfloat16)
    return w_lat, w[..., n_lat:].astype(jnp.bfloat16)


def _prep_w_uq_t(w):
    L = w.shape[0]
    w = w.reshape(L, Q_LORA, N_HEADS, NOPE + ROPE)
    w = jnp.concatenate([w, _rotate_half_cols(w[..., NOPE:])], axis=-1)
    return jnp.swapaxes(w.reshape(L, Q_LORA, N_HEADS * HEAD_W), 1, 2).astype(jnp.bfloat16)


def _prep_w_ukv(w):
    L = w.shape[0]
    w = w.reshape(L, KV_LORA, N_HEADS, NOPE + V_DIM)
    wk = w[..., :NOPE].reshape(L, KV_LORA, N_HEADS * NOPE)
    wv = w[..., NOPE:].reshape(L, KV_LORA, N_HEADS * V_DIM)
    return wk.astype(jnp.bfloat16), jnp.swapaxes(wv, 1, 2).astype(jnp.bfloat16)


def kernel(x, positions, emb_ln_g, emb_ln_b, w_in, q_norm_g, kv_norm_g, w_uq, w_ukv, w_pool,
           pool_scale, conv_w, w_out, b_out, ln_g, ln_b):
    B, S, D = x.shape
    bf = jnp.bfloat16
    posr = positions.reshape(B, 1, S)
    inv_freq = ROPE_THETA ** (-jnp.arange(HALF, dtype=jnp.float32) / HALF)
    invfc = inv_freq.reshape(HALF, 1)
    row = lambda a: a.reshape(1, -1)

    w_lat, w_mix = _prep_w_in(w_in)
    w_uqt = _prep_w_uq_t(w_uq)
    w_uk, w_uvt = _prep_w_ukv(w_ukv)
    w_pool_b = w_pool.astype(bf)
    w_out_b = w_out.astype(bf)

    h = x
    for l in range(DEPTH):
        first = l == 0
        consts = (w_lat[l], w_mix[l], row(q_norm_g[l]), row(kv_norm_g[l]), w_uqt[l],
                  w_uk[l], w_uvt[l], w_pool_b[l], row(pool_scale[l]), conv_w[l], invfc)
        outs = _proj_call(first, h, posr, row(emb_ln_g), row(emb_ln_b), consts)
        if first:
            qt, k, vt, gm, pc, h = outs
        else:
            qt, k, vt, gm, pc = outs
        o = _attn_call(qt, k, vt, gm)
        h = _out_call(o, pc, h, w_out_b[l], row(b_out[l]), row(ln_g[l]), row(ln_b[l]))
    return h
```

```python
import functools
import math

import jax
import jax.numpy as jnp
import numpy as np
from jax.experimental import pallas as pl
from jax.experimental.pallas import tpu as pltpu

D_MODEL = 2048
DEPTH = 2
N_HEADS = 8
NOPE = 128
ROPE = 64
V_DIM = 128
Q_LORA = 512
KV_LORA = 256
D_MLA = N_HEADS * V_DIM
ROPE_THETA = 10000.0
POOL_WINDOWS = (2, 4, 8, 16)
POOL_GROUP = 128
D_POOL = 512
D_CONV = 512
CONV_WIDTH = 3
LN_EPS = 1e-5
RMS_EPS = 1e-6
DEEPNORM_ALPHA = (2 * DEPTH) ** 0.25

LANES = 128
SUBLANES = 8
VMEM_LIMIT_BYTES = 56 * 1024 * 1024

PROJ_TM = 256
ATT_TQ = 512
ATT_TK = 512
ATT_G = 2
ATT_CH = 64
WPREP_ROWS = 256
OUT_TM = 512
OUT_CH = 128
POOL_HALO = 16
CONV_HALO = 8

KT = PROJ_TM
HEAD_W = 2 * LANES
HALF = ROPE // 2

C_QLAT = 0
C_KVLAT = C_QLAT + Q_LORA
C_KR = C_KVLAT + KV_LORA
C_LAT_END = C_KR + LANES
R_GMLA = 0
R_PIN = R_GMLA + D_MLA
R_GPOOL = R_PIN + D_POOL
R_CH = R_GPOOL + D_POOL
R_END = R_CH + 4 * D_CONV

Q_PRESCALE = (NOPE + ROPE) ** -0.5 * math.log2(math.e)
M_INIT = -1e30


def _silu(g):
    return g * (1.0 / (1.0 + jnp.exp(-g)))


def _layernorm(x, g, b):
    mu = jnp.mean(x, axis=-1, keepdims=True)
    xc = x - mu
    var = jnp.mean(xc * xc, axis=-1, keepdims=True)
    return xc * jax.lax.rsqrt(var + LN_EPS) * g + b


def _rmsnorm(x, g):
    return x * jax.lax.rsqrt(jnp.mean(x * x, axis=-1, keepdims=True) + RMS_EPS) * g


def _dot(a, b):
    return jnp.dot(a, b, preferred_element_type=jnp.float32)


def _dot_nt(a, b):
    return jax.lax.dot_general(a, b, (((1,), (1,)), ((), ())), preferred_element_type=jnp.float32)


def _proj_kernel(first, *refs):
    if first:
        (x_ref, posr_ref, eg_ref, eb_ref, w_lat_ref, w_mix_ref, qg_ref, kvg_ref, w_uqt_ref, w_uk_ref,
         w_uvt_ref, w_pool_ref, pscale_ref, convw_ref, invfc_ref,
         qt_ref, k_ref, vt_ref, gm_ref, pc_ref, h_ref, pool_ext, conv_ext) = refs
    else:
        (x_ref, posr_ref, w_lat_ref, w_mix_ref, qg_ref, kvg_ref, w_uqt_ref, w_uk_ref,
         w_uvt_ref, w_pool_ref, pscale_ref, convw_ref, invfc_ref,
         qt_ref, k_ref, vt_ref, gm_ref, pc_ref, pool_ext, conv_ext) = refs
    tm = PROJ_TM
    i = pl.program_id(1)

    @pl.when(i == 0)
    def _():
        pool_ext[0:POOL_HALO, :] = jnp.zeros((POOL_HALO, D_POOL), jnp.float32)
        conv_ext[0:CONV_HALO, :] = jnp.zeros((CONV_HALO, D_CONV), jnp.float32)

    @pl.when(i > 0)
    def _():
        pool_ext[0:POOL_HALO, :] = pool_ext[tm:tm + POOL_HALO, :]
        conv_ext[0:CONV_HALO, :] = conv_ext[tm:tm + CONV_HALO, :]

    x = x_ref[0]
    if first:
        x = _layernorm(x, eg_ref[...], eb_ref[...])
        h_ref[0] = x
    hb = x.astype(jnp.bfloat16)

    lat = _dot(hb, w_lat_ref[...])
    pin = _dot(hb, w_mix_ref[:, R_PIN:R_GPOOL])
    cv = _dot(hb, w_mix_ref[:, R_CH:R_END])

    qn = _rmsnorm(lat[:, C_QLAT:C_KVLAT], qg_ref[...]).astype(jnp.bfloat16)
    kvn = _rmsnorm(lat[:, C_KVLAT:C_KR], kvg_ref[...]).astype(jnp.bfloat16)
    kr = lat[:, C_KR:C_LAT_END]
    angt = invfc_ref[...] * posr_ref[0].astype(jnp.float32)
    cos_t, sin_t = jnp.cos(angt), jnp.sin(angt)
    cs = jnp.concatenate([cos_t, cos_t, sin_t, sin_t], axis=0).T
    lane = jax.lax.broadcasted_iota(jnp.int32, (tm, LANES), 1)
    t = kr * cs
    kroped = jnp.where(lane < ROPE, t + pltpu.roll(t, ROPE, axis=1), 0.0).astype(jnp.bfloat16)

    kn = _dot(kvn, w_uk_ref[...])
    vt = _dot_nt(w_uvt_ref[...], kvn)
    qt = _dot_nt(w_uqt_ref[...], qn)
    gpool = _dot(hb, w_mix_ref[:, R_GPOOL:R_CH])

    for hh in range(N_HEADS):
        k_ref[0, hh] = jnp.concatenate(
            [kn[:, hh * NOPE:(hh + 1) * NOPE].astype(jnp.bfloat16), kroped], axis=1)
        vt_ref[0, hh, 0] = vt[hh * V_DIM:(hh + 1) * V_DIM, :].astype(jnp.bfloat16)

    u = cv[:, 2 * D_CONV:3 * D_CONV] * cv[:, 0:D_CONV]
    conv_ext[CONV_HALO:CONV_HALO + tm, :] = u
    yc = convw_ref[2:3, :] * u
    yc = yc + convw_ref[1:2, :] * conv_ext[CONV_HALO - 1:CONV_HALO - 1 + tm, :]
    yc = yc + convw_ref[0:1, :] * conv_ext[CONV_HALO - 2:CONV_HALO - 2 + tm, :]
    y_conv = cv[:, D_CONV:2 * D_CONV] * yc * _silu(cv[:, 3 * D_CONV:4 * D_CONV])
    pc_ref[0, :, D_POOL:] = y_conv.astype(jnp.bfloat16)

    pool_ext[POOL_HALO:POOL_HALO + tm, :] = pin
    t1 = (i * tm + 1 + jax.lax.broadcasted_iota(jnp.int32, (tm, 1), 0)).astype(jnp.float32)
    ys = []
    for g, w in enumerate(POOL_WINDOWS):
        lo = g * POOL_GROUP
        acc = pin[:, lo:lo + POOL_GROUP]
        for s in range(1, w):
            acc = acc + pool_ext[POOL_HALO - s:POOL_HALO - s + tm, lo:lo + POOL_GROUP]
        pooled = acc / jnp.minimum(t1, float(w)) - pin[:, lo:lo + POOL_GROUP]
        ys.append(_dot(pooled.astype(jnp.bfloat16), w_pool_ref[g]))
    gm = _dot(hb, w_mix_ref[:, R_GMLA:R_PIN])
    y_pool = jnp.concatenate(ys, axis=1) * pscale_ref[...] * _silu(gpool)
    pc_ref[0, :, :D_POOL] = y_pool.astype(jnp.bfloat16)

    cos_t = jnp.concatenate([cos_t, cos_t], axis=0)
    sin_t = jnp.concatenate([sin_t, sin_t], axis=0)
    zeros_t = jnp.zeros((HEAD_W - NOPE - ROPE, tm), jnp.float32)
    for hh in range(N_HEADS):
        r0 = hh * HEAD_W
        roped = qt[r0 + NOPE:r0 + NOPE + ROPE, :] * cos_t + qt[r0 + NOPE + ROPE:r0 + HEAD_W, :] * sin_t
        qh = jnp.concatenate([qt[r0:r0 + NOPE, :], roped, zeros_t], axis=0)
        qt_ref[0, hh] = (qh * Q_PRESCALE).astype(jnp.bfloat16)

    gm_ref[0] = _silu(gm).astype(jnp.bfloat16)


def _const_spec(shape):
    nd = len(shape)
    return pl.BlockSpec(shape, lambda b, i: (0,) * nd, pipeline_mode=pl.Buffered(1))


def _proj_call(first, x, posr, emb_g, emb_b, consts):
    B, S, D = x.shape
    tm = PROJ_TM
    grid = (B, S // tm)
    tok = lambda w: pl.BlockSpec((1, tm, w), lambda b, i: (b, i, 0))
    in_specs = [tok(D), pl.BlockSpec((1, 1, tm), lambda b, i: (b, 0, i))]
    args = [x, posr]
    if first:
        in_specs += [_const_spec((1, D)), _const_spec((1, D))]
        args += [emb_g, emb_b]
    in_specs += [_const_spec(c.shape) for c in consts]
    args += list(consts)
    bf = jnp.bfloat16
    out_shape = [
        jax.ShapeDtypeStruct((B, N_HEADS, HEAD_W, S), bf),
        jax.ShapeDtypeStruct((B, N_HEADS, S, HEAD_W), bf),
        jax.ShapeDtypeStruct((B, N_HEADS, S // KT, V_DIM, KT), bf),
        jax.ShapeDtypeStruct((B, S, D_MLA), bf),
        jax.ShapeDtypeStruct((B, S, D_POOL + D_CONV), bf),
    ]
    out_specs = [
        pl.BlockSpec((1, N_HEADS, HEAD_W, tm), lambda b, i: (b, 0, 0, i)),
        pl.BlockSpec((1, N_HEADS, tm, HEAD_W), lambda b, i: (b, 0, i, 0)),
        pl.BlockSpec((1, N_HEADS, 1, V_DIM, KT), lambda b, i: (b, 0, i, 0, 0)),
        tok(D_MLA), tok(D_POOL + D_CONV)]
    if first:
        out_shape.append(jax.ShapeDtypeStruct((B, S, D), jnp.float32))
        out_specs.append(tok(D))
    return pl.pallas_call(
        functools.partial(_proj_kernel, first),
        grid=grid,
        in_specs=in_specs,
        out_specs=out_specs,
        out_shape=out_shape,
        scratch_shapes=[
            pltpu.VMEM((POOL_HALO + tm, D_POOL), jnp.float32),
            pltpu.VMEM((CONV_HALO + tm, D_CONV), jnp.float32),
        ],
        compiler_params=pltpu.CompilerParams(
            dimension_semantics=("arbitrary", "arbitrary"),
            vmem_limit_bytes=VMEM_LIMIT_BYTES),
        name="proj_first" if first else "proj",
    )(*args)


def _rows8_reduce(x, op):
    acc = x[0:SUBLANES]
    for r in range(SUBLANES, x.shape[0], SUBLANES):
        acc = op(acc, x[r:r + SUBLANES])
    return acc


def _attn_kernel(qt_ref, k_ref, vt_ref, gm_ref, o_ref,
                 s0_sc, s1_sc, mt0_sc, mt1_sc, p_sc, m_sc, l_sc, acc_sc):
    tq, tk, G, ch = ATT_TQ, ATT_TK, ATT_G, ATT_CH
    sub = tk // KT
    i = pl.program_id(2)

    def scores_into(s_ref, mt_ref, n):
        start = pl.multiple_of(n * tk, tk)
        for g in range(G):
            s = _dot(k_ref[0, g, pl.ds(start, tk), :], qt_ref[0, g])
            s_ref[g] = s
            mt_ref[g] = _rows8_reduce(s, jnp.maximum)

    def update(g, n, s_ref, m_tile8, mask):
        m_prev = m_sc[g]
        m_new = jnp.maximum(m_prev, jnp.max(m_tile8, axis=0, keepdims=True))
        alpha = jnp.exp2(m_prev - m_new)
        lsum = jnp.zeros((SUBLANES, tq), jnp.float32)
        for c in range(tk // ch):
            s_c = s_ref[g, c * ch:(c + 1) * ch, :]
            if mask is not None:
                s_c = jnp.where(mask(c), s_c, -jnp.inf)
            p_c = jnp.exp2(s_c - m_new)
            lsum = lsum + _rows8_reduce(p_c, jnp.add)
            p_sc[g, c * ch:(c + 1) * ch, :] = p_c.astype(jnp.bfloat16)
        l_sc[g] = alpha * l_sc[g] + jnp.sum(lsum, axis=0, keepdims=True)
        vt = jnp.concatenate([vt_ref[0, g, n * sub + c] for c in range(sub)], axis=1)
        acc_sc[g] = alpha * acc_sc[g] + _dot(vt, p_sc[g])
        m_sc[g] = m_new

    m_sc[...] = jnp.full(m_sc.shape, M_INIT, jnp.float32)
    l_sc[...] = jnp.zeros(l_sc.shape, jnp.float32)
    acc_sc[...] = jnp.zeros(acc_sc.shape, jnp.float32)
    scores_into(s0_sc, mt0_sc, 0)

    def full_tile(cur, nxt, n):
        scores_into(nxt[0], nxt[1], n + 1)
        for g in range(G):
            update(g, n, cur[0], cur[1][g], None)

    even, odd = (s0_sc, mt0_sc), (s1_sc, mt1_sc)

    def body(pair, carry):
        n = 2 * pair
        full_tile(even, odd, n)

        @pl.when(n + 1 < i)
        def _():
            full_tile(odd, even, n + 1)
        return carry

    jax.lax.fori_loop(0, (i + 1) // 2, body, 0)

    def diag_tile(cur):
        def mask(c):
            key = c * ch + jax.lax.broadcasted_iota(jnp.int32, (ch, tq), 0)
            return key <= jax.lax.broadcasted_iota(jnp.int32, (ch, tq), 1)
        for g in range(G):
            m8 = jnp.full((SUBLANES, tq), -jnp.inf, jnp.float32)
            for c in range(tk // ch):
                s_c = jnp.where(mask(c), cur[0][g, c * ch:(c + 1) * ch, :], -jnp.inf)
                m8 = jnp.maximum(m8, _rows8_reduce(s_c, jnp.maximum))
            update(g, i, cur[0], m8, mask)

    @pl.when(i % 2 == 0)
    def _():
        diag_tile(even)

    @pl.when(i % 2 == 1)
    def _():
        diag_tile(odd)

    for g in range(G):
        out_t = acc_sc[g] / l_sc[g]
        out = out_t.T * gm_ref[0, :, g * V_DIM:(g + 1) * V_DIM].astype(jnp.float32)
        o_ref[0, :, g * V_DIM:(g + 1) * V_DIM] = out.astype(o_ref.dtype)


def _attn_call(qt, k, vt, gm):
    B, H, S, _ = k.shape
    tq, G = ATT_TQ, ATT_G
    assert ATT_TQ == ATT_TK and ATT_TK % KT == 0
    grid = (B, H // G, S // tq)
    return pl.pallas_call(
        _attn_kernel,
        grid=grid,
        in_specs=[
            pl.BlockSpec((1, G, HEAD_W, tq), lambda b, h, i: (b, h, 0, i)),
            pl.BlockSpec((1, G, S, HEAD_W), lambda b, h, i: (b, h, 0, 0)),
            pl.BlockSpec((1, G, S // KT, V_DIM, KT), lambda b, h, i: (b, h, 0, 0, 0)),
            pl.BlockSpec((1, tq, G * V_DIM), lambda b, h, i: (b, i, h)),
        ],
        out_specs=pl.BlockSpec((1, tq, G * V_DIM), lambda b, h, i: (b, i, h)),
        out_shape=jax.ShapeDtypeStruct((B, S, D_MLA), jnp.bfloat16),
        scratch_shapes=[
            pltpu.VMEM((G, ATT_TK, tq), jnp.float32),
            pltpu.VMEM((G, ATT_TK, tq), jnp.float32),
            pltpu.VMEM((G, SUBLANES, tq), jnp.float32),
            pltpu.VMEM((G, SUBLANES, tq), jnp.float32),
            pltpu.VMEM((G, ATT_TK, tq), jnp.bfloat16),
            pltpu.VMEM((G, 1, tq), jnp.float32),
            pltpu.VMEM((G, 1, tq), jnp.float32),
            pltpu.VMEM((G, V_DIM, tq), jnp.float32),
        ],
        compiler_params=pltpu.CompilerParams(
            dimension_semantics=("arbitrary", "arbitrary", "arbitrary"),
            vmem_limit_bytes=VMEM_LIMIT_BYTES),
        name="attn",
    )(qt, k, vt, gm)


def _out_kernel(o_ref, pc_ref, h_ref, w_ref, b_ref, g_ref, beta_ref, y_ref):
    for r in range(0, OUT_TM, OUT_CH):
        rows = slice(r, r + OUT_CH)
        mix = jnp.concatenate([o_ref[0, rows, :], pc_ref[0, rows, :]], axis=1)
        y = _dot(mix, w_ref[...]) + b_ref[...] + DEEPNORM_ALPHA * h_ref[0, rows, :]
        y_ref[0, rows, :] = _layernorm(y, g_ref[...], beta_ref[...])


def _out_call(o, pc, h, w_out, b_out, ln_g, ln_b):
    B, S, D = h.shape
    tm = OUT_TM
    tok = lambda w: pl.BlockSpec((1, tm, w), lambda b, i: (b, i, 0))
    return pl.pallas_call(
        _out_kernel,
        grid=(B, S // tm),
        in_specs=[tok(D_MLA), tok(D_POOL + D_CONV), tok(D),
                  _const_spec(w_out.shape), _const_spec((1, D)), _const_spec((1, D)),
                  _const_spec((1, D))],
        out_specs=tok(D),
        out_shape=jax.ShapeDtypeStruct((B, S, D), jnp.float32),
        compiler_params=pltpu.CompilerParams(
            dimension_semantics=("arbitrary", "arbitrary"),
            vmem_limit_bytes=VMEM_LIMIT_BYTES),
        name="out",
    )(o, pc, h, w_out, b_out, ln_g, ln_b)


def _rotate_half_cols(w):
    return jnp.concatenate([-w[..., HALF:], w[..., :HALF]], axis=-1)


def _wprep_kernel(w_ref, lat_ref, mix_ref):
    x = w_ref[0]
    mix_ref[0] = x[:, C_KR + ROPE:].astype(jnp.bfloat16)
    rot = _rotate_half_cols(x[:, C_KR:C_KR + ROPE])
    lat_ref[0] = jnp.concatenate([x[:, :C_KR + ROPE], rot], axis=1).astype(jnp.bfloat16)


def _prep_w_in(w):
    L, D, C = w.shape
    blk = lambda width: pl.BlockSpec((1, WPREP_ROWS, width), lambda l, r: (l, r, 0))
    return pl.pallas_call(
        _wprep_kernel,
        grid=(L, D // WPREP_ROWS),
        in_specs=[blk(C)],
        out_specs=[blk(C_LAT_END), blk(R_END)],
        out_shape=[jax.ShapeDtypeStruct((L, D, C_LAT_END), jnp.bfloat16),
                   jax.ShapeDtypeStruct((L, D, R_END), jnp.bfloat16)],
        compiler_params=pltpu.CompilerParams(
            dimension_semantics=("arbitrary", "arbitrary"),
            vmem_limit_bytes=VMEM_LIMIT_BYTES),
        name="wprep",
    )(w)


def _prep_w_uq_t(w):
    L = w.shape[0]
    w = w.reshape(L, Q_LORA, N_HEADS, NOPE + ROPE)
    w = jnp.concatenate([w, _rotate_half_cols(w[..., NOPE:])], axis=-1)
    return jnp.swapaxes(w.reshape(L, Q_LORA, N_HEADS * HEAD_W), 1, 2).astype(jnp.bfloat16)


def _prep_w_ukv(w):
    L = w.shape[0]
    w = w.reshape(L, KV_LORA, N_HEADS, NOPE + V_DIM)
    wk = w[..., :NOPE].reshape(L, KV_LORA, N_HEADS * NOPE)
    wv = w[..., NOPE:].reshape(L, KV_LORA, N_HEADS * V_DIM)
    return wk.astype(jnp.bfloat16), jnp.swapaxes(wv, 1, 2).astype(jnp.bfloat16)


def kernel(x, positions, emb_ln_g, emb_ln_b, w_in, q_norm_g, kv_norm_g, w_uq, w_ukv, w_pool,
           pool_scale, conv_w, w_out, b_out, ln_g, ln_b):
    B, S, D = x.shape
    bf = jnp.bfloat16
    posr = positions.reshape(B, 1, S)
    inv_freq = ROPE_THETA ** (-jnp.arange(HALF, dtype=jnp.float32) / HALF)
    invfc = inv_freq.reshape(HALF, 1)
    row = lambda a: a.reshape(1, -1)

    w_lat, w_mix = _prep_w_in(w_in)
    w_uqt = _prep_w_uq_t(w_uq)
    w_uk, w_uvt = _prep_w_ukv(w_ukv)
    w_pool_b = w_pool.astype(bf)
    w_out_b = w_out.astype(bf)

    h = x
    for l in range(DEPTH):
        first = l == 0
        consts = (w_lat[l], w_mix[l], row(q_norm_g[l]), row(kv_norm_g[l]), w_uqt[l],
                  w_uk[l], w_uvt[l], w_pool_b[l], row(pool_scale[l]), conv_w[l], invfc)
        outs = _proj_call(first, h, posr, row(emb_ln_g), row(emb_ln_b), consts)
        if first:
            qt, k, vt, gm, pc, h = outs
        else:
            qt, k, vt, gm, pc = outs
        o = _attn_call(qt, k, vt, gm)
        h = _out_call(o, pc, h, w_out_b[l], row(b_out[l]), row(ln_g[l]), row(ln_b[l]))
    return h
```

```python
import functools
import math

import jax
import jax.numpy as jnp
import numpy as np
from jax.experimental import pallas as pl
from jax.experimental.pallas import tpu as pltpu

D_MODEL = 2048
DEPTH = 2
N_HEADS = 8
NOPE = 128
ROPE = 64
V_DIM = 128
Q_LORA = 512
KV_LORA = 256
D_MLA = N_HEADS * V_DIM
ROPE_THETA = 10000.0
POOL_WINDOWS = (2, 4, 8, 16)
POOL_GROUP = 128
D_POOL = 512
D_CONV = 512
CONV_WIDTH = 3
LN_EPS = 1e-5
RMS_EPS = 1e-6
DEEPNORM_ALPHA = (2 * DEPTH) ** 0.25

LANES = 128
SUBLANES = 8
VMEM_LIMIT_BYTES = 56 * 1024 * 1024

PROJ_TM = 256
ATT_TQ = 512
ATT_TK = 512
ATT_G = 2
ATT_CH = 64
WPREP_COLS = 512
OUT_TM = 512
OUT_CH = 128
POOL_HALO = 16
CONV_HALO = 8

KT = PROJ_TM
HEAD_W = 2 * LANES
HALF = ROPE // 2

C_QLAT = 0
C_KVLAT = C_QLAT + Q_LORA
C_KR = C_KVLAT + KV_LORA
C_LAT_END = C_KR + LANES
R_GMLA = 0
R_PIN = R_GMLA + D_MLA
R_GPOOL = R_PIN + D_POOL
R_CH = R_GPOOL + D_POOL
R_END = R_CH + 4 * D_CONV

Q_PRESCALE = (NOPE + ROPE) ** -0.5 * math.log2(math.e)
M_INIT = -1e30


def _silu(g):
    return g * (1.0 / (1.0 + jnp.exp(-g)))


def _layernorm(x, g, b):
    mu = jnp.mean(x, axis=-1, keepdims=True)
    xc = x - mu
    var = jnp.mean(xc * xc, axis=-1, keepdims=True)
    return xc * jax.lax.rsqrt(var + LN_EPS) * g + b


def _rmsnorm(x, g):
    return x * jax.lax.rsqrt(jnp.mean(x * x, axis=-1, keepdims=True) + RMS_EPS) * g


def _dot(a, b):
    return jnp.dot(a, b, preferred_element_type=jnp.float32)


def _dot_nt(a, b):
    return jax.lax.dot_general(a, b, (((1,), (1,)), ((), ())), preferred_element_type=jnp.float32)


def _proj_kernel(first, *refs):
    if first:
        (x_ref, posr_ref, eg_ref, eb_ref, w_lat_ref, w_mix_ref, qg_ref, kvg_ref, w_uqt_ref, w_uk_ref,
         w_uvt_ref, w_pool_ref, pscale_ref, convw_ref, invfc_ref,
         qt_ref, k_ref, vt_ref, gm_ref, pc_ref, h_ref, pool_ext, conv_ext) = refs
    else:
        (x_ref, posr_ref, w_lat_ref, w_mix_ref, qg_ref, kvg_ref, w_uqt_ref, w_uk_ref,
         w_uvt_ref, w_pool_ref, pscale_ref, convw_ref, invfc_ref,
         qt_ref, k_ref, vt_ref, gm_ref, pc_ref, pool_ext, conv_ext) = refs
    tm = PROJ_TM
    i = pl.program_id(1)

    @pl.when(i == 0)
    def _():
        pool_ext[0:POOL_HALO, :] = jnp.zeros((POOL_HALO, D_POOL), jnp.float32)
        conv_ext[0:CONV_HALO, :] = jnp.zeros((CONV_HALO, D_CONV), jnp.float32)

    @pl.when(i > 0)
    def _():
        pool_ext[0:POOL_HALO, :] = pool_ext[tm:tm + POOL_HALO, :]
        conv_ext[0:CONV_HALO, :] = conv_ext[tm:tm + CONV_HALO, :]

    x = x_ref[0]
    if first:
        x = _layernorm(x, eg_ref[...], eb_ref[...])
        h_ref[0] = x
    hb = x.astype(jnp.bfloat16)

    lat = _dot(hb, w_lat_ref[0])
    pin = _dot(hb, w_mix_ref[0, :, R_PIN:R_GPOOL])
    cv = _dot(hb, w_mix_ref[0, :, R_CH:R_END])

    qn = _rmsnorm(lat[:, C_QLAT:C_KVLAT], qg_ref[0]).astype(jnp.bfloat16)
    kvn = _rmsnorm(lat[:, C_KVLAT:C_KR], kvg_ref[0]).astype(jnp.bfloat16)
    kr = lat[:, C_KR:C_LAT_END]
    angt = invfc_ref[...] * posr_ref[0].astype(jnp.float32)
    cos_t, sin_t = jnp.cos(angt), jnp.sin(angt)
    cs = jnp.concatenate([cos_t, cos_t, sin_t, sin_t], axis=0).T
    lane = jax.lax.broadcasted_iota(jnp.int32, (tm, LANES), 1)
    t = kr * cs
    kroped = jnp.where(lane < ROPE, t + pltpu.roll(t, ROPE, axis=1), 0.0).astype(jnp.bfloat16)

    kn = _dot(kvn, w_uk_ref[0])
    vt = _dot_nt(w_uvt_ref[0], kvn)
    qt = _dot_nt(w_uqt_ref[0], qn)
    gpool = _dot(hb, w_mix_ref[0, :, R_GPOOL:R_CH])

    for hh in range(N_HEADS):
        k_ref[0, hh] = jnp.concatenate(
            [kn[:, hh * NOPE:(hh + 1) * NOPE].astype(jnp.bfloat16), kroped], axis=1)
        vt_ref[0, hh, 0] = vt[hh * V_DIM:(hh + 1) * V_DIM, :].astype(jnp.bfloat16)

    u = cv[:, 2 * D_CONV:3 * D_CONV] * cv[:, 0:D_CONV]
    conv_ext[CONV_HALO:CONV_HALO + tm, :] = u
    yc = convw_ref[0, 2:3, :] * u
    yc = yc + convw_ref[0, 1:2, :] * conv_ext[CONV_HALO - 1:CONV_HALO - 1 + tm, :]
    yc = yc + convw_ref[0, 0:1, :] * conv_ext[CONV_HALO - 2:CONV_HALO - 2 + tm, :]
    y_conv = cv[:, D_CONV:2 * D_CONV] * yc * _silu(cv[:, 3 * D_CONV:4 * D_CONV])
    pc_ref[0, :, D_POOL:] = y_conv.astype(jnp.bfloat16)

    pool_ext[POOL_HALO:POOL_HALO + tm, :] = pin
    t1 = (i * tm + 1 + jax.lax.broadcasted_iota(jnp.int32, (tm, 1), 0)).astype(jnp.float32)
    ys = []
    for g, w in enumerate(POOL_WINDOWS):
        lo = g * POOL_GROUP
        acc = pin[:, lo:lo + POOL_GROUP]
        for s in range(1, w):
            acc = acc + pool_ext[POOL_HALO - s:POOL_HALO - s + tm, lo:lo + POOL_GROUP]
        pooled = acc / jnp.minimum(t1, float(w)) - pin[:, lo:lo + POOL_GROUP]
        ys.append(_dot(pooled.astype(jnp.bfloat16), w_pool_ref[0, g]))
    gm = _dot(hb, w_mix_ref[0, :, R_GMLA:R_PIN])
    y_pool = jnp.concatenate(ys, axis=1) * pscale_ref[0] * _silu(gpool)
    pc_ref[0, :, :D_POOL] = y_pool.astype(jnp.bfloat16)

    cos_t = jnp.concatenate([cos_t, cos_t], axis=0)
    sin_t = jnp.concatenate([sin_t, sin_t], axis=0)
    zeros_t = jnp.zeros((HEAD_W - NOPE - ROPE, tm), jnp.float32)
    for hh in range(N_HEADS):
        r0 = hh * HEAD_W
        roped = qt[r0 + NOPE:r0 + NOPE + ROPE, :] * cos_t + qt[r0 + NOPE + ROPE:r0 + HEAD_W, :] * sin_t
        qh = jnp.concatenate([qt[r0:r0 + NOPE, :], roped, zeros_t], axis=0)
        qt_ref[0, hh] = (qh * Q_PRESCALE).astype(jnp.bfloat16)

    gm_ref[0] = _silu(gm).astype(jnp.bfloat16)


def _const_spec(shape):
    nd = len(shape)
    return pl.BlockSpec(shape, lambda b, i: (0,) * nd, pipeline_mode=pl.Buffered(1))


def _layer_spec(arr, layer):
    nd = arr.ndim
    return pl.BlockSpec((1,) + arr.shape[1:], lambda b, i: (layer,) + (0,) * (nd - 1),
                        pipeline_mode=pl.Buffered(1))


def _proj_call(layer, x, posr, emb_g, emb_b, inv_freq_col, stacked):
    first = layer == 0
    B, S, D = x.shape
    tm = PROJ_TM
    grid = (B, S // tm)
    tok = lambda w: pl.BlockSpec((1, tm, w), lambda b, i: (b, i, 0))
    in_specs = [tok(D), pl.BlockSpec((1, 1, tm), lambda b, i: (b, 0, i))]
    args = [x, posr]
    if first:
        in_specs += [_const_spec((1, D)), _const_spec((1, D))]
        args += [emb_g, emb_b]
    in_specs += [_layer_spec(a, layer) for a in stacked] + [_const_spec(inv_freq_col.shape)]
    args += list(stacked) + [inv_freq_col]
    bf = jnp.bfloat16
    out_shape = [
        jax.ShapeDtypeStruct((B, N_HEADS, HEAD_W, S), bf),
        jax.ShapeDtypeStruct((B, N_HEADS, S, HEAD_W), bf),
        jax.ShapeDtypeStruct((B, N_HEADS, S // KT, V_DIM, KT), bf),
        jax.ShapeDtypeStruct((B, S, D_MLA), bf),
        jax.ShapeDtypeStruct((B, S, D_POOL + D_CONV), bf),
    ]
    out_specs = [
        pl.BlockSpec((1, N_HEADS, HEAD_W, tm), lambda b, i: (b, 0, 0, i)),
        pl.BlockSpec((1, N_HEADS, tm, HEAD_W), lambda b, i: (b, 0, i, 0)),
        pl.BlockSpec((1, N_HEADS, 1, V_DIM, KT), lambda b, i: (b, 0, i, 0, 0)),
        tok(D_MLA), tok(D_POOL + D_CONV)]
    if first:
        out_shape.append(jax.ShapeDtypeStruct((B, S, D), jnp.float32))
        out_specs.append(tok(D))
    return pl.pallas_call(
        functools.partial(_proj_kernel, first),
        grid=grid,
        in_specs=in_specs,
        out_specs=out_specs,
        out_shape=out_shape,
        scratch_shapes=[
            pltpu.VMEM((POOL_HALO + tm, D_POOL), jnp.float32),
            pltpu.VMEM((CONV_HALO + tm, D_CONV), jnp.float32),
        ],
        compiler_params=pltpu.CompilerParams(
            dimension_semantics=("arbitrary", "arbitrary"),
            vmem_limit_bytes=VMEM_LIMIT_BYTES),
        name="proj_first" if first else "proj",
    )(*args)


def _rows8_reduce(x, op):
    acc = x[0:SUBLANES]
    for r in range(SUBLANES, x.shape[0], SUBLANES):
        acc = op(acc, x[r:r + SUBLANES])
    return acc


def _attn_kernel(qt_ref, k_ref, vt_ref, gm_ref, o_ref,
                 s0_sc, s1_sc, mt0_sc, mt1_sc, p_sc, m_sc, l_sc, acc_sc):
    tq, tk, G, ch = ATT_TQ, ATT_TK, ATT_G, ATT_CH
    sub = tk // KT
    i = pl.program_id(2)

    def scores_into(s_ref, mt_ref, n):
        start = pl.multiple_of(n * tk, tk)
        for g in range(G):
            s = _dot(k_ref[0, g, pl.ds(start, tk), :], qt_ref[0, g])
            s_ref[g] = s
            mt_ref[g] = _rows8_reduce(s, jnp.maximum)

    def update(g, n, s_ref, m_tile8, mask):
        m_prev = m_sc[g]
        m_new = jnp.maximum(m_prev, jnp.max(m_tile8, axis=0, keepdims=True))
        alpha = jnp.exp2(m_prev - m_new)
        lsum = jnp.zeros((SUBLANES, tq), jnp.float32)
        for c in range(tk // ch):
            s_c = s_ref[g, c * ch:(c + 1) * ch, :]
            if mask is not None:
                s_c = jnp.where(mask(c), s_c, -jnp.inf)
            p_c = jnp.exp2(s_c - m_new)
            lsum = lsum + _rows8_reduce(p_c, jnp.add)
            p_sc[g, c * ch:(c + 1) * ch, :] = p_c.astype(jnp.bfloat16)
        l_sc[g] = alpha * l_sc[g] + jnp.sum(lsum, axis=0, keepdims=True)
        vt = jnp.concatenate([vt_ref[0, g, n * sub + c] for c in range(sub)], axis=1)
        acc_sc[g] = alpha * acc_sc[g] + _dot(vt, p_sc[g])
        m_sc[g] = m_new

    m_sc[...] = jnp.full(m_sc.shape, M_INIT, jnp.float32)
    l_sc[...] = jnp.zeros(l_sc.shape, jnp.float32)
    acc_sc[...] = jnp.zeros(acc_sc.shape, jnp.float32)
    scores_into(s0_sc, mt0_sc, 0)

    def full_tile(cur, nxt, n):
        scores_into(nxt[0], nxt[1], n + 1)
        for g in range(G):
            update(g, n, cur[0], cur[1][g], None)

    even, odd = (s0_sc, mt0_sc), (s1_sc, mt1_sc)

    def body(pair, carry):
        n = 2 * pair
        full_tile(even, odd, n)

        @pl.when(n + 1 < i)
        def _():
            full_tile(odd, even, n + 1)
        return carry

    jax.lax.fori_loop(0, (i + 1) // 2, body, 0)

    def diag_tile(cur):
        def mask(c):
            key = c * ch + jax.lax.broadcasted_iota(jnp.int32, (ch, tq), 0)
            return key <= jax.lax.broadcasted_iota(jnp.int32, (ch, tq), 1)
        for g in range(G):
            m8 = jnp.full((SUBLANES, tq), -jnp.inf, jnp.float32)
            for c in range(tk // ch):
                s_c = jnp.where(mask(c), cur[0][g, c * ch:(c + 1) * ch, :], -jnp.inf)
                m8 = jnp.maximum(m8, _rows8_reduce(s_c, jnp.maximum))
            update(g, i, cur[0], m8, mask)

    @pl.when(i % 2 == 0)
    def _():
        diag_tile(even)

    @pl.when(i % 2 == 1)
    def _():
        diag_tile(odd)

    for g in range(G):
        out_t = acc_sc[g] / l_sc[g]
        out = out_t.T * gm_ref[0, :, g * V_DIM:(g + 1) * V_DIM].astype(jnp.float32)
        o_ref[0, :, g * V_DIM:(g + 1) * V_DIM] = out.astype(o_ref.dtype)


def _attn_call(qt, k, vt, gm):
    B, H, S, _ = k.shape
    tq, G = ATT_TQ, ATT_G
    assert ATT_TQ == ATT_TK and ATT_TK % KT == 0
    grid = (B, H // G, S // tq)
    return pl.pallas_call(
        _attn_kernel,
        grid=grid,
        in_specs=[
            pl.BlockSpec((1, G, HEAD_W, tq), lambda b, h, i: (b, h, 0, i)),
            pl.BlockSpec((1, G, S, HEAD_W), lambda b, h, i: (b, h, 0, 0)),
            pl.BlockSpec((1, G, S // KT, V_DIM, KT), lambda b, h, i: (b, h, 0, 0, 0)),
            pl.BlockSpec((1, tq, G * V_DIM), lambda b, h, i: (b, i, h)),
        ],
        out_specs=pl.BlockSpec((1, tq, G * V_DIM), lambda b, h, i: (b, i, h)),
        out_shape=jax.ShapeDtypeStruct((B, S, D_MLA), jnp.bfloat16),
        scratch_shapes=[
            pltpu.VMEM((G, ATT_TK, tq), jnp.float32),
            pltpu.VMEM((G, ATT_TK, tq), jnp.float32),
            pltpu.VMEM((G, SUBLANES, tq), jnp.float32),
            pltpu.VMEM((G, SUBLANES, tq), jnp.float32),
            pltpu.VMEM((G, ATT_TK, tq), jnp.bfloat16),
            pltpu.VMEM((G, 1, tq), jnp.float32),
            pltpu.VMEM((G, 1, tq), jnp.float32),
            pltpu.VMEM((G, V_DIM, tq), jnp.float32),
        ],
        compiler_params=pltpu.CompilerParams(
            dimension_semantics=("arbitrary", "arbitrary", "arbitrary"),
            vmem_limit_bytes=VMEM_LIMIT_BYTES),
        name="attn",
    )(qt, k, vt, gm)


def _out_kernel(o_ref, pc_ref, h_ref, w_ref, b_ref, g_ref, beta_ref, y_ref):
    for r in range(0, OUT_TM, OUT_CH):
        rows = slice(r, r + OUT_CH)
        mix = jnp.concatenate([o_ref[0, rows, :], pc_ref[0, rows, :]], axis=1)
        y = _dot(mix, w_ref[0]) + b_ref[0] + DEEPNORM_ALPHA * h_ref[0, rows, :]
        y_ref[0, rows, :] = _layernorm(y, g_ref[0], beta_ref[0])


def _out_call(layer, o, pc, h, w_out, b_out, ln_g, ln_b):
    B, S, D = h.shape
    tm = OUT_TM
    tok = lambda w: pl.BlockSpec((1, tm, w), lambda b, i: (b, i, 0))
    return pl.pallas_call(
        _out_kernel,
        grid=(B, S // tm),
        in_specs=[tok(D_MLA), tok(D_POOL + D_CONV), tok(D),
                  _layer_spec(w_out, layer), _layer_spec(b_out, layer), _layer_spec(ln_g, layer),
                  _layer_spec(ln_b, layer)],
        out_specs=tok(D),
        out_shape=jax.ShapeDtypeStruct((B, S, D), jnp.float32),
        compiler_params=pltpu.CompilerParams(
            dimension_semantics=("arbitrary", "arbitrary"),
            vmem_limit_bytes=VMEM_LIMIT_BYTES),
        name="out",
    )(o, pc, h, w_out, b_out, ln_g, ln_b)


def _rotate_half_cols(w):
    return jnp.concatenate([-w[..., HALF:], w[..., :HALF]], axis=-1)


def _wprep_lat_kernel(wt_ref, o_ref):
    x = wt_ref[0]
    kr = C_KR
    rot = jnp.concatenate([-x[kr + HALF:kr + ROPE], x[kr:kr + HALF]], axis=0)
    o_ref[0] = jnp.concatenate([x, rot], axis=0).T.astype(jnp.bfloat16)


def _wprep_mix_kernel(wt_ref, o_ref):
    o_ref[0] = wt_ref[0].T.astype(jnp.bfloat16)


def _prep_w_in(w):
    L, D, C = w.shape
    wt = jnp.swapaxes(w, 1, 2)
    n_lat = C_KR + ROPE
    params = pltpu.CompilerParams(vmem_limit_bytes=VMEM_LIMIT_BYTES)
    w_lat = pl.pallas_call(
        _wprep_lat_kernel,
        grid=(L,),
        in_specs=[pl.BlockSpec((pl.Element(1), pl.Element(n_lat), pl.Element(D)), lambda l: (l, 0, 0))],
        out_specs=pl.BlockSpec((1, D, C_LAT_END), lambda l: (l, 0, 0)),
        out_shape=jax.ShapeDtypeStruct((L, D, C_LAT_END), jnp.bfloat16),
        compiler_params=params,
        name="wprep_lat",
    )(wt)
    w_mix = pl.pallas_call(
        _wprep_mix_kernel,
        grid=(L, R_END // WPREP_COLS),
        in_specs=[pl.BlockSpec((pl.Element(1), pl.Element(WPREP_COLS), pl.Element(D)),
                               lambda l, j: (l, pl.multiple_of(n_lat + WPREP_COLS * j, SUBLANES), 0))],
        out_specs=pl.BlockSpec((1, D, WPREP_COLS), lambda l, j: (l, 0, j)),
        out_shape=jax.ShapeDtypeStruct((L, D, R_END), jnp.bfloat16),
        compiler_params=params,
        name="wprep_mix",
    )(wt)
    return w_lat, w_mix


def _prep_w_uq_t(w):
    L = w.shape[0]
    w = w.reshape(L, Q_LORA, N_HEADS, NOPE + ROPE)
    w = jnp.concatenate([w, _rotate_half_cols(w[..., NOPE:])], axis=-1)
    return jnp.swapaxes(w.reshape(L, Q_LORA, N_HEADS * HEAD_W), 1, 2).astype(jnp.bfloat16)


def _prep_w_ukv(w):
    L = w.shape[0]
    w = w.reshape(L, KV_LORA, N_HEADS, NOPE + V_DIM)
    wk = w[..., :NOPE].reshape(L, KV_LORA, N_HEADS * NOPE)
    wv = w[..., NOPE:].reshape(L, KV_LORA, N_HEADS * V_DIM)
    return wk.astype(jnp.bfloat16), jnp.swapaxes(wv, 1, 2).astype(jnp.bfloat16)


def kernel(x, positions, emb_ln_g, emb_ln_b, w_in, q_norm_g, kv_norm_g, w_uq, w_ukv, w_pool,
           pool_scale, conv_w, w_out, b_out, ln_g, ln_b):
    B, S, D = x.shape
    bf = jnp.bfloat16
    posr = positions.reshape(B, 1, S)
    inv_freq = ROPE_THETA ** (-jnp.arange(HALF, dtype=jnp.float32) / HALF)
    invfc = inv_freq.reshape(HALF, 1)
    row = lambda a: a.reshape(1, -1)

    w_lat, w_mix = _prep_w_in(w_in)
    w_uk, w_uvt = _prep_w_ukv(w_ukv)
    row3 = lambda a: a.reshape(a.shape[0], 1, -1)
    stacked = (w_lat, w_mix, row3(q_norm_g), row3(kv_norm_g), _prep_w_uq_t(w_uq), w_uk, w_uvt,
               w_pool.astype(bf), row3(pool_scale), conv_w)
    w_out_b = w_out.astype(bf)

    h = x
    for l in range(DEPTH):
        outs = _proj_call(l, h, posr, row(emb_ln_g), row(emb_ln_b), invfc, stacked)
        if l == 0:
            qt, k, vt, gm, pc, h = outs
        else:
            qt, k, vt, gm, pc = outs
        o = _attn_call(qt, k, vt, gm)
        h = _out_call(l, o, pc, h, w_out_b, row3(b_out), row3(ln_g), row3(ln_b))
    return h
```

```python
import functools
import math

import jax
import jax.numpy as jnp
import numpy as np
from jax.experimental import pallas as pl
from jax.experimental.pallas import tpu as pltpu

D_MODEL = 2048
DEPTH = 2
N_HEADS = 8
NOPE = 128
ROPE = 64
V_DIM = 128
Q_LORA = 512
KV_LORA = 256
D_MLA = N_HEADS * V_DIM
ROPE_THETA = 10000.0
POOL_WINDOWS = (2, 4, 8, 16)
POOL_GROUP = 128
D_POOL = 512
D_CONV = 512
CONV_WIDTH = 3
LN_EPS = 1e-5
RMS_EPS = 1e-6
DEEPNORM_ALPHA = (2 * DEPTH) ** 0.25

LANES = 128
SUBLANES = 8
VMEM_LIMIT_BYTES = 56 * 1024 * 1024

PROJ_TM = 256
ATT_TQ = 512
ATT_TK = 512
ATT_G = 2
ATT_CH = 64
WPREP_COLS = 512
OUT_TM = 512
OUT_CH = 128
POOL_HALO = 16
CONV_HALO = 8

KT = PROJ_TM
HEAD_W = 2 * LANES
HALF = ROPE // 2

C_QLAT = 0
C_KVLAT = C_QLAT + Q_LORA
C_KR = C_KVLAT + KV_LORA
C_LAT_END = C_KR + LANES
R_GMLA = 0
R_PIN = R_GMLA + D_MLA
R_GPOOL = R_PIN + D_POOL
R_CH = R_GPOOL + D_POOL
R_END = R_CH + 4 * D_CONV

Q_PRESCALE = (NOPE + ROPE) ** -0.5 * math.log2(math.e)
M_INIT = -1e30
BIAS_ROW0 = NOPE + ROPE
BIAS_ROWS = 16
SHIFT_L_MIN = 2.0 ** -100
SHIFT_L_MAX = 2.0 ** 100


def _silu(g):
    return g * (1.0 / (1.0 + jnp.exp(-g)))


def _layernorm(x, g, b):
    mu = jnp.mean(x, axis=-1, keepdims=True)
    xc = x - mu
    var = jnp.mean(xc * xc, axis=-1, keepdims=True)
    return xc * jax.lax.rsqrt(var + LN_EPS) * g + b


def _rmsnorm(x, g):
    return x * jax.lax.rsqrt(jnp.mean(x * x, axis=-1, keepdims=True) + RMS_EPS) * g


def _dot(a, b):
    return jnp.dot(a, b, preferred_element_type=jnp.float32)


def _dot_nt(a, b):
    return jax.lax.dot_general(a, b, (((1,), (1,)), ((), ())), preferred_element_type=jnp.float32)


def _proj_kernel(first, *refs):
    if first:
        (x_ref, posr_ref, eg_ref, eb_ref, w_lat_ref, w_mix_ref, qg_ref, kvg_ref, w_uqt_ref, w_uk_ref,
         w_uvt_ref, w_pool_ref, pscale_ref, convw_ref, invfc_ref,
         qt_ref, k_ref, vt_ref, gm_ref, pc_ref, h_ref, pool_ext, conv_ext) = refs
    else:
        (x_ref, posr_ref, w_lat_ref, w_mix_ref, qg_ref, kvg_ref, w_uqt_ref, w_uk_ref,
         w_uvt_ref, w_pool_ref, pscale_ref, convw_ref, invfc_ref,
         qt_ref, k_ref, vt_ref, gm_ref, pc_ref, pool_ext, conv_ext) = refs
    tm = PROJ_TM
    i = pl.program_id(1)

    @pl.when(i == 0)
    def _():
        pool_ext[0:POOL_HALO, :] = jnp.zeros((POOL_HALO, D_POOL), jnp.float32)
        conv_ext[0:CONV_HALO, :] = jnp.zeros((CONV_HALO, D_CONV), jnp.float32)

    @pl.when(i > 0)
    def _():
        pool_ext[0:POOL_HALO, :] = pool_ext[tm:tm + POOL_HALO, :]
        conv_ext[0:CONV_HALO, :] = conv_ext[tm:tm + CONV_HALO, :]

    x = x_ref[0]
    if first:
        x = _layernorm(x, eg_ref[...], eb_ref[...])
        h_ref[0] = x
    hb = x.astype(jnp.bfloat16)

    lat = _dot(hb, w_lat_ref[0])
    pin = _dot(hb, w_mix_ref[0, :, R_PIN:R_GPOOL])
    cv = _dot(hb, w_mix_ref[0, :, R_CH:R_END])

    qn = _rmsnorm(lat[:, C_QLAT:C_KVLAT], qg_ref[0]).astype(jnp.bfloat16)
    kvn = _rmsnorm(lat[:, C_KVLAT:C_KR], kvg_ref[0]).astype(jnp.bfloat16)
    kr = lat[:, C_KR:C_LAT_END]
    angt = invfc_ref[...] * posr_ref[0].astype(jnp.float32)
    cos_t, sin_t = jnp.cos(angt), jnp.sin(angt)
    cs = jnp.concatenate([cos_t, cos_t, sin_t, sin_t], axis=0).T
    lane = jax.lax.broadcasted_iota(jnp.int32, (tm, LANES), 1)
    t = kr * cs
    ones = jnp.where(lane < ROPE + BIAS_ROWS, 1.0, 0.0)
    kroped = jnp.where(lane < ROPE, t + pltpu.roll(t, ROPE, axis=1), ones).astype(jnp.bfloat16)

    kn = _dot(kvn, w_uk_ref[0])
    vt = _dot_nt(w_uvt_ref[0], kvn)
    qt = _dot_nt(w_uqt_ref[0], qn)
    gpool = _dot(hb, w_mix_ref[0, :, R_GPOOL:R_CH])

    for hh in range(N_HEADS):
        k_ref[0, hh] = jnp.concatenate(
            [kn[:, hh * NOPE:(hh + 1) * NOPE].astype(jnp.bfloat16), kroped], axis=1)
        vt_ref[0, hh, 0] = vt[hh * V_DIM:(hh + 1) * V_DIM, :].astype(jnp.bfloat16)

    u = cv[:, 2 * D_CONV:3 * D_CONV] * cv[:, 0:D_CONV]
    conv_ext[CONV_HALO:CONV_HALO + tm, :] = u
    yc = convw_ref[0, 2:3, :] * u
    yc = yc + convw_ref[0, 1:2, :] * conv_ext[CONV_HALO - 1:CONV_HALO - 1 + tm, :]
    yc = yc + convw_ref[0, 0:1, :] * conv_ext[CONV_HALO - 2:CONV_HALO - 2 + tm, :]
    y_conv = cv[:, D_CONV:2 * D_CONV] * yc * _silu(cv[:, 3 * D_CONV:4 * D_CONV])
    pc_ref[0, :, D_POOL:] = y_conv.astype(jnp.bfloat16)

    pool_ext[POOL_HALO:POOL_HALO + tm, :] = pin
    t1 = (i * tm + 1 + jax.lax.broadcasted_iota(jnp.int32, (tm, 1), 0)).astype(jnp.float32)
    ys = []
    for g, w in enumerate(POOL_WINDOWS):
        lo = g * POOL_GROUP
        acc = pin[:, lo:lo + POOL_GROUP]
        for s in range(1, w):
            acc = acc + pool_ext[POOL_HALO - s:POOL_HALO - s + tm, lo:lo + POOL_GROUP]
        pooled = acc / jnp.minimum(t1, float(w)) - pin[:, lo:lo + POOL_GROUP]
        ys.append(_dot(pooled.astype(jnp.bfloat16), w_pool_ref[0, g]))
    gm = _dot(hb, w_mix_ref[0, :, R_GMLA:R_PIN])
    y_pool = jnp.concatenate(ys, axis=1) * pscale_ref[0] * _silu(gpool)
    pc_ref[0, :, :D_POOL] = y_pool.astype(jnp.bfloat16)

    cos_t = jnp.concatenate([cos_t, cos_t], axis=0)
    sin_t = jnp.concatenate([sin_t, sin_t], axis=0)
    zeros_t = jnp.zeros((HEAD_W - NOPE - ROPE, tm), jnp.float32)
    for hh in range(N_HEADS):
        r0 = hh * HEAD_W
        roped = qt[r0 + NOPE:r0 + NOPE + ROPE, :] * cos_t + qt[r0 + NOPE + ROPE:r0 + HEAD_W, :] * sin_t
        qh = jnp.concatenate([qt[r0:r0 + NOPE, :], roped, zeros_t], axis=0)
        qt_ref[0, hh] = (qh * Q_PRESCALE).astype(jnp.bfloat16)

    gm_ref[0] = _silu(gm).astype(jnp.bfloat16)


def _const_spec(shape):
    nd = len(shape)
    return pl.BlockSpec(shape, lambda b, i: (0,) * nd, pipeline_mode=pl.Buffered(1))


def _layer_spec(arr, layer):
    nd = arr.ndim
    return pl.BlockSpec((1,) + arr.shape[1:], lambda b, i: (layer,) + (0,) * (nd - 1),
                        pipeline_mode=pl.Buffered(1))


def _proj_call(layer, x, posr, emb_g, emb_b, inv_freq_col, stacked):
    first = layer == 0
    B, S, D = x.shape
    tm = PROJ_TM
    grid = (B, S // tm)
    tok = lambda w: pl.BlockSpec((1, tm, w), lambda b, i: (b, i, 0))
    in_specs = [tok(D), pl.BlockSpec((1, 1, tm), lambda b, i: (b, 0, i))]
    args = [x, posr]
    if first:
        in_specs += [_const_spec((1, D)), _const_spec((1, D))]
        args += [emb_g, emb_b]
    in_specs += [_layer_spec(a, layer) for a in stacked] + [_const_spec(inv_freq_col.shape)]
    args += list(stacked) + [inv_freq_col]
    bf = jnp.bfloat16
    out_shape = [
        jax.ShapeDtypeStruct((B, N_HEADS, HEAD_W, S), bf),
        jax.ShapeDtypeStruct((B, N_HEADS, S, HEAD_W), bf),
        jax.ShapeDtypeStruct((B, N_HEADS, S // KT, V_DIM, KT), bf),
        jax.ShapeDtypeStruct((B, S, D_MLA), bf),
        jax.ShapeDtypeStruct((B, S, D_POOL + D_CONV), bf),
    ]
    out_specs = [
        pl.BlockSpec((1, N_HEADS, HEAD_W, tm), lambda b, i: (b, 0, 0, i)),
        pl.BlockSpec((1, N_HEADS, tm, HEAD_W), lambda b, i: (b, 0, i, 0)),
        pl.BlockSpec((1, N_HEADS, 1, V_DIM, KT), lambda b, i: (b, 0, i, 0, 0)),
        tok(D_MLA), tok(D_POOL + D_CONV)]
    if first:
        out_shape.append(jax.ShapeDtypeStruct((B, S, D), jnp.float32))
        out_specs.append(tok(D))
    return pl.pallas_call(
        functools.partial(_proj_kernel, first),
        grid=grid,
        in_specs=in_specs,
        out_specs=out_specs,
        out_shape=out_shape,
        scratch_shapes=[
            pltpu.VMEM((POOL_HALO + tm, D_POOL), jnp.float32),
            pltpu.VMEM((CONV_HALO + tm, D_CONV), jnp.float32),
        ],
        compiler_params=pltpu.CompilerParams(
            dimension_semantics=("arbitrary", "arbitrary"),
            vmem_limit_bytes=VMEM_LIMIT_BYTES),
        name="proj_first" if first else "proj",
    )(*args)


def _rows8_reduce(x, op):
    acc = x[0:SUBLANES]
    for r in range(SUBLANES, x.shape[0], SUBLANES):
        acc = op(acc, x[r:r + SUBLANES])
    return acc


def _attn_kernel(qt_ref, k_ref, vt_ref, gm_ref, o_ref,
                 qb_sc, s_sc, p0_sc, p1_sc, l8_sc, acc_sc, m_sc, l_sc):
    tq, tk, G, ch = ATT_TQ, ATT_TK, ATT_G, ATT_CH
    sub, nch = tk // KT, tk // ch
    i = pl.program_id(2)

    def vt_tile(g, n):
        return jnp.concatenate([vt_ref[0, g, n * sub + c] for c in range(sub)], axis=1)

    def rows(c):
        return slice(c * ch, (c + 1) * ch)

    def causal(c):
        key = c * ch + jax.lax.broadcasted_iota(jnp.int32, (ch, tq), 0)
        return key <= jax.lax.broadcasted_iota(jnp.int32, (ch, tq), 1)

    def masked_chunk(g, c, masked):
        s_c = s_sc[g, rows(c), :]
        return jnp.where(causal(c), s_c, -jnp.inf) if masked else s_c

    def tile_max8(g, masked):
        m8 = masked_chunk(g, 0, masked)
        m8 = _rows8_reduce(m8, jnp.maximum)
        for c in range(1, nch):
            m8 = jnp.maximum(m8, _rows8_reduce(masked_chunk(g, c, masked), jnp.maximum))
        return m8

    def exp_chunks(g, chunk_of, p_ref, lsum):
        for c in range(nch):
            p_c = jnp.exp2(chunk_of(c))
            lsum = lsum + _rows8_reduce(p_c, jnp.add)
            p_ref[g, rows(c), :] = p_c.astype(jnp.bfloat16)
        return lsum

    d0 = pl.multiple_of(i * tk, tk)
    for g in range(G):
        qb_sc[g] = qt_ref[0, g]
        s_sc[g] = _dot(k_ref[0, g, pl.ds(d0, tk), :], qt_ref[0, g])
    for g in range(G):
        m_d = jnp.max(tile_max8(g, True), axis=0, keepdims=True)
        m_ref = m_d.astype(jnp.bfloat16).astype(jnp.float32)
        l8_sc[g] = exp_chunks(g, lambda c: masked_chunk(g, c, True) - m_ref, p1_sc,
                              jnp.zeros((SUBLANES, tq), jnp.float32))
        bias = jnp.broadcast_to(m_ref * (-1.0 / BIAS_ROWS), (BIAS_ROWS, tq))
        qb_sc[g, BIAS_ROW0:BIAS_ROW0 + BIAS_ROWS, :] = bias.astype(jnp.bfloat16)
    acc_sc[...] = jnp.zeros(acc_sc.shape, jnp.float32)

    def qk_exp(g, p_ref, n):
        start = pl.multiple_of(n * tk, tk)
        s = _dot(k_ref[0, g, pl.ds(start, tk), :], qb_sc[g])
        l8_sc[g] = exp_chunks(g, lambda c: s[rows(c)], p_ref, l8_sc[g])

    def pv(g, p_ref, n):
        acc_sc[g] = acc_sc[g] + _dot(vt_tile(g, n), p_ref[g])

    def body(pair, carry):
        n = 2 * pair
        before = jnp.where(n == 0, i, n - 1)
        for g in range(G):
            qk_exp(g, p0_sc, n)
            pv(g, p1_sc, before)

        @pl.when(n + 1 < i)
        def _():
            for g in range(G):
                qk_exp(g, p1_sc, n + 1)
                pv(g, p0_sc, n)
        return carry

    jax.lax.fori_loop(0, (i + 1) // 2, body, 0)

    @pl.when(i % 2 == 1)
    def _():
        for g in range(G):
            pv(g, p0_sc, i - 1)

    @pl.when(i % 2 == 0)
    def _():
        for g in range(G):
            pv(g, p1_sc, jnp.maximum(i - 1, 0))

    l_min, l_max = None, None
    for g in range(G):
        l = jnp.sum(l8_sc[g], axis=0, keepdims=True)
        l_sc[g] = l
        l_min = jnp.min(l) if l_min is None else jnp.minimum(l_min, jnp.min(l))
        l_max = jnp.max(l) if l_max is None else jnp.maximum(l_max, jnp.max(l))
    in_range = jnp.logical_and(l_min > SHIFT_L_MIN, l_max < SHIFT_L_MAX)

    @pl.when(jnp.logical_not(in_range))
    def _():
        m_sc[...] = jnp.full(m_sc.shape, M_INIT, jnp.float32)
        l_sc[...] = jnp.zeros(l_sc.shape, jnp.float32)
        acc_sc[...] = jnp.zeros(acc_sc.shape, jnp.float32)

        def tile(n, masked):
            start = pl.multiple_of(n * tk, tk)
            for g in range(G):
                s_sc[g] = _dot(k_ref[0, g, pl.ds(start, tk), :], qt_ref[0, g])
                m_prev = m_sc[g]
                m_new = jnp.maximum(m_prev, jnp.max(tile_max8(g, masked), axis=0, keepdims=True))
                alpha = jnp.exp2(m_prev - m_new)
                lsum = exp_chunks(g, lambda c: masked_chunk(g, c, masked) - m_new, p0_sc,
                                  jnp.zeros((SUBLANES, tq), jnp.float32))
                l_sc[g] = alpha * l_sc[g] + jnp.sum(lsum, axis=0, keepdims=True)
                acc_sc[g] = alpha * acc_sc[g] + _dot(vt_tile(g, n), p0_sc[g])
                m_sc[g] = m_new

        def full(n, carry):
            tile(n, False)
            return carry

        jax.lax.fori_loop(0, i, full, 0)
        tile(i, True)

    for g in range(G):
        out_t = acc_sc[g] / l_sc[g]
        out = out_t.T * gm_ref[0, :, g * V_DIM:(g + 1) * V_DIM].astype(jnp.float32)
        o_ref[0, :, g * V_DIM:(g + 1) * V_DIM] = out.astype(o_ref.dtype)


def _attn_call(qt, k, vt, gm):
    B, H, S, _ = k.shape
    tq, G = ATT_TQ, ATT_G
    assert ATT_TQ == ATT_TK and ATT_TK % KT == 0
    grid = (B, H // G, S // tq)
    return pl.pallas_call(
        _attn_kernel,
        grid=grid,
        in_specs=[
            pl.BlockSpec((1, G, HEAD_W, tq), lambda b, h, i: (b, h, 0, i)),
            pl.BlockSpec((1, G, S, HEAD_W), lambda b, h, i: (b, h, 0, 0)),
            pl.BlockSpec((1, G, S // KT, V_DIM, KT), lambda b, h, i: (b, h, 0, 0, 0)),
            pl.BlockSpec((1, tq, G * V_DIM), lambda b, h, i: (b, i, h)),
        ],
        out_specs=pl.BlockSpec((1, tq, G * V_DIM), lambda b, h, i: (b, i, h)),
        out_shape=jax.ShapeDtypeStruct((B, S, D_MLA), jnp.bfloat16),
        scratch_shapes=[
            pltpu.VMEM((G, HEAD_W, tq), jnp.bfloat16),
            pltpu.VMEM((G, ATT_TK, tq), jnp.float32),
            pltpu.VMEM((G, ATT_TK, tq), jnp.bfloat16),
            pltpu.VMEM((G, ATT_TK, tq), jnp.bfloat16),
            pltpu.VMEM((G, SUBLANES, tq), jnp.float32),
            pltpu.VMEM((G, V_DIM, tq), jnp.float32),
            pltpu.VMEM((G, 1, tq), jnp.float32),
            pltpu.VMEM((G, 1, tq), jnp.float32),
        ],
        compiler_params=pltpu.CompilerParams(
            dimension_semantics=("arbitrary", "arbitrary", "arbitrary"),
            vmem_limit_bytes=VMEM_LIMIT_BYTES),
        name="attn",
    )(qt, k, vt, gm)


def _out_kernel(o_ref, pc_ref, h_ref, w_ref, b_ref, g_ref, beta_ref, y_ref):
    for r in range(0, OUT_TM, OUT_CH):
        rows = slice(r, r + OUT_CH)
        mix = jnp.concatenate([o_ref[0, rows, :], pc_ref[0, rows, :]], axis=1)
        y = _dot(mix, w_ref[0]) + b_ref[0] + DEEPNORM_ALPHA * h_ref[0, rows, :]
        y_ref[0, rows, :] = _layernorm(y, g_ref[0], beta_ref[0])


def _out_call(layer, o, pc, h, w_out, b_out, ln_g, ln_b):
    B, S, D = h.shape
    tm = OUT_TM
    tok = lambda w: pl.BlockSpec((1, tm, w), lambda b, i: (b, i, 0))
    return pl.pallas_call(
        _out_kernel,
        grid=(B, S // tm),
        in_specs=[tok(D_MLA), tok(D_POOL + D_CONV), tok(D),
                  _layer_spec(w_out, layer), _layer_spec(b_out, layer), _layer_spec(ln_g, layer),
                  _layer_spec(ln_b, layer)],
        out_specs=tok(D),
        out_shape=jax.ShapeDtypeStruct((B, S, D), jnp.float32),
        compiler_params=pltpu.CompilerParams(
            dimension_semantics=("arbitrary", "arbitrary"),
            vmem_limit_bytes=VMEM_LIMIT_BYTES),
        name="out",
    )(o, pc, h, w_out, b_out, ln_g, ln_b)


def _rotate_half_cols(w):
    return jnp.concatenate([-w[..., HALF:], w[..., :HALF]], axis=-1)


def _wprep_lat_kernel(wt_ref, o_ref):
    x = wt_ref[0]
    kr = C_KR
    rot = jnp.concatenate([-x[kr + HALF:kr + ROPE], x[kr:kr + HALF]], axis=0)
    o_ref[0] = jnp.concatenate([x, rot], axis=0).T.astype(jnp.bfloat16)


def _wprep_mix_kernel(wt_ref, o_ref):
    o_ref[0] = wt_ref[0].T.astype(jnp.bfloat16)


def _prep_w_in(w):
    L, D, C = w.shape
    wt = jnp.swapaxes(w, 1, 2)
    n_lat = C_KR + ROPE
    params = pltpu.CompilerParams(vmem_limit_bytes=VMEM_LIMIT_BYTES)
    w_lat = pl.pallas_call(
        _wprep_lat_kernel,
        grid=(L,),
        in_specs=[pl.BlockSpec((pl.Element(1), pl.Element(n_lat), pl.Element(D)), lambda l: (l, 0, 0))],
        out_specs=pl.BlockSpec((1, D, C_LAT_END), lambda l: (l, 0, 0)),
        out_shape=jax.ShapeDtypeStruct((L, D, C_LAT_END), jnp.bfloat16),
        compiler_params=params,
        name="wprep_lat",
    )(wt)
    w_mix = pl.pallas_call(
        _wprep_mix_kernel,
        grid=(L, R_END // WPREP_COLS),
        in_specs=[pl.BlockSpec((pl.Element(1), pl.Element(WPREP_COLS), pl.Element(D)),
                               lambda l, j: (l, pl.multiple_of(n_lat + WPREP_COLS * j, SUBLANES), 0))],
        out_specs=pl.BlockSpec((1, D, WPREP_COLS), lambda l, j: (l, 0, j)),
        out_shape=jax.ShapeDtypeStruct((L, D, R_END), jnp.bfloat16),
        compiler_params=params,
        name="wprep_mix",
    )(wt)
    return w_lat, w_mix


def _prep_w_uq_t(w):
    L = w.shape[0]
    w = w.reshape(L, Q_LORA, N_HEADS, NOPE + ROPE)
    w = jnp.concatenate([w, _rotate_half_cols(w[..., NOPE:])], axis=-1)
    return jnp.swapaxes(w.reshape(L, Q_LORA, N_HEADS * HEAD_W), 1, 2).astype(jnp.bfloat16)


def _prep_w_ukv(w):
    L = w.shape[0]
    w = w.reshape(L, KV_LORA, N_HEADS, NOPE + V_DIM)
    wk = w[..., :NOPE].reshape(L, KV_LORA, N_HEADS * NOPE)
    wv = w[..., NOPE:].reshape(L, KV_LORA, N_HEADS * V_DIM)
    return wk.astype(jnp.bfloat16), jnp.swapaxes(wv, 1, 2).astype(jnp.bfloat16)


def kernel(x, positions, emb_ln_g, emb_ln_b, w_in, q_norm_g, kv_norm_g, w_uq, w_ukv, w_pool,
           pool_scale, conv_w, w_out, b_out, ln_g, ln_b):
    B, S, D = x.shape
    bf = jnp.bfloat16
    posr = positions.reshape(B, 1, S)
    inv_freq = ROPE_THETA ** (-jnp.arange(HALF, dtype=jnp.float32) / HALF)
    invfc = inv_freq.reshape(HALF, 1)
    row = lambda a: a.reshape(1, -1)

    w_lat, w_mix = _prep_w_in(w_in)
    w_uk, w_uvt = _prep_w_ukv(w_ukv)
    row3 = lambda a: a.reshape(a.shape[0], 1, -1)
    stacked = (w_lat, w_mix, row3(q_norm_g), row3(kv_norm_g), _prep_w_uq_t(w_uq), w_uk, w_uvt,
               w_pool.astype(bf), row3(pool_scale), conv_w)
    w_out_b = w_out.astype(bf)

    h = x
    for l in range(DEPTH):
        outs = _proj_call(l, h, posr, row(emb_ln_g), row(emb_ln_b), invfc, stacked)
        if l == 0:
            qt, k, vt, gm, pc, h = outs
        else:
            qt, k, vt, gm, pc = outs
        o = _attn_call(qt, k, vt, gm)
        h = _out_call(l, o, pc, h, w_out_b, row3(b_out), row3(ln_g), row3(ln_b))
    return h
```

```python
import functools
import math

import jax
import jax.numpy as jnp
import numpy as np
from jax.experimental import pallas as pl
from jax.experimental.pallas import tpu as pltpu

D_MODEL = 2048
DEPTH = 2
N_HEADS = 8
NOPE = 128
ROPE = 64
V_DIM = 128
Q_LORA = 512
KV_LORA = 256
D_MLA = N_HEADS * V_DIM
ROPE_THETA = 10000.0
POOL_WINDOWS = (2, 4, 8, 16)
POOL_GROUP = 128
D_POOL = 512
D_CONV = 512
CONV_WIDTH = 3
LN_EPS = 1e-5
RMS_EPS = 1e-6
DEEPNORM_ALPHA = (2 * DEPTH) ** 0.25

LANES = 128
SUBLANES = 8
VMEM_LIMIT_BYTES = 56 * 1024 * 1024

PROJ_TM = 256
ATT_TQ = 512
ATT_TK = 512
ATT_G = 2
ATT_CH = 64
WPREP_COLS = 512
OUT_TM = 512
OUT_CH = 128
POOL_HALO = 16
CONV_HALO = 8

KT = PROJ_TM
HEAD_W = 2 * LANES
HALF = ROPE // 2

C_QLAT = 0
C_KVLAT = C_QLAT + Q_LORA
C_KR = C_KVLAT + KV_LORA
C_LAT_END = C_KR + LANES
R_GMLA = 0
R_PIN = R_GMLA + D_MLA
R_GPOOL = R_PIN + D_POOL
R_CH = R_GPOOL + D_POOL
R_END = R_CH + 4 * D_CONV

Q_PRESCALE = (NOPE + ROPE) ** -0.5 * math.log2(math.e)
M_INIT = -1e30
BIAS_ROW0 = NOPE + ROPE
BIAS_ROWS = 16
SHIFT_L_MIN = 2.0 ** -100
SHIFT_L_MAX = 2.0 ** 100


def _silu(g):
    return g * (1.0 / (1.0 + jnp.exp(-g)))


def _layernorm(x, g, b):
    mu = jnp.mean(x, axis=-1, keepdims=True)
    xc = x - mu
    var = jnp.mean(xc * xc, axis=-1, keepdims=True)
    return xc * jax.lax.rsqrt(var + LN_EPS) * g + b


def _rmsnorm(x, g):
    return x * jax.lax.rsqrt(jnp.mean(x * x, axis=-1, keepdims=True) + RMS_EPS) * g


def _dot(a, b):
    return jnp.dot(a, b, preferred_element_type=jnp.float32)


def _dot_nt(a, b):
    return jax.lax.dot_general(a, b, (((1,), (1,)), ((), ())), preferred_element_type=jnp.float32)


def _proj_kernel(first, *refs):
    if first:
        (x_ref, posr_ref, eg_ref, eb_ref, w_lat_ref, w_mix_ref, qg_ref, kvg_ref, w_uqt_ref, w_uk_ref,
         w_uvt_ref, w_pool_ref, pscale_ref, convw_ref, invfc_ref,
         qt_ref, k_ref, vt_ref, gm_ref, pc_ref, h_ref, pool_ext, conv_ext) = refs
    else:
        (x_ref, posr_ref, w_lat_ref, w_mix_ref, qg_ref, kvg_ref, w_uqt_ref, w_uk_ref,
         w_uvt_ref, w_pool_ref, pscale_ref, convw_ref, invfc_ref,
         qt_ref, k_ref, vt_ref, gm_ref, pc_ref, pool_ext, conv_ext) = refs
    tm = PROJ_TM
    i = pl.program_id(1)

    @pl.when(i == 0)
    def _():
        pool_ext[0:POOL_HALO, :] = jnp.zeros((POOL_HALO, D_POOL), jnp.float32)
        conv_ext[0:CONV_HALO, :] = jnp.zeros((CONV_HALO, D_CONV), jnp.float32)

    @pl.when(i > 0)
    def _():
        pool_ext[0:POOL_HALO, :] = pool_ext[tm:tm + POOL_HALO, :]
        conv_ext[0:CONV_HALO, :] = conv_ext[tm:tm + CONV_HALO, :]

    x = x_ref[0]
    if first:
        x = _layernorm(x, eg_ref[...], eb_ref[...])
        h_ref[0] = x
    hb = x.astype(jnp.bfloat16)

    lat = _dot(hb, w_lat_ref[0])
    pin = _dot(hb, w_mix_ref[0, :, R_PIN:R_GPOOL])
    cv = _dot(hb, w_mix_ref[0, :, R_CH:R_END])

    qn = _rmsnorm(lat[:, C_QLAT:C_KVLAT], qg_ref[0]).astype(jnp.bfloat16)
    kvn = _rmsnorm(lat[:, C_KVLAT:C_KR], kvg_ref[0]).astype(jnp.bfloat16)
    kr = lat[:, C_KR:C_LAT_END]
    angt = invfc_ref[...] * posr_ref[0].astype(jnp.float32)
    cos_t, sin_t = jnp.cos(angt), jnp.sin(angt)
    cs = jnp.concatenate([cos_t, cos_t, sin_t, sin_t], axis=0).T
    lane = jax.lax.broadcasted_iota(jnp.int32, (tm, LANES), 1)
    t = kr * cs
    ones = jnp.where(lane < ROPE + BIAS_ROWS, 1.0, 0.0)
    kroped = jnp.where(lane < ROPE, t + pltpu.roll(t, ROPE, axis=1), ones).astype(jnp.bfloat16)

    kn = _dot(kvn, w_uk_ref[0])
    vt = _dot_nt(w_uvt_ref[0], kvn)
    qt = _dot_nt(w_uqt_ref[0], qn)
    gpool = _dot(hb, w_mix_ref[0, :, R_GPOOL:R_CH])

    for hh in range(N_HEADS):
        k_ref[0, hh] = jnp.concatenate(
            [kn[:, hh * NOPE:(hh + 1) * NOPE].astype(jnp.bfloat16), kroped], axis=1)
        vt_ref[0, hh, 0] = vt[hh * V_DIM:(hh + 1) * V_DIM, :].astype(jnp.bfloat16)

    u = cv[:, 2 * D_CONV:3 * D_CONV] * cv[:, 0:D_CONV]
    conv_ext[CONV_HALO:CONV_HALO + tm, :] = u
    yc = convw_ref[0, 2:3, :] * u
    yc = yc + convw_ref[0, 1:2, :] * conv_ext[CONV_HALO - 1:CONV_HALO - 1 + tm, :]
    yc = yc + convw_ref[0, 0:1, :] * conv_ext[CONV_HALO - 2:CONV_HALO - 2 + tm, :]
    y_conv = cv[:, D_CONV:2 * D_CONV] * yc * _silu(cv[:, 3 * D_CONV:4 * D_CONV])
    pc_ref[0, :, D_POOL:] = y_conv.astype(jnp.bfloat16)

    pool_ext[POOL_HALO:POOL_HALO + tm, :] = pin
    t1 = (i * tm + 1 + jax.lax.broadcasted_iota(jnp.int32, (tm, 1), 0)).astype(jnp.float32)
    ys = []
    for g, w in enumerate(POOL_WINDOWS):
        lo = g * POOL_GROUP
        acc = pin[:, lo:lo + POOL_GROUP]
        for s in range(1, w):
            acc = acc + pool_ext[POOL_HALO - s:POOL_HALO - s + tm, lo:lo + POOL_GROUP]
        pooled = acc / jnp.minimum(t1, float(w)) - pin[:, lo:lo + POOL_GROUP]
        ys.append(_dot(pooled.astype(jnp.bfloat16), w_pool_ref[0, g]))
    gm = _dot(hb, w_mix_ref[0, :, R_GMLA:R_PIN])
    y_pool = jnp.concatenate(ys, axis=1) * pscale_ref[0] * _silu(gpool)
    pc_ref[0, :, :D_POOL] = y_pool.astype(jnp.bfloat16)

    cos_t = jnp.concatenate([cos_t, cos_t], axis=0)
    sin_t = jnp.concatenate([sin_t, sin_t], axis=0)
    zeros_t = jnp.zeros((HEAD_W - NOPE - ROPE, tm), jnp.float32)
    for hh in range(N_HEADS):
        r0 = hh * HEAD_W
        roped = qt[r0 + NOPE:r0 + NOPE + ROPE, :] * cos_t + qt[r0 + NOPE + ROPE:r0 + HEAD_W, :] * sin_t
        qh = jnp.concatenate([qt[r0:r0 + NOPE, :], roped, zeros_t], axis=0)
        qt_ref[0, hh] = (qh * Q_PRESCALE).astype(jnp.bfloat16)

    gm_ref[0] = _silu(gm).astype(jnp.bfloat16)


def _const_spec(shape):
    nd = len(shape)
    return pl.BlockSpec(shape, lambda b, i: (0,) * nd, pipeline_mode=pl.Buffered(1))


def _layer_spec(arr, layer):
    nd = arr.ndim
    return pl.BlockSpec((1,) + arr.shape[1:], lambda b, i: (layer,) + (0,) * (nd - 1),
                        pipeline_mode=pl.Buffered(1))


def _proj_call(layer, x, posr, emb_g, emb_b, inv_freq_col, stacked):
    first = layer == 0
    B, S, D = x.shape
    tm = PROJ_TM
    grid = (B, S // tm)
    tok = lambda w: pl.BlockSpec((1, tm, w), lambda b, i: (b, i, 0))
    in_specs = [tok(D), pl.BlockSpec((1, 1, tm), lambda b, i: (b, 0, i))]
    args = [x, posr]
    if first:
        in_specs += [_const_spec((1, D)), _const_spec((1, D))]
        args += [emb_g, emb_b]
    in_specs += [_layer_spec(a, layer) for a in stacked] + [_const_spec(inv_freq_col.shape)]
    args += list(stacked) + [inv_freq_col]
    bf = jnp.bfloat16
    out_shape = [
        jax.ShapeDtypeStruct((B, N_HEADS, HEAD_W, S), bf),
        jax.ShapeDtypeStruct((B, N_HEADS, S, HEAD_W), bf),
        jax.ShapeDtypeStruct((B, N_HEADS, S // KT, V_DIM, KT), bf),
        jax.ShapeDtypeStruct((B, S, D_MLA), bf),
        jax.ShapeDtypeStruct((B, S, D_POOL + D_CONV), bf),
    ]
    out_specs = [
        pl.BlockSpec((1, N_HEADS, HEAD_W, tm), lambda b, i: (b, 0, 0, i)),
        pl.BlockSpec((1, N_HEADS, tm, HEAD_W), lambda b, i: (b, 0, i, 0)),
        pl.BlockSpec((1, N_HEADS, 1, V_DIM, KT), lambda b, i: (b, 0, i, 0, 0)),
        tok(D_MLA), tok(D_POOL + D_CONV)]
    if first:
        out_shape.append(jax.ShapeDtypeStruct((B, S, D), jnp.float32))
        out_specs.append(tok(D))
    return pl.pallas_call(
        functools.partial(_proj_kernel, first),
        grid=grid,
        in_specs=in_specs,
        out_specs=out_specs,
        out_shape=out_shape,
        scratch_shapes=[
            pltpu.VMEM((POOL_HALO + tm, D_POOL), jnp.float32),
            pltpu.VMEM((CONV_HALO + tm, D_CONV), jnp.float32),
        ],
        compiler_params=pltpu.CompilerParams(
            dimension_semantics=("arbitrary", "arbitrary"),
            vmem_limit_bytes=VMEM_LIMIT_BYTES),
        name="proj_first" if first else "proj",
    )(*args)


def _rows8_reduce(x, op):
    acc = x[0:SUBLANES]
    for r in range(SUBLANES, x.shape[0], SUBLANES):
        acc = op(acc, x[r:r + SUBLANES])
    return acc


def _attn_kernel(qt_ref, k_ref, vt_ref, gm_ref, o_ref,
                 qb_sc, s_sc, p_sc, l8_sc, acc_sc, m_sc, l_sc):
    tq, tk, G, ch = ATT_TQ, ATT_TK, ATT_G, ATT_CH
    sub, nch, nqb = tk // KT, tk // ch, tq // LANES
    i = pl.program_id(2)
    par = i % 2

    def vt_tile(g, n):
        return jnp.concatenate([vt_ref[0, g, n * sub + c] for c in range(sub)], axis=1)

    def rows(c):
        return slice(c * ch, (c + 1) * ch)

    def lanes(j):
        return slice(j * LANES, (j + 1) * LANES)

    d0 = pl.multiple_of(i * tk, tk)
    for g in range(G):
        qb_sc[g] = qt_ref[0, g]
        s_sc[g] = _dot(k_ref[0, g, pl.ds(d0, tk), :], qt_ref[0, g])

    def diag_block(g, c, j):
        s_b = s_sc[g, rows(c), lanes(j)]
        if c < 2 * j:
            return s_b
        key = c * ch + jax.lax.broadcasted_iota(jnp.int32, (ch, LANES), 0)
        qry = j * LANES + jax.lax.broadcasted_iota(jnp.int32, (ch, LANES), 1)
        return jnp.where(key <= qry, s_b, -jnp.inf)

    for g in range(G):
        for j in range(nqb):
            visible = range(2 * j + 2)
            m8 = _rows8_reduce(diag_block(g, 0, j), jnp.maximum)
            for c in visible[1:]:
                m8 = jnp.maximum(m8, _rows8_reduce(diag_block(g, c, j), jnp.maximum))
            m_d = jnp.max(m8, axis=0, keepdims=True)
            m_ref = m_d.astype(jnp.bfloat16).astype(jnp.float32)
            lsum = jnp.zeros((SUBLANES, LANES), jnp.float32)
            for c in range(nch):
                if c in visible:
                    p_b = jnp.exp2(diag_block(g, c, j) - m_ref)
                    lsum = lsum + _rows8_reduce(p_b, jnp.add)
                    p_sc[par, g, rows(c), lanes(j)] = p_b.astype(jnp.bfloat16)
                else:
                    p_sc[par, g, rows(c), lanes(j)] = jnp.zeros((ch, LANES), jnp.bfloat16)
            l8_sc[g, :, lanes(j)] = lsum
            bias = jnp.broadcast_to(m_ref * (-1.0 / BIAS_ROWS), (BIAS_ROWS, LANES))
            qb_sc[g, BIAS_ROW0:BIAS_ROW0 + BIAS_ROWS, lanes(j)] = bias.astype(jnp.bfloat16)
    acc_sc[...] = jnp.zeros(acc_sc.shape, jnp.float32)

    def exp_chunks(g, chunk_of, buf, lsum):
        for c in range(nch):
            p_c = jnp.exp2(chunk_of(c))
            lsum = lsum + _rows8_reduce(p_c, jnp.add)
            p_sc[buf, g, rows(c), :] = p_c.astype(jnp.bfloat16)
        return lsum

    def qk_exp(g, buf, n):
        start = pl.multiple_of(n * tk, tk)
        s = _dot(k_ref[0, g, pl.ds(start, tk), :], qb_sc[g])
        l8_sc[g] = exp_chunks(g, lambda c: s[rows(c)], buf, l8_sc[g])

    def pv(g, buf, n):
        acc_sc[g] = acc_sc[g] + _dot(vt_tile(g, n), p_sc[buf, g])

    @pl.when(par == 1)
    def _():
        for g in range(G):
            qk_exp(g, 0, 0)
            pv(g, 1, i)

    def body(t, carry):
        n = par + 2 * t
        before = jnp.where(jnp.logical_and(t == 0, par == 0), i, n - 1)
        for g in range(G):
            qk_exp(g, 1, n)
            pv(g, 0, before)
        for g in range(G):
            qk_exp(g, 0, n + 1)
            pv(g, 1, n)
        return carry

    jax.lax.fori_loop(0, i // 2, body, 0)

    for g in range(G):
        pv(g, 0, jnp.maximum(i - 1, 0))

    l_min, l_max = None, None
    for g in range(G):
        l = jnp.sum(l8_sc[g], axis=0, keepdims=True)
        l_sc[g] = l
        l_min = jnp.min(l) if l_min is None else jnp.minimum(l_min, jnp.min(l))
        l_max = jnp.max(l) if l_max is None else jnp.maximum(l_max, jnp.max(l))
    in_range = jnp.logical_and(l_min > SHIFT_L_MIN, l_max < SHIFT_L_MAX)

    @pl.when(jnp.logical_not(in_range))
    def _():
        m_sc[...] = jnp.full(m_sc.shape, M_INIT, jnp.float32)
        l_sc[...] = jnp.zeros(l_sc.shape, jnp.float32)
        acc_sc[...] = jnp.zeros(acc_sc.shape, jnp.float32)

        def causal(c):
            key = c * ch + jax.lax.broadcasted_iota(jnp.int32, (ch, tq), 0)
            return key <= jax.lax.broadcasted_iota(jnp.int32, (ch, tq), 1)

        def chunk(g, c, masked):
            s_c = s_sc[g, rows(c), :]
            return jnp.where(causal(c), s_c, -jnp.inf) if masked else s_c

        def tile(n, masked):
            start = pl.multiple_of(n * tk, tk)
            for g in range(G):
                s_sc[g] = _dot(k_ref[0, g, pl.ds(start, tk), :], qt_ref[0, g])
                m8 = _rows8_reduce(chunk(g, 0, masked), jnp.maximum)
                for c in range(1, nch):
                    m8 = jnp.maximum(m8, _rows8_reduce(chunk(g, c, masked), jnp.maximum))
                m_prev = m_sc[g]
                m_new = jnp.maximum(m_prev, jnp.max(m8, axis=0, keepdims=True))
                alpha = jnp.exp2(m_prev - m_new)
                lsum = exp_chunks(g, lambda c: chunk(g, c, masked) - m_new, 0,
                                  jnp.zeros((SUBLANES, tq), jnp.float32))
                l_sc[g] = alpha * l_sc[g] + jnp.sum(lsum, axis=0, keepdims=True)
                acc_sc[g] = alpha * acc_sc[g] + _dot(vt_tile(g, n), p_sc[0, g])
                m_sc[g] = m_new

        def full(n, carry):
            tile(n, False)
            return carry

        jax.lax.fori_loop(0, i, full, 0)
        tile(i, True)

    for g in range(G):
        out_t = acc_sc[g] / l_sc[g]
        out = out_t.T * gm_ref[0, :, g * V_DIM:(g + 1) * V_DIM].astype(jnp.float32)
        o_ref[0, :, g * V_DIM:(g + 1) * V_DIM] = out.astype(o_ref.dtype)


def _attn_call(qt, k, vt, gm):
    B, H, S, _ = k.shape
    tq, G = ATT_TQ, ATT_G
    assert ATT_TQ == ATT_TK and ATT_TK % KT == 0
    grid = (B, H // G, S // tq)
    return pl.pallas_call(
        _attn_kernel,
        grid=grid,
        in_specs=[
            pl.BlockSpec((1, G, HEAD_W, tq), lambda b, h, i: (b, h, 0, i)),
            pl.BlockSpec((1, G, S, HEAD_W), lambda b, h, i: (b, h, 0, 0)),
            pl.BlockSpec((1, G, S // KT, V_DIM, KT), lambda b, h, i: (b, h, 0, 0, 0)),
            pl.BlockSpec((1, tq, G * V_DIM), lambda b, h, i: (b, i, h)),
        ],
        out_specs=pl.BlockSpec((1, tq, G * V_DIM), lambda b, h, i: (b, i, h)),
        out_shape=jax.ShapeDtypeStruct((B, S, D_MLA), jnp.bfloat16),
        scratch_shapes=[
            pltpu.VMEM((G, HEAD_W, tq), jnp.bfloat16),
            pltpu.VMEM((G, ATT_TK, tq), jnp.float32),
            pltpu.VMEM((2, G, ATT_TK, tq), jnp.bfloat16),
            pltpu.VMEM((G, SUBLANES, tq), jnp.float32),
            pltpu.VMEM((G, V_DIM, tq), jnp.float32),
            pltpu.VMEM((G, 1, tq), jnp.float32),
            pltpu.VMEM((G, 1, tq), jnp.float32),
        ],
        compiler_params=pltpu.CompilerParams(
            dimension_semantics=("arbitrary", "arbitrary", "arbitrary"),
            vmem_limit_bytes=VMEM_LIMIT_BYTES),
        name="attn",
    )(qt, k, vt, gm)


def _out_kernel(o_ref, pc_ref, h_ref, w_ref, b_ref, g_ref, beta_ref, y_ref):
    for r in range(0, OUT_TM, OUT_CH):
        rows = slice(r, r + OUT_CH)
        mix = jnp.concatenate([o_ref[0, rows, :], pc_ref[0, rows, :]], axis=1)
        y = _dot(mix, w_ref[0]) + b_ref[0] + DEEPNORM_ALPHA * h_ref[0, rows, :]
        y_ref[0, rows, :] = _layernorm(y, g_ref[0], beta_ref[0])


def _out_call(layer, o, pc, h, w_out, b_out, ln_g, ln_b):
    B, S, D = h.shape
    tm = OUT_TM
    tok = lambda w: pl.BlockSpec((1, tm, w), lambda b, i: (b, i, 0))
    return pl.pallas_call(
        _out_kernel,
        grid=(B, S // tm),
        in_specs=[tok(D_MLA), tok(D_POOL + D_CONV), tok(D),
                  _layer_spec(w_out, layer), _layer_spec(b_out, layer), _layer_spec(ln_g, layer),
                  _layer_spec(ln_b, layer)],
        out_specs=tok(D),
        out_shape=jax.ShapeDtypeStruct((B, S, D), jnp.float32),
        compiler_params=pltpu.CompilerParams(
            dimension_semantics=("arbitrary", "arbitrary"),
            vmem_limit_bytes=VMEM_LIMIT_BYTES),
        name="out",
    )(o, pc, h, w_out, b_out, ln_g, ln_b)


def _rotate_half_cols(w):
    return jnp.concatenate([-w[..., HALF:], w[..., :HALF]], axis=-1)


def _wprep_lat_kernel(wt_ref, o_ref):
    x = wt_ref[0]
    kr = C_KR
    rot = jnp.concatenate([-x[kr + HALF:kr + ROPE], x[kr:kr + HALF]], axis=0)
    o_ref[0] = jnp.concatenate([x, rot], axis=0).T.astype(jnp.bfloat16)


def _wprep_mix_kernel(wt_ref, o_ref):
    o_ref[0] = wt_ref[0].T.astype(jnp.bfloat16)


def _prep_w_in(w):
    L, D, C = w.shape
    wt = jnp.swapaxes(w, 1, 2)
    n_lat = C_KR + ROPE
    params = pltpu.CompilerParams(vmem_limit_bytes=VMEM_LIMIT_BYTES)
    w_lat = pl.pallas_call(
        _wprep_lat_kernel,
        grid=(L,),
        in_specs=[pl.BlockSpec((pl.Element(1), pl.Element(n_lat), pl.Element(D)), lambda l: (l, 0, 0))],
        out_specs=pl.BlockSpec((1, D, C_LAT_END), lambda l: (l, 0, 0)),
        out_shape=jax.ShapeDtypeStruct((L, D, C_LAT_END), jnp.bfloat16),
        compiler_params=params,
        name="wprep_lat",
    )(wt)
    w_mix = pl.pallas_call(
        _wprep_mix_kernel,
        grid=(L, R_END // WPREP_COLS),
        in_specs=[pl.BlockSpec((pl.Element(1), pl.Element(WPREP_COLS), pl.Element(D)),
                               lambda l, j: (l, pl.multiple_of(n_lat + WPREP_COLS * j, SUBLANES), 0))],
        out_specs=pl.BlockSpec((1, D, WPREP_COLS), lambda l, j: (l, 0, j)),
        out_shape=jax.ShapeDtypeStruct((L, D, R_END), jnp.bfloat16),
        compiler_params=params,
        name="wprep_mix",
    )(wt)
    return w_lat, w_mix


def _prep_w_uq_t(w):
    L = w.shape[0]
    w = w.reshape(L, Q_LORA, N_HEADS, NOPE + ROPE)
    w = jnp.concatenate([w, _rotate_half_cols(w[..., NOPE:])], axis=-1)
    return jnp.swapaxes(w.reshape(L, Q_LORA, N_HEADS * HEAD_W), 1, 2).astype(jnp.bfloat16)


def _prep_w_ukv(w):
    L = w.shape[0]
    w = w.reshape(L, KV_LORA, N_HEADS, NOPE + V_DIM)
    wk = w[..., :NOPE].reshape(L, KV_LORA, N_HEADS * NOPE)
    wv = w[..., NOPE:].reshape(L, KV_LORA, N_HEADS * V_DIM)
    return wk.astype(jnp.bfloat16), jnp.swapaxes(wv, 1, 2).astype(jnp.bfloat16)


def kernel(x, positions, emb_ln_g, emb_ln_b, w_in, q_norm_g, kv_norm_g, w_uq, w_ukv, w_pool,
           pool_scale, conv_w, w_out, b_out, ln_g, ln_b):
    B, S, D = x.shape
    bf = jnp.bfloat16
    posr = positions.reshape(B, 1, S)
    inv_freq = ROPE_THETA ** (-jnp.arange(HALF, dtype=jnp.float32) / HALF)
    invfc = inv_freq.reshape(HALF, 1)
    row = lambda a: a.reshape(1, -1)

    w_lat, w_mix = _prep_w_in(w_in)
    w_uk, w_uvt = _prep_w_ukv(w_ukv)
    row3 = lambda a: a.reshape(a.shape[0], 1, -1)
    stacked = (w_lat, w_mix, row3(q_norm_g), row3(kv_norm_g), _prep_w_uq_t(w_uq), w_uk, w_uvt,
               w_pool.astype(bf), row3(pool_scale), conv_w)
    w_out_b = w_out.astype(bf)

    h = x
    for l in range(DEPTH):
        outs = _proj_call(l, h, posr, row(emb_ln_g), row(emb_ln_b), invfc, stacked)
        if l == 0:
            qt, k, vt, gm, pc, h = outs
        else:
            qt, k, vt, gm, pc = outs
        o = _attn_call(qt, k, vt, gm)
        h = _out_call(l, o, pc, h, w_out_b, row3(b_out), row3(ln_g), row3(ln_b))
    return h
```

```python
import functools
import math

import jax
import jax.numpy as jnp
import numpy as np
from jax.experimental import pallas as pl
from jax.experimental.pallas import tpu as pltpu

D_MODEL = 2048
DEPTH = 2
N_HEADS = 8
NOPE = 128
ROPE = 64
V_DIM = 128
Q_LORA = 512
KV_LORA = 256
D_MLA = N_HEADS * V_DIM
ROPE_THETA = 10000.0
POOL_WINDOWS = (2, 4, 8, 16)
POOL_GROUP = 128
D_POOL = 512
D_CONV = 512
CONV_WIDTH = 3
LN_EPS = 1e-5
RMS_EPS = 1e-6
DEEPNORM_ALPHA = (2 * DEPTH) ** 0.25

LANES = 128
SUBLANES = 8
VMEM_LIMIT_BYTES = 56 * 1024 * 1024

PROJ_TM = 256
ATT_TQ = 512
ATT_TK = 512
ATT_G = 2
ATT_CH = 64
WPREP_COLS = 512
OUT_TM = 512
OUT_CH = 128
POOL_HALO = 16
CONV_HALO = 8

KT = PROJ_TM
HEAD_W = 2 * LANES
HALF = ROPE // 2

C_QLAT = 0
C_KVLAT = C_QLAT + Q_LORA
C_KR = C_KVLAT + KV_LORA
C_LAT_END = C_KR + LANES
R_GMLA = 0
R_PIN = R_GMLA + D_MLA
R_GPOOL = R_PIN + D_POOL
R_CH = R_GPOOL + D_POOL
R_END = R_CH + 4 * D_CONV

Q_PRESCALE = (NOPE + ROPE) ** -0.5 * math.log2(math.e)
M_INIT = -1e30
BIAS_ROW0 = NOPE + ROPE
BIAS_ROWS = 16
SHIFT_L_MIN = 2.0 ** -100
SHIFT_L_MAX = 2.0 ** 100


def _silu(g):
    return g * (1.0 / (1.0 + jnp.exp(-g)))


def _layernorm(x, g, b):
    mu = jnp.mean(x, axis=-1, keepdims=True)
    xc = x - mu
    var = jnp.mean(xc * xc, axis=-1, keepdims=True)
    return xc * jax.lax.rsqrt(var + LN_EPS) * g + b


def _rmsnorm(x, g):
    return x * jax.lax.rsqrt(jnp.mean(x * x, axis=-1, keepdims=True) + RMS_EPS) * g


def _dot(a, b):
    return jnp.dot(a, b, preferred_element_type=jnp.float32)


def _dot_nt(a, b):
    return jax.lax.dot_general(a, b, (((1,), (1,)), ((), ())), preferred_element_type=jnp.float32)


def _proj_kernel(first, *refs):
    if first:
        (x_ref, posr_ref, eg_ref, eb_ref, w_lat_ref, w_mix_ref, qg_ref, kvg_ref, w_uqt_ref, w_uk_ref,
         w_uvt_ref, w_pool_ref, pscale_ref, convw_ref, invfc_ref,
         qt_ref, k_ref, vt_ref, gm_ref, pc_ref, h_ref, pool_ext, conv_ext) = refs
    else:
        (x_ref, posr_ref, w_lat_ref, w_mix_ref, qg_ref, kvg_ref, w_uqt_ref, w_uk_ref,
         w_uvt_ref, w_pool_ref, pscale_ref, convw_ref, invfc_ref,
         qt_ref, k_ref, vt_ref, gm_ref, pc_ref, pool_ext, conv_ext) = refs
    tm = PROJ_TM
    i = pl.program_id(1)

    @pl.when(i == 0)
    def _():
        pool_ext[0:POOL_HALO, :] = jnp.zeros((POOL_HALO, D_POOL), jnp.float32)
        conv_ext[0:CONV_HALO, :] = jnp.zeros((CONV_HALO, D_CONV), jnp.float32)

    @pl.when(i > 0)
    def _():
        pool_ext[0:POOL_HALO, :] = pool_ext[tm:tm + POOL_HALO, :]
        conv_ext[0:CONV_HALO, :] = conv_ext[tm:tm + CONV_HALO, :]

    x = x_ref[0]
    if first:
        x = _layernorm(x, eg_ref[...], eb_ref[...])
        h_ref[0] = x
    hb = x.astype(jnp.bfloat16)

    lat = _dot(hb, w_lat_ref[0])
    pin = _dot(hb, w_mix_ref[0, :, R_PIN:R_GPOOL])
    cv = _dot(hb, w_mix_ref[0, :, R_CH:R_END])

    qn = _rmsnorm(lat[:, C_QLAT:C_KVLAT], qg_ref[0]).astype(jnp.bfloat16)
    kvn = _rmsnorm(lat[:, C_KVLAT:C_KR], kvg_ref[0]).astype(jnp.bfloat16)
    kr = lat[:, C_KR:C_LAT_END]
    angt = invfc_ref[...] * posr_ref[0].astype(jnp.float32)
    cos_t, sin_t = jnp.cos(angt), jnp.sin(angt)
    cs = jnp.concatenate([cos_t, cos_t, sin_t, sin_t], axis=0).T
    lane = jax.lax.broadcasted_iota(jnp.int32, (tm, LANES), 1)
    t = kr * cs
    ones = jnp.where(lane < ROPE + BIAS_ROWS, 1.0, 0.0)
    kroped = jnp.where(lane < ROPE, t + pltpu.roll(t, ROPE, axis=1), ones).astype(jnp.bfloat16)

    kn = _dot(kvn, w_uk_ref[0])
    vt = _dot_nt(w_uvt_ref[0], kvn)
    qt = _dot_nt(w_uqt_ref[0], qn)
    gpool = _dot(hb, w_mix_ref[0, :, R_GPOOL:R_CH])

    for hh in range(N_HEADS):
        k_ref[0, hh] = jnp.concatenate(
            [kn[:, hh * NOPE:(hh + 1) * NOPE].astype(jnp.bfloat16), kroped], axis=1)
        vt_ref[0, hh, 0] = vt[hh * V_DIM:(hh + 1) * V_DIM, :].astype(jnp.bfloat16)

    u = cv[:, 2 * D_CONV:3 * D_CONV] * cv[:, 0:D_CONV]
    conv_ext[CONV_HALO:CONV_HALO + tm, :] = u
    yc = convw_ref[0, 2:3, :] * u
    yc = yc + convw_ref[0, 1:2, :] * conv_ext[CONV_HALO - 1:CONV_HALO - 1 + tm, :]
    yc = yc + convw_ref[0, 0:1, :] * conv_ext[CONV_HALO - 2:CONV_HALO - 2 + tm, :]
    y_conv = cv[:, D_CONV:2 * D_CONV] * yc * _silu(cv[:, 3 * D_CONV:4 * D_CONV])
    pc_ref[0, :, D_POOL:] = y_conv.astype(jnp.bfloat16)

    pool_ext[POOL_HALO:POOL_HALO + tm, :] = pin
    t1 = (i * tm + 1 + jax.lax.broadcasted_iota(jnp.int32, (tm, 1), 0)).astype(jnp.float32)
    ys = []
    for g, w in enumerate(POOL_WINDOWS):
        lo = g * POOL_GROUP
        acc = pin[:, lo:lo + POOL_GROUP]
        for s in range(1, w):
            acc = acc + pool_ext[POOL_HALO - s:POOL_HALO - s + tm, lo:lo + POOL_GROUP]
        pooled = acc / jnp.minimum(t1, float(w)) - pin[:, lo:lo + POOL_GROUP]
        ys.append(_dot(pooled.astype(jnp.bfloat16), w_pool_ref[0, g]))
    gm = _dot(hb, w_mix_ref[0, :, R_GMLA:R_PIN])
    y_pool = jnp.concatenate(ys, axis=1) * pscale_ref[0] * _silu(gpool)
    pc_ref[0, :, :D_POOL] = y_pool.astype(jnp.bfloat16)

    cos_t = jnp.concatenate([cos_t, cos_t], axis=0)
    sin_t = jnp.concatenate([sin_t, sin_t], axis=0)
    zeros_t = jnp.zeros((HEAD_W - NOPE - ROPE, tm), jnp.float32)
    for hh in range(N_HEADS):
        r0 = hh * HEAD_W
        roped = qt[r0 + NOPE:r0 + NOPE + ROPE, :] * cos_t + qt[r0 + NOPE + ROPE:r0 + HEAD_W, :] * sin_t
        qh = jnp.concatenate([qt[r0:r0 + NOPE, :], roped, zeros_t], axis=0)
        qt_ref[0, hh, 0] = (qh * Q_PRESCALE).astype(jnp.bfloat16)

    gm_ref[0] = _silu(gm).astype(jnp.bfloat16)


def _const_spec(shape):
    nd = len(shape)
    return pl.BlockSpec(shape, lambda b, i: (0,) * nd, pipeline_mode=pl.Buffered(1))


def _layer_spec(arr, layer):
    nd = arr.ndim
    return pl.BlockSpec((1,) + arr.shape[1:], lambda b, i: (layer,) + (0,) * (nd - 1),
                        pipeline_mode=pl.Buffered(1))


def _proj_call(layer, x, posr, emb_g, emb_b, inv_freq_col, stacked):
    first = layer == 0
    B, S, D = x.shape
    tm = PROJ_TM
    grid = (B, S // tm)
    qsub = ATT_TQ // tm
    tok = lambda w: pl.BlockSpec((1, tm, w), lambda b, i: (b, i, 0))
    in_specs = [tok(D), pl.BlockSpec((1, 1, tm), lambda b, i: (b, 0, i))]
    args = [x, posr]
    if first:
        in_specs += [_const_spec((1, D)), _const_spec((1, D))]
        args += [emb_g, emb_b]
    in_specs += [_layer_spec(a, layer) for a in stacked] + [_const_spec(inv_freq_col.shape)]
    args += list(stacked) + [inv_freq_col]
    bf = jnp.bfloat16
    out_shape = [
        jax.ShapeDtypeStruct((B, N_HEADS, S // ATT_TQ, HEAD_W, ATT_TQ), bf),
        jax.ShapeDtypeStruct((B, N_HEADS, S, HEAD_W), bf),
        jax.ShapeDtypeStruct((B, N_HEADS, S // KT, V_DIM, KT), bf),
        jax.ShapeDtypeStruct((B, S, D_MLA), bf),
        jax.ShapeDtypeStruct((B, S, D_POOL + D_CONV), bf),
    ]
    out_specs = [
        pl.BlockSpec((1, N_HEADS, 1, HEAD_W, tm), lambda b, i: (b, 0, i // qsub, 0, i % qsub)),
        pl.BlockSpec((1, N_HEADS, tm, HEAD_W), lambda b, i: (b, 0, i, 0)),
        pl.BlockSpec((1, N_HEADS, 1, V_DIM, KT), lambda b, i: (b, 0, i, 0, 0)),
        tok(D_MLA), tok(D_POOL + D_CONV)]
    if first:
        out_shape.append(jax.ShapeDtypeStruct((B, S, D), jnp.float32))
        out_specs.append(tok(D))
    return pl.pallas_call(
        functools.partial(_proj_kernel, first),
        grid=grid,
        in_specs=in_specs,
        out_specs=out_specs,
        out_shape=out_shape,
        scratch_shapes=[
            pltpu.VMEM((POOL_HALO + tm, D_POOL), jnp.float32),
            pltpu.VMEM((CONV_HALO + tm, D_CONV), jnp.float32),
        ],
        compiler_params=pltpu.CompilerParams(
            dimension_semantics=("arbitrary", "arbitrary"),
            vmem_limit_bytes=VMEM_LIMIT_BYTES),
        name="proj_first" if first else "proj",
    )(*args)


def _rows8_reduce(x, op):
    acc = x[0:SUBLANES]
    for r in range(SUBLANES, x.shape[0], SUBLANES):
        acc = op(acc, x[r:r + SUBLANES])
    return acc


def _attn_kernel(qt_ref, k_ref, vt_ref, gm_ref, o_ref,
                 qb_sc, s_sc, p_sc, l8_sc, acc_sc, m_sc, l_sc):
    tq, tk, G, ch = ATT_TQ, ATT_TK, ATT_G, ATT_CH
    sub, nch, nqb = tk // KT, tk // ch, tq // LANES
    nq = qt_ref.shape[2]

    def vt_tile(g, n):
        return jnp.concatenate([vt_ref[0, g, n * sub + c] for c in range(sub)], axis=1)

    def k_tile(g, n):
        return k_ref[0, g, pl.ds(pl.multiple_of(n * tk, tk), tk), :]

    def rows(c):
        return slice(c * ch, (c + 1) * ch)

    def lanes(j):
        return slice(j * LANES, (j + 1) * LANES)

    def exp_chunks(g, chunk_of, buf, lsum):
        for c in range(nch):
            p_c = jnp.exp2(chunk_of(c))
            lsum = lsum + _rows8_reduce(p_c, jnp.add)
            p_sc[buf, g, rows(c), :] = p_c.astype(jnp.bfloat16)
        return lsum

    def qk_exp(g, buf, n):
        s = _dot(k_tile(g, n), qb_sc[g])
        l8_sc[g] = exp_chunks(g, lambda c: s[rows(c)], buf, l8_sc[g])

    def pv(g, buf, n):
        acc_sc[g] = acc_sc[g] + _dot(vt_tile(g, n), p_sc[buf, g])

    def finish(i):
        q0 = pl.multiple_of(i * tq, tq)
        last = jnp.maximum(i - 1, 0)
        in_buf1 = jnp.logical_or(i == 0, last % 2 == 1)
        for g in range(G):
            p_last = jnp.where(in_buf1, p_sc[1, g], p_sc[0, g])
            acc_sc[g] = acc_sc[g] + _dot(vt_tile(g, last), p_last)
        for g in range(G):
            out_t = acc_sc[g] / l_sc[g]
            gate = gm_ref[0, pl.ds(q0, tq), g * V_DIM:(g + 1) * V_DIM].astype(jnp.float32)
            o_ref[0, pl.ds(q0, tq), g * V_DIM:(g + 1) * V_DIM] = (out_t.T * gate).astype(o_ref.dtype)

    def diagonal(i):
        for g in range(G):
            s_sc[g] = _dot(k_tile(g, i), qt_ref[0, g, i])
            qb_sc[g] = qt_ref[0, g, i]

        def block(g, c, j):
            s_b = s_sc[g, rows(c), lanes(j)]
            if c < 2 * j:
                return s_b
            key = c * ch + jax.lax.broadcasted_iota(jnp.int32, (ch, LANES), 0)
            qry = j * LANES + jax.lax.broadcasted_iota(jnp.int32, (ch, LANES), 1)
            return jnp.where(key <= qry, s_b, -jnp.inf)

        for g in range(G):
            for j in range(nqb):
                visible = range(2 * j + 2)
                m8 = _rows8_reduce(block(g, 0, j), jnp.maximum)
                for c in visible[1:]:
                    m8 = jnp.maximum(m8, _rows8_reduce(block(g, c, j), jnp.maximum))
                m_d = jnp.max(m8, axis=0, keepdims=True)
                m_ref = m_d.astype(jnp.bfloat16).astype(jnp.float32)
                lsum = jnp.zeros((SUBLANES, LANES), jnp.float32)
                for c in range(nch):
                    if c in visible:
                        p_b = jnp.exp2(block(g, c, j) - m_ref)
                        lsum = lsum + _rows8_reduce(p_b, jnp.add)
                        p_sc[1, g, rows(c), lanes(j)] = p_b.astype(jnp.bfloat16)
                    else:
                        p_sc[1, g, rows(c), lanes(j)] = jnp.zeros((ch, LANES), jnp.bfloat16)
                l8_sc[g, :, lanes(j)] = lsum
                bias = jnp.broadcast_to(m_ref * (-1.0 / BIAS_ROWS), (BIAS_ROWS, LANES))
                qb_sc[g, BIAS_ROW0:BIAS_ROW0 + BIAS_ROWS, lanes(j)] = bias.astype(jnp.bfloat16)
        acc_sc[...] = jnp.zeros(acc_sc.shape, jnp.float32)

    def slow_path(i):
        m_sc[...] = jnp.full(m_sc.shape, M_INIT, jnp.float32)
        l_sc[...] = jnp.zeros(l_sc.shape, jnp.float32)
        acc_sc[...] = jnp.zeros(acc_sc.shape, jnp.float32)

        def chunk(g, c, masked):
            s_c = s_sc[g, rows(c), :]
            if not masked:
                return s_c
            key = c * ch + jax.lax.broadcasted_iota(jnp.int32, (ch, tq), 0)
            return jnp.where(key <= jax.lax.broadcasted_iota(jnp.int32, (ch, tq), 1), s_c, -jnp.inf)

        def tile(n, masked):
            for g in range(G):
                s_sc[g] = _dot(k_tile(g, n), qt_ref[0, g, i])
                m8 = _rows8_reduce(chunk(g, 0, masked), jnp.maximum)
                for c in range(1, nch):
                    m8 = jnp.maximum(m8, _rows8_reduce(chunk(g, c, masked), jnp.maximum))
                m_prev = m_sc[g]
                m_new = jnp.maximum(m_prev, jnp.max(m8, axis=0, keepdims=True))
                alpha = jnp.exp2(m_prev - m_new)
                lsum = exp_chunks(g, lambda c: chunk(g, c, masked) - m_new, 0,
                                  jnp.zeros((SUBLANES, tq), jnp.float32))
                l_sc[g] = alpha * l_sc[g] + jnp.sum(lsum, axis=0, keepdims=True)
                acc_sc[g] = alpha * acc_sc[g] + _dot(vt_tile(g, n), p_sc[0, g])
                m_sc[g] = m_new

        def full(n, carry):
            tile(n, False)
            return carry

        jax.lax.fori_loop(0, i, full, 0)
        tile(i, True)
        p_sc[...] = jnp.zeros(p_sc.shape, jnp.bfloat16)

    def query_tile(i, carry):
        finish(jnp.maximum(i - 1, 0))
        diagonal(i)

        def pair(t, c):
            n = 2 * t
            for g in range(G):
                qk_exp(g, 0, n)
                pv(g, 1, jnp.where(t == 0, i, n - 1))
            for g in range(G):
                qk_exp(g, 1, n + 1)
                pv(g, 0, n)
            return c

        jax.lax.fori_loop(0, i // 2, pair, 0)

        @pl.when(i % 2 == 1)
        def _():
            for g in range(G):
                qk_exp(g, 0, i - 1)
                pv(g, 1, jnp.where(i == 1, i, i - 2))

        l_min, l_max = None, None
        for g in range(G):
            l = jnp.sum(l8_sc[g], axis=0, keepdims=True)
            l_sc[g] = l
            l_min = jnp.min(l) if l_min is None else jnp.minimum(l_min, jnp.min(l))
            l_max = jnp.max(l) if l_max is None else jnp.maximum(l_max, jnp.max(l))
        in_range = jnp.logical_and(l_min > SHIFT_L_MIN, l_max < SHIFT_L_MAX)

        @pl.when(jnp.logical_not(in_range))
        def _():
            slow_path(i)
        return carry

    acc_sc[...] = jnp.zeros(acc_sc.shape, jnp.float32)
    l_sc[...] = jnp.ones(l_sc.shape, jnp.float32)
    p_sc[1] = jnp.zeros(p_sc.shape[1:], jnp.bfloat16)
    jax.lax.fori_loop(0, nq, query_tile, 0)
    finish(nq - 1)


def _attn_call(qt, k, vt, gm):
    B, H, S, _ = k.shape
    tq, G = ATT_TQ, ATT_G
    assert ATT_TQ == ATT_TK and ATT_TK % KT == 0
    seq = lambda a: pl.BlockSpec((1, G) + a.shape[2:], lambda b, h: (b, h) + (0,) * (a.ndim - 2))
    tok = pl.BlockSpec((1, S, G * V_DIM), lambda b, h: (b, 0, h))
    return pl.pallas_call(
        _attn_kernel,
        grid=(B, H // G),
        in_specs=[seq(qt), seq(k), seq(vt), tok],
        out_specs=tok,
        out_shape=jax.ShapeDtypeStruct((B, S, D_MLA), jnp.bfloat16),
        scratch_shapes=[
            pltpu.VMEM((G, HEAD_W, tq), jnp.bfloat16),
            pltpu.VMEM((G, ATT_TK, tq), jnp.float32),
            pltpu.VMEM((2, G, ATT_TK, tq), jnp.bfloat16),
            pltpu.VMEM((G, SUBLANES, tq), jnp.float32),
            pltpu.VMEM((G, V_DIM, tq), jnp.float32),
            pltpu.VMEM((G, 1, tq), jnp.float32),
            pltpu.VMEM((G, 1, tq), jnp.float32),
        ],
        compiler_params=pltpu.CompilerParams(
            dimension_semantics=("arbitrary", "arbitrary"),
            vmem_limit_bytes=VMEM_LIMIT_BYTES),
        name="attn",
    )(qt, k, vt, gm)


def _out_kernel(o_ref, pc_ref, h_ref, w_ref, b_ref, g_ref, beta_ref, y_ref):
    for r in range(0, OUT_TM, OUT_CH):
        rows = slice(r, r + OUT_CH)
        mix = jnp.concatenate([o_ref[0, rows, :], pc_ref[0, rows, :]], axis=1)
        y = _dot(mix, w_ref[0]) + b_ref[0] + DEEPNORM_ALPHA * h_ref[0, rows, :]
        y_ref[0, rows, :] = _layernorm(y, g_ref[0], beta_ref[0])


def _out_call(layer, o, pc, h, w_out, b_out, ln_g, ln_b):
    B, S, D = h.shape
    tm = OUT_TM
    tok = lambda w: pl.BlockSpec((1, tm, w), lambda b, i: (b, i, 0))
    return pl.pallas_call(
        _out_kernel,
        grid=(B, S // tm),
        in_specs=[tok(D_MLA), tok(D_POOL + D_CONV), tok(D),
                  _layer_spec(w_out, layer), _layer_spec(b_out, layer), _layer_spec(ln_g, layer),
                  _layer_spec(ln_b, layer)],
        out_specs=tok(D),
        out_shape=jax.ShapeDtypeStruct((B, S, D), jnp.float32),
        compiler_params=pltpu.CompilerParams(
            dimension_semantics=("arbitrary", "arbitrary"),
            vmem_limit_bytes=VMEM_LIMIT_BYTES),
        name="out",
    )(o, pc, h, w_out, b_out, ln_g, ln_b)


def _rotate_half_cols(w):
    return jnp.concatenate([-w[..., HALF:], w[..., :HALF]], axis=-1)


def _wprep_lat_kernel(wt_ref, o_ref):
    x = wt_ref[0]
    kr = C_KR
    rot = jnp.concatenate([-x[kr + HALF:kr + ROPE], x[kr:kr + HALF]], axis=0)
    o_ref[0] = jnp.concatenate([x, rot], axis=0).T.astype(jnp.bfloat16)


def _wprep_mix_kernel(wt_ref, o_ref):
    o_ref[0] = wt_ref[0].T.astype(jnp.bfloat16)


def _prep_w_in(w):
    L, D, C = w.shape
    wt = jnp.swapaxes(w, 1, 2)
    n_lat = C_KR + ROPE
    params = pltpu.CompilerParams(vmem_limit_bytes=VMEM_LIMIT_BYTES)
    w_lat = pl.pallas_call(
        _wprep_lat_kernel,
        grid=(L,),
        in_specs=[pl.BlockSpec((pl.Element(1), pl.Element(n_lat), pl.Element(D)), lambda l: (l, 0, 0))],
        out_specs=pl.BlockSpec((1, D, C_LAT_END), lambda l: (l, 0, 0)),
        out_shape=jax.ShapeDtypeStruct((L, D, C_LAT_END), jnp.bfloat16),
        compiler_params=params,
        name="wprep_lat",
    )(wt)
    w_mix = pl.pallas_call(
        _wprep_mix_kernel,
        grid=(L, R_END // WPREP_COLS),
        in_specs=[pl.BlockSpec((pl.Element(1), pl.Element(WPREP_COLS), pl.Element(D)),
                               lambda l, j: (l, pl.multiple_of(n_lat + WPREP_COLS * j, SUBLANES), 0))],
        out_specs=pl.BlockSpec((1, D, WPREP_COLS), lambda l, j: (l, 0, j)),
        out_shape=jax.ShapeDtypeStruct((L, D, R_END), jnp.bfloat16),
        compiler_params=params,
        name="wprep_mix",
    )(wt)
    return w_lat, w_mix


def _prep_w_uq_t(w):
    L = w.shape[0]
    w = w.reshape(L, Q_LORA, N_HEADS, NOPE + ROPE)
    w = jnp.concatenate([w, _rotate_half_cols(w[..., NOPE:])], axis=-1)
    return jnp.swapaxes(w.reshape(L, Q_LORA, N_HEADS * HEAD_W), 1, 2).astype(jnp.bfloat16)


def _prep_w_ukv(w):
    L = w.shape[0]
    w = w.reshape(L, KV_LORA, N_HEADS, NOPE + V_DIM)
    wk = w[..., :NOPE].reshape(L, KV_LORA, N_HEADS * NOPE)
    wv = w[..., NOPE:].reshape(L, KV_LORA, N_HEADS * V_DIM)
    return wk.astype(jnp.bfloat16), jnp.swapaxes(wv, 1, 2).astype(jnp.bfloat16)


def kernel(x, positions, emb_ln_g, emb_ln_b, w_in, q_norm_g, kv_norm_g, w_uq, w_ukv, w_pool,
           pool_scale, conv_w, w_out, b_out, ln_g, ln_b):
    B, S, D = x.shape
    bf = jnp.bfloat16
    posr = positions.reshape(B, 1, S)
    inv_freq = ROPE_THETA ** (-jnp.arange(HALF, dtype=jnp.float32) / HALF)
    invfc = inv_freq.reshape(HALF, 1)
    row = lambda a: a.reshape(1, -1)

    w_lat, w_mix = _prep_w_in(w_in)
    w_uk, w_uvt = _prep_w_ukv(w_ukv)
    row3 = lambda a: a.reshape(a.shape[0], 1, -1)
    stacked = (w_lat, w_mix, row3(q_norm_g), row3(kv_norm_g), _prep_w_uq_t(w_uq), w_uk, w_uvt,
               w_pool.astype(bf), row3(pool_scale), conv_w)
    w_out_b = w_out.astype(bf)

    h = x
    for l in range(DEPTH):
        outs = _proj_call(l, h, posr, row(emb_ln_g), row(emb_ln_b), invfc, stacked)
        if l == 0:
            qt, k, vt, gm, pc, h = outs
        else:
            qt, k, vt, gm, pc = outs
        o = _attn_call(qt, k, vt, gm)
        h = _out_call(l, o, pc, h, w_out_b, row3(b_out), row3(ln_g), row3(ln_b))
    return h
```

```python
import functools
import math

import jax
import jax.numpy as jnp
import numpy as np
from jax.experimental import pallas as pl
from jax.experimental.pallas import tpu as pltpu

D_MODEL = 2048
DEPTH = 2
N_HEADS = 8
NOPE = 128
ROPE = 64
V_DIM = 128
Q_LORA = 512
KV_LORA = 256
D_MLA = N_HEADS * V_DIM
ROPE_THETA = 10000.0
POOL_WINDOWS = (2, 4, 8, 16)
POOL_GROUP = 128
D_POOL = 512
D_CONV = 512
CONV_WIDTH = 3
LN_EPS = 1e-5
RMS_EPS = 1e-6
DEEPNORM_ALPHA = (2 * DEPTH) ** 0.25

LANES = 128
SUBLANES = 8
VMEM_LIMIT_BYTES = 56 * 1024 * 1024

PROJ_TM = 256
ATT_TQ = 512
ATT_TK = 512
ATT_G = 2
ATT_CH = 64
WPREP_COLS = 512
OUT_TM = 512
OUT_CH = 128
POOL_HALO = 16
CONV_HALO = 8

KT = PROJ_TM
HEAD_W = 2 * LANES
HALF = ROPE // 2

C_QLAT = 0
C_KVLAT = C_QLAT + Q_LORA
C_KR = C_KVLAT + KV_LORA
C_LAT_END = C_KR + LANES
R_GMLA = 0
R_PIN = R_GMLA + D_MLA
R_GPOOL = R_PIN + D_POOL
R_CH = R_GPOOL + D_POOL
R_END = R_CH + 4 * D_CONV

Q_PRESCALE = (NOPE + ROPE) ** -0.5 * math.log2(math.e)
M_INIT = -1e30
BIAS_ROW0 = NOPE + ROPE
BIAS_ROWS = 16
SHIFT_L_MIN = 2.0 ** -100
SHIFT_L_MAX = 2.0 ** 100


def _silu(g):
    return g * (1.0 / (1.0 + jnp.exp(-g)))


def _layernorm(x, g, b):
    mu = jnp.mean(x, axis=-1, keepdims=True)
    xc = x - mu
    var = jnp.mean(xc * xc, axis=-1, keepdims=True)
    return xc * jax.lax.rsqrt(var + LN_EPS) * g + b


def _rmsnorm(x, g):
    return x * jax.lax.rsqrt(jnp.mean(x * x, axis=-1, keepdims=True) + RMS_EPS) * g


def _dot(a, b):
    return jnp.dot(a, b, preferred_element_type=jnp.float32)


def _dot_nt(a, b):
    return jax.lax.dot_general(a, b, (((1,), (1,)), ((), ())), preferred_element_type=jnp.float32)


def _proj_kernel(first, *refs):
    if first:
        (x_ref, posr_ref, eg_ref, eb_ref, w_lat_ref, w_mix_ref, qg_ref, kvg_ref, w_uqt_ref, w_uk_ref,
         w_uvt_ref, w_pool_ref, pscale_ref, convw_ref, invfc_ref,
         qt_ref, k_ref, vt_ref, gm_ref, pc_ref, h_ref, pool_ext, conv_ext) = refs
    else:
        (x_ref, posr_ref, w_lat_ref, w_mix_ref, qg_ref, kvg_ref, w_uqt_ref, w_uk_ref,
         w_uvt_ref, w_pool_ref, pscale_ref, convw_ref, invfc_ref,
         qt_ref, k_ref, vt_ref, gm_ref, pc_ref, pool_ext, conv_ext) = refs
    tm = PROJ_TM
    i = pl.program_id(1)

    @pl.when(i == 0)
    def _():
        pool_ext[0:POOL_HALO, :] = jnp.zeros((POOL_HALO, D_POOL), jnp.float32)
        conv_ext[0:CONV_HALO, :] = jnp.zeros((CONV_HALO, D_CONV), jnp.float32)

    @pl.when(i > 0)
    def _():
        pool_ext[0:POOL_HALO, :] = pool_ext[tm:tm + POOL_HALO, :]
        conv_ext[0:CONV_HALO, :] = conv_ext[tm:tm + CONV_HALO, :]

    x = x_ref[0]
    if first:
        x = _layernorm(x, eg_ref[...], eb_ref[...])
        h_ref[0] = x
    hb = x.astype(jnp.bfloat16)

    lat = _dot(hb, w_lat_ref[0])
    pin = _dot(hb, w_mix_ref[0, :, R_PIN:R_GPOOL])
    cv = _dot(hb, w_mix_ref[0, :, R_CH:R_END])

    qn = _rmsnorm(lat[:, C_QLAT:C_KVLAT], qg_ref[0]).astype(jnp.bfloat16)
    kvn = _rmsnorm(lat[:, C_KVLAT:C_KR], kvg_ref[0]).astype(jnp.bfloat16)
    kr = lat[:, C_KR:C_LAT_END]
    angt = invfc_ref[...] * posr_ref[0].astype(jnp.float32)
    cos_t, sin_t = jnp.cos(angt), jnp.sin(angt)
    cs = jnp.concatenate([cos_t, cos_t, sin_t, sin_t], axis=0).T
    lane = jax.lax.broadcasted_iota(jnp.int32, (tm, LANES), 1)
    t = kr * cs
    ones = jnp.where(lane < ROPE + BIAS_ROWS, 1.0, 0.0)
    kroped = jnp.where(lane < ROPE, t + pltpu.roll(t, ROPE, axis=1), ones).astype(jnp.bfloat16)

    kn = _dot(kvn, w_uk_ref[0])
    vt = _dot_nt(w_uvt_ref[0], kvn)
    qt = _dot_nt(w_uqt_ref[0], qn)
    gpool = _dot(hb, w_mix_ref[0, :, R_GPOOL:R_CH])

    for hh in range(N_HEADS):
        k_ref[0, hh] = jnp.concatenate(
            [kn[:, hh * NOPE:(hh + 1) * NOPE].astype(jnp.bfloat16), kroped], axis=1)
        vt_ref[0, hh, 0] = vt[hh * V_DIM:(hh + 1) * V_DIM, :].astype(jnp.bfloat16)

    u = cv[:, 2 * D_CONV:3 * D_CONV] * cv[:, 0:D_CONV]
    conv_ext[CONV_HALO:CONV_HALO + tm, :] = u
    yc = convw_ref[0, 2:3, :] * u
    yc = yc + convw_ref[0, 1:2, :] * conv_ext[CONV_HALO - 1:CONV_HALO - 1 + tm, :]
    yc = yc + convw_ref[0, 0:1, :] * conv_ext[CONV_HALO - 2:CONV_HALO - 2 + tm, :]
    y_conv = cv[:, D_CONV:2 * D_CONV] * yc * _silu(cv[:, 3 * D_CONV:4 * D_CONV])
    pc_ref[0, :, D_POOL:] = y_conv.astype(jnp.bfloat16)

    pool_ext[POOL_HALO:POOL_HALO + tm, :] = pin
    t1 = (i * tm + 1 + jax.lax.broadcasted_iota(jnp.int32, (tm, 1), 0)).astype(jnp.float32)
    ys = []
    for g, w in enumerate(POOL_WINDOWS):
        lo = g * POOL_GROUP
        acc = pin[:, lo:lo + POOL_GROUP]
        for s in range(1, w):
            acc = acc + pool_ext[POOL_HALO - s:POOL_HALO - s + tm, lo:lo + POOL_GROUP]
        pooled = acc / jnp.minimum(t1, float(w)) - pin[:, lo:lo + POOL_GROUP]
        ys.append(_dot(pooled.astype(jnp.bfloat16), w_pool_ref[0, g]))
    gm = _dot(hb, w_mix_ref[0, :, R_GMLA:R_PIN])
    y_pool = jnp.concatenate(ys, axis=1) * pscale_ref[0] * _silu(gpool)
    pc_ref[0, :, :D_POOL] = y_pool.astype(jnp.bfloat16)

    cos_t = jnp.concatenate([cos_t, cos_t], axis=0)
    sin_t = jnp.concatenate([sin_t, sin_t], axis=0)
    zeros_t = jnp.zeros((HEAD_W - NOPE - ROPE, tm), jnp.float32)
    for hh in range(N_HEADS):
        r0 = hh * HEAD_W
        roped = qt[r0 + NOPE:r0 + NOPE + ROPE, :] * cos_t + qt[r0 + NOPE + ROPE:r0 + HEAD_W, :] * sin_t
        qh = jnp.concatenate([qt[r0:r0 + NOPE, :], roped, zeros_t], axis=0)
        qt_ref[0, hh, 0] = (qh * Q_PRESCALE).astype(jnp.bfloat16)

    gm_ref[0] = _silu(gm).astype(jnp.bfloat16)


def _const_spec(shape):
    nd = len(shape)
    return pl.BlockSpec(shape, lambda *_: (0,) * nd, pipeline_mode=pl.Buffered(1))


def _layer_spec(arr, layer):
    nd = arr.ndim
    return pl.BlockSpec((1,) + arr.shape[1:], lambda *_: (layer,) + (0,) * (nd - 1),
                        pipeline_mode=pl.Buffered(1))


def _proj_call(layer, x, posr, emb_g, emb_b, inv_freq_col, stacked):
    first = layer == 0
    B, S, D = x.shape
    tm = PROJ_TM
    grid = (B, S // tm)
    qsub = ATT_TQ // tm
    tok = lambda w: pl.BlockSpec((1, tm, w), lambda b, i: (b, i, 0))
    in_specs = [tok(D), pl.BlockSpec((1, 1, tm), lambda b, i: (b, 0, i))]
    args = [x, posr]
    if first:
        in_specs += [_const_spec((1, D)), _const_spec((1, D))]
        args += [emb_g, emb_b]
    in_specs += [_layer_spec(a, layer) for a in stacked] + [_const_spec(inv_freq_col.shape)]
    args += list(stacked) + [inv_freq_col]
    bf = jnp.bfloat16
    out_shape = [
        jax.ShapeDtypeStruct((B, N_HEADS, S // ATT_TQ, HEAD_W, ATT_TQ), bf),
        jax.ShapeDtypeStruct((B, N_HEADS, S, HEAD_W), bf),
        jax.ShapeDtypeStruct((B, N_HEADS, S // KT, V_DIM, KT), bf),
        jax.ShapeDtypeStruct((B, S, D_MLA), bf),
        jax.ShapeDtypeStruct((B, S, D_POOL + D_CONV), bf),
    ]
    out_specs = [
        pl.BlockSpec((1, N_HEADS, 1, HEAD_W, tm), lambda b, i: (b, 0, i // qsub, 0, i % qsub)),
        pl.BlockSpec((1, N_HEADS, tm, HEAD_W), lambda b, i: (b, 0, i, 0)),
        pl.BlockSpec((1, N_HEADS, 1, V_DIM, KT), lambda b, i: (b, 0, i, 0, 0)),
        tok(D_MLA), tok(D_POOL + D_CONV)]
    if first:
        out_shape.append(jax.ShapeDtypeStruct((B, S, D), jnp.float32))
        out_specs.append(tok(D))
    return pl.pallas_call(
        functools.partial(_proj_kernel, first),
        grid=grid,
        in_specs=in_specs,
        out_specs=out_specs,
        out_shape=out_shape,
        scratch_shapes=[
            pltpu.VMEM((POOL_HALO + tm, D_POOL), jnp.float32),
            pltpu.VMEM((CONV_HALO + tm, D_CONV), jnp.float32),
        ],
        compiler_params=pltpu.CompilerParams(
            dimension_semantics=("arbitrary", "arbitrary"),
            vmem_limit_bytes=VMEM_LIMIT_BYTES),
        name="proj_first" if first else "proj",
    )(*args)


def _rows8_reduce(x, op):
    acc = x[0:SUBLANES]
    for r in range(SUBLANES, x.shape[0], SUBLANES):
        acc = op(acc, x[r:r + SUBLANES])
    return acc


def _attn_kernel(qt_ref, k_ref, vt_ref, gm_ref, o_ref,
                 qb_sc, s_sc, p_sc, l8_sc, acc_sc, m_sc, l_sc):
    tq, tk, G, ch = ATT_TQ, ATT_TK, ATT_G, ATT_CH
    sub, nch, nqb = tk // KT, tk // ch, tq // LANES
    nq = qt_ref.shape[2]

    def vt_tile(g, n):
        return jnp.concatenate([vt_ref[0, g, n * sub + c] for c in range(sub)], axis=1)

    def k_tile(g, n):
        return k_ref[0, g, pl.ds(pl.multiple_of(n * tk, tk), tk), :]

    def rows(c):
        return slice(c * ch, (c + 1) * ch)

    def lanes(j):
        return slice(j * LANES, (j + 1) * LANES)

    def exp_chunks(g, chunk_of, buf, lsum):
        for c in range(nch):
            p_c = jnp.exp2(chunk_of(c))
            lsum = lsum + _rows8_reduce(p_c, jnp.add)
            p_sc[buf, g, rows(c), :] = p_c.astype(jnp.bfloat16)
        return lsum

    def qk_exp(g, buf, n):
        s = _dot(k_tile(g, n), qb_sc[g])
        l8_sc[g] = exp_chunks(g, lambda c: s[rows(c)], buf, l8_sc[g])

    def pv(g, buf, n):
        acc_sc[g] = acc_sc[g] + _dot(vt_tile(g, n), p_sc[buf, g])

    def emit(i):
        q0 = pl.multiple_of(i * tq, tq)
        for g in range(G):
            out_t = acc_sc[g] / l_sc[g]
            gate = gm_ref[0, pl.ds(q0, tq), g * V_DIM:(g + 1) * V_DIM].astype(jnp.float32)
            o_ref[0, pl.ds(q0, tq), g * V_DIM:(g + 1) * V_DIM] = (out_t.T * gate).astype(o_ref.dtype)

    def finish(i):
        last = jnp.maximum(i - 1, 0)
        in_buf1 = jnp.logical_or(i == 0, last % 2 == 1)
        for g in range(G):
            p_last = jnp.where(in_buf1, p_sc[1, g], p_sc[0, g])
            acc_sc[g] = acc_sc[g] + _dot(vt_tile(g, last), p_last)
        emit(i)

    def diagonal(i):
        for g in range(G):
            s_sc[g] = _dot(k_tile(g, i), qt_ref[0, g, i])
            qb_sc[g] = qt_ref[0, g, i]

        def block(g, c, j):
            s_b = s_sc[g, rows(c), lanes(j)]
            if c < 2 * j:
                return s_b
            key = c * ch + jax.lax.broadcasted_iota(jnp.int32, (ch, LANES), 0)
            qry = j * LANES + jax.lax.broadcasted_iota(jnp.int32, (ch, LANES), 1)
            return jnp.where(key <= qry, s_b, -jnp.inf)

        for g in range(G):
            for j in range(nqb):
                visible = range(2 * j + 2)
                m8 = _rows8_reduce(block(g, 0, j), jnp.maximum)
                for c in visible[1:]:
                    m8 = jnp.maximum(m8, _rows8_reduce(block(g, c, j), jnp.maximum))
                m_d = jnp.max(m8, axis=0, keepdims=True)
                m_ref = m_d.astype(jnp.bfloat16).astype(jnp.float32)
                lsum = jnp.zeros((SUBLANES, LANES), jnp.float32)
                for c in range(nch):
                    if c in visible:
                        p_b = jnp.exp2(block(g, c, j) - m_ref)
                        lsum = lsum + _rows8_reduce(p_b, jnp.add)
                        p_sc[1, g, rows(c), lanes(j)] = p_b.astype(jnp.bfloat16)
                    else:
                        p_sc[1, g, rows(c), lanes(j)] = jnp.zeros((ch, LANES), jnp.bfloat16)
                l8_sc[g, :, lanes(j)] = lsum
                bias = jnp.broadcast_to(m_ref * (-1.0 / BIAS_ROWS), (BIAS_ROWS, LANES))
                qb_sc[g, BIAS_ROW0:BIAS_ROW0 + BIAS_ROWS, lanes(j)] = bias.astype(jnp.bfloat16)
        acc_sc[...] = jnp.zeros(acc_sc.shape, jnp.float32)

    def slow_path(i):
        m_sc[...] = jnp.full(m_sc.shape, M_INIT, jnp.float32)
        l_sc[...] = jnp.zeros(l_sc.shape, jnp.float32)
        acc_sc[...] = jnp.zeros(acc_sc.shape, jnp.float32)

        def chunk(g, c, masked):
            s_c = s_sc[g, rows(c), :]
            if not masked:
                return s_c
            key = c * ch + jax.lax.broadcasted_iota(jnp.int32, (ch, tq), 0)
            return jnp.where(key <= jax.lax.broadcasted_iota(jnp.int32, (ch, tq), 1), s_c, -jnp.inf)

        def tile(n, masked):
            for g in range(G):
                s_sc[g] = _dot(k_tile(g, n), qt_ref[0, g, i])
                m8 = _rows8_reduce(chunk(g, 0, masked), jnp.maximum)
                for c in range(1, nch):
                    m8 = jnp.maximum(m8, _rows8_reduce(chunk(g, c, masked), jnp.maximum))
                m_prev = m_sc[g]
                m_new = jnp.maximum(m_prev, jnp.max(m8, axis=0, keepdims=True))
                alpha = jnp.exp2(m_prev - m_new)
                lsum = exp_chunks(g, lambda c: chunk(g, c, masked) - m_new, 0,
                                  jnp.zeros((SUBLANES, tq), jnp.float32))
                l_sc[g] = alpha * l_sc[g] + jnp.sum(lsum, axis=0, keepdims=True)
                acc_sc[g] = alpha * acc_sc[g] + _dot(vt_tile(g, n), p_sc[0, g])
                m_sc[g] = m_new

        def full(n, carry):
            tile(n, False)
            return carry

        jax.lax.fori_loop(0, i, full, 0)
        tile(i, True)

    def denominators(l_range):
        l_lo, l_hi = l_range
        for g in range(G):
            l = jnp.sum(l8_sc[g], axis=0, keepdims=True)
            l_sc[g] = l
            l_lo = jnp.minimum(l_lo, jnp.min(l))
            l_hi = jnp.maximum(l_hi, jnp.max(l))
        return l_lo, l_hi

    def query_tile(i, l_range):
        l_range = denominators(l_range)
        finish(jnp.maximum(i - 1, 0))
        diagonal(i)

        def pair(t, c):
            n = 2 * t
            for g in range(G):
                qk_exp(g, 0, n)
                pv(g, 1, jnp.where(t == 0, i, n - 1))
            for g in range(G):
                qk_exp(g, 1, n + 1)
                pv(g, 0, n)
            return c

        jax.lax.fori_loop(0, i // 2, pair, 0)

        @pl.when(i % 2 == 1)
        def _():
            for g in range(G):
                qk_exp(g, 0, i - 1)
                pv(g, 1, jnp.where(i == 1, i, i - 2))

        return l_range

    acc_sc[...] = jnp.zeros(acc_sc.shape, jnp.float32)
    l8_sc[...] = jnp.ones(l8_sc.shape, jnp.float32)
    p_sc[1] = jnp.zeros(p_sc.shape[1:], jnp.bfloat16)
    l_range = jax.lax.fori_loop(0, nq, query_tile, (jnp.float32(jnp.inf), jnp.float32(0.0)))
    l_lo, l_hi = denominators(l_range)
    finish(nq - 1)

    in_range = jnp.logical_and(l_lo > SHIFT_L_MIN, l_hi < SHIFT_L_MAX)

    @pl.when(jnp.logical_not(in_range))
    def _():
        def redo(i, carry):
            slow_path(i)
            emit(i)
            return carry

        jax.lax.fori_loop(0, nq, redo, 0)


def _attn_call(qt, k, vt, gm):
    B, H, S, _ = k.shape
    tq, G = ATT_TQ, ATT_G
    assert ATT_TQ == ATT_TK and ATT_TK % KT == 0
    seq = lambda a: pl.BlockSpec((1, G) + a.shape[2:], lambda b, h: (b, h) + (0,) * (a.ndim - 2))
    tok = pl.BlockSpec((1, S, G * V_DIM), lambda b, h: (b, 0, h))
    return pl.pallas_call(
        _attn_kernel,
        grid=(B, H // G),
        in_specs=[seq(qt), seq(k), seq(vt), tok],
        out_specs=tok,
        out_shape=jax.ShapeDtypeStruct((B, S, D_MLA), jnp.bfloat16),
        scratch_shapes=[
            pltpu.VMEM((G, HEAD_W, tq), jnp.bfloat16),
            pltpu.VMEM((G, ATT_TK, tq), jnp.float32),
            pltpu.VMEM((2, G, ATT_TK, tq), jnp.bfloat16),
            pltpu.VMEM((G, SUBLANES, tq), jnp.float32),
            pltpu.VMEM((G, V_DIM, tq), jnp.float32),
            pltpu.VMEM((G, 1, tq), jnp.float32),
            pltpu.VMEM((G, 1, tq), jnp.float32),
        ],
        compiler_params=pltpu.CompilerParams(
            dimension_semantics=("arbitrary", "arbitrary"),
            vmem_limit_bytes=VMEM_LIMIT_BYTES),
        name="attn",
    )(qt, k, vt, gm)


def _out_kernel(o_ref, pc_ref, h_ref, w_ref, b_ref, g_ref, beta_ref, y_ref):
    for r in range(0, OUT_TM, OUT_CH):
        rows = slice(r, r + OUT_CH)
        mix = jnp.concatenate([o_ref[0, rows, :], pc_ref[0, rows, :]], axis=1)
        y = _dot(mix, w_ref[0]) + b_ref[0] + DEEPNORM_ALPHA * h_ref[0, rows, :]
        y_ref[0, rows, :] = _layernorm(y, g_ref[0], beta_ref[0])


def _out_call(layer, o, pc, h, w_out, b_out, ln_g, ln_b):
    B, S, D = h.shape
    tm = OUT_TM
    tok = lambda w: pl.BlockSpec((1, tm, w), lambda b, i: (b, i, 0))
    return pl.pallas_call(
        _out_kernel,
        grid=(B, S // tm),
        in_specs=[tok(D_MLA), tok(D_POOL + D_CONV), tok(D),
                  _layer_spec(w_out, layer), _layer_spec(b_out, layer), _layer_spec(ln_g, layer),
                  _layer_spec(ln_b, layer)],
        out_specs=tok(D),
        out_shape=jax.ShapeDtypeStruct((B, S, D), jnp.float32),
        compiler_params=pltpu.CompilerParams(
            dimension_semantics=("arbitrary", "arbitrary"),
            vmem_limit_bytes=VMEM_LIMIT_BYTES),
        name="out",
    )(o, pc, h, w_out, b_out, ln_g, ln_b)


def _rotate_half_cols(w):
    return jnp.concatenate([-w[..., HALF:], w[..., :HALF]], axis=-1)


def _wprep_lat_kernel(wt_ref, o_ref):
    x = wt_ref[0]
    kr = C_KR
    rot = jnp.concatenate([-x[kr + HALF:kr + ROPE], x[kr:kr + HALF]], axis=0)
    o_ref[0] = jnp.concatenate([x, rot], axis=0).T.astype(jnp.bfloat16)


def _wprep_mix_kernel(wt_ref, o_ref):
    o_ref[0] = wt_ref[0].T.astype(jnp.bfloat16)


def _prep_w_in(w):
    L, D, C = w.shape
    wt = jnp.swapaxes(w, 1, 2)
    n_lat = C_KR + ROPE
    params = pltpu.CompilerParams(vmem_limit_bytes=VMEM_LIMIT_BYTES)
    w_lat = pl.pallas_call(
        _wprep_lat_kernel,
        grid=(L,),
        in_specs=[pl.BlockSpec((pl.Element(1), pl.Element(n_lat), pl.Element(D)), lambda l: (l, 0, 0))],
        out_specs=pl.BlockSpec((1, D, C_LAT_END), lambda l: (l, 0, 0)),
        out_shape=jax.ShapeDtypeStruct((L, D, C_LAT_END), jnp.bfloat16),
        compiler_params=params,
        name="wprep_lat",
    )(wt)
    w_mix = pl.pallas_call(
        _wprep_mix_kernel,
        grid=(L, R_END // WPREP_COLS),
        in_specs=[pl.BlockSpec((pl.Element(1), pl.Element(WPREP_COLS), pl.Element(D)),
                               lambda l, j: (l, pl.multiple_of(n_lat + WPREP_COLS * j, SUBLANES), 0))],
        out_specs=pl.BlockSpec((1, D, WPREP_COLS), lambda l, j: (l, 0, j)),
        out_shape=jax.ShapeDtypeStruct((L, D, R_END), jnp.bfloat16),
        compiler_params=params,
        name="wprep_mix",
    )(wt)
    return w_lat, w_mix


def _prep_w_uq_t(w):
    L = w.shape[0]
    w = w.reshape(L, Q_LORA, N_HEADS, NOPE + ROPE)
    w = jnp.concatenate([w, _rotate_half_cols(w[..., NOPE:])], axis=-1)
    return jnp.swapaxes(w.reshape(L, Q_LORA, N_HEADS * HEAD_W), 1, 2).astype(jnp.bfloat16)


def _prep_w_ukv(w):
    L = w.shape[0]
    w = w.reshape(L, KV_LORA, N_HEADS, NOPE + V_DIM)
    wk = w[..., :NOPE].reshape(L, KV_LORA, N_HEADS * NOPE)
    wv = w[..., NOPE:].reshape(L, KV_LORA, N_HEADS * V_DIM)
    return wk.astype(jnp.bfloat16), jnp.swapaxes(wv, 1, 2).astype(jnp.bfloat16)


def kernel(x, positions, emb_ln_g, emb_ln_b, w_in, q_norm_g, kv_norm_g, w_uq, w_ukv, w_pool,
           pool_scale, conv_w, w_out, b_out, ln_g, ln_b):
    B, S, D = x.shape
    bf = jnp.bfloat16
    posr = positions.reshape(B, 1, S)
    inv_freq = ROPE_THETA ** (-jnp.arange(HALF, dtype=jnp.float32) / HALF)
    invfc = inv_freq.reshape(HALF, 1)
    row = lambda a: a.reshape(1, -1)

    w_lat, w_mix = _prep_w_in(w_in)
    w_uk, w_uvt = _prep_w_ukv(w_ukv)
    row3 = lambda a: a.reshape(a.shape[0], 1, -1)
    stacked = (w_lat, w_mix, row3(q_norm_g), row3(kv_norm_g), _prep_w_uq_t(w_uq), w_uk, w_uvt,
               w_pool.astype(bf), row3(pool_scale), conv_w)
    w_out_b = w_out.astype(bf)

    h = x
    for l in range(DEPTH):
        outs = _proj_call(l, h, posr, row(emb_ln_g), row(emb_ln_b), invfc, stacked)
        if l == 0:
            qt, k, vt, gm, pc, h = outs
        else:
            qt, k, vt, gm, pc = outs
        o = _attn_call(qt, k, vt, gm)
        h = _out_call(l, o, pc, h, w_out_b, row3(b_out), row3(ln_g), row3(ln_b))
    return h
```

```python
import functools
import math

import jax
import jax.numpy as jnp
import numpy as np
from jax.experimental import pallas as pl
from jax.experimental.pallas import tpu as pltpu

D_MODEL = 2048
DEPTH = 2
N_HEADS = 8
NOPE = 128
ROPE = 64
V_DIM = 128
Q_LORA = 512
KV_LORA = 256
D_MLA = N_HEADS * V_DIM
ROPE_THETA = 10000.0
POOL_WINDOWS = (2, 4, 8, 16)
POOL_GROUP = 128
D_POOL = 512
D_CONV = 512
CONV_WIDTH = 3
LN_EPS = 1e-5
RMS_EPS = 1e-6
DEEPNORM_ALPHA = (2 * DEPTH) ** 0.25

LANES = 128
SUBLANES = 8
VMEM_LIMIT_BYTES = 56 * 1024 * 1024

PROJ_TM = 256
ATT_TQ = 512
ATT_TK = 512
ATT_G = 2
ATT_CH = 64
WPREP_COLS = 1024
OUT_TM = 512
OUT_CH = 128
POOL_HALO = 16
CONV_HALO = 8

KT = PROJ_TM
HEAD_W = 2 * LANES
HALF = ROPE // 2

C_QLAT = 0
C_KVLAT = C_QLAT + Q_LORA
C_KR = C_KVLAT + KV_LORA
C_LAT_END = C_KR + LANES
R_GMLA = 0
R_PIN = R_GMLA + D_MLA
R_GPOOL = R_PIN + D_POOL
R_CH = R_GPOOL + D_POOL
R_END = R_CH + 4 * D_CONV

Q_PRESCALE = (NOPE + ROPE) ** -0.5 * math.log2(math.e)
M_INIT = -1e30
BIAS_ROW0 = NOPE + ROPE
BIAS_ROWS = 16
SHIFT_L_MIN = 2.0 ** -100
SHIFT_L_MAX = 2.0 ** 100


def _silu(g):
    return g * (1.0 / (1.0 + jnp.exp(-g)))


def _layernorm(x, g, b):
    mu = jnp.mean(x, axis=-1, keepdims=True)
    xc = x - mu
    var = jnp.mean(xc * xc, axis=-1, keepdims=True)
    return xc * jax.lax.rsqrt(var + LN_EPS) * g + b


def _rmsnorm(x, g):
    return x * jax.lax.rsqrt(jnp.mean(x * x, axis=-1, keepdims=True) + RMS_EPS) * g


def _dot(a, b):
    return jnp.dot(a, b, preferred_element_type=jnp.float32)


def _dot_nt(a, b):
    return jax.lax.dot_general(a, b, (((1,), (1,)), ((), ())), preferred_element_type=jnp.float32)


def _proj_kernel(first, *refs):
    if first:
        (x_ref, posr_ref, eg_ref, eb_ref, w_lat_ref, w_mix_ref, qg_ref, kvg_ref, w_uqt_ref, w_uk_ref,
         w_uvt_ref, w_pool_ref, pscale_ref, convw_ref, invfc_ref,
         qt_ref, k_ref, vt_ref, gm_ref, pc_ref, h_ref, pool_ext, conv_ext) = refs
    else:
        (x_ref, posr_ref, w_lat_ref, w_mix_ref, qg_ref, kvg_ref, w_uqt_ref, w_uk_ref,
         w_uvt_ref, w_pool_ref, pscale_ref, convw_ref, invfc_ref,
         qt_ref, k_ref, vt_ref, gm_ref, pc_ref, pool_ext, conv_ext) = refs
    tm = PROJ_TM
    i = pl.program_id(1)

    @pl.when(i == 0)
    def _():
        pool_ext[0:POOL_HALO, :] = jnp.zeros((POOL_HALO, D_POOL), jnp.float32)
        conv_ext[0:CONV_HALO, :] = jnp.zeros((CONV_HALO, D_CONV), jnp.float32)

    @pl.when(i > 0)
    def _():
        pool_ext[0:POOL_HALO, :] = pool_ext[tm:tm + POOL_HALO, :]
        conv_ext[0:CONV_HALO, :] = conv_ext[tm:tm + CONV_HALO, :]

    x = x_ref[0]
    if first:
        x = _layernorm(x, eg_ref[...], eb_ref[...])
        h_ref[0] = x
    hb = x.astype(jnp.bfloat16)

    lat = _dot(hb, w_lat_ref[0])
    pin = _dot(hb, w_mix_ref[0, :, R_PIN:R_GPOOL])
    cv = _dot(hb, w_mix_ref[0, :, R_CH:R_END])

    qn = _rmsnorm(lat[:, C_QLAT:C_KVLAT], qg_ref[0]).astype(jnp.bfloat16)
    kvn = _rmsnorm(lat[:, C_KVLAT:C_KR], kvg_ref[0]).astype(jnp.bfloat16)
    kr = lat[:, C_KR:C_LAT_END]
    angt = invfc_ref[...] * posr_ref[0].astype(jnp.float32)
    cos_t, sin_t = jnp.cos(angt), jnp.sin(angt)
    cs = jnp.concatenate([cos_t, cos_t, sin_t, sin_t], axis=0).T
    lane = jax.lax.broadcasted_iota(jnp.int32, (tm, LANES), 1)
    t = kr * cs
    ones = jnp.where(lane < ROPE + BIAS_ROWS, 1.0, 0.0)
    kroped = jnp.where(lane < ROPE, t + pltpu.roll(t, ROPE, axis=1), ones).astype(jnp.bfloat16)

    kn = _dot(kvn, w_uk_ref[0])
    vt = _dot_nt(w_uvt_ref[0], kvn)
    qt = _dot_nt(w_uqt_ref[0], qn)
    gpool = _dot(hb, w_mix_ref[0, :, R_GPOOL:R_CH])

    for hh in range(N_HEADS):
        k_ref[0, hh] = jnp.concatenate(
            [kn[:, hh * NOPE:(hh + 1) * NOPE].astype(jnp.bfloat16), kroped], axis=1)
        vt_ref[0, hh, 0] = vt[hh * V_DIM:(hh + 1) * V_DIM, :].astype(jnp.bfloat16)

    u = cv[:, 2 * D_CONV:3 * D_CONV] * cv[:, 0:D_CONV]
    conv_ext[CONV_HALO:CONV_HALO + tm, :] = u
    yc = convw_ref[0, 2:3, :] * u
    yc = yc + convw_ref[0, 1:2, :] * conv_ext[CONV_HALO - 1:CONV_HALO - 1 + tm, :]
    yc = yc + convw_ref[0, 0:1, :] * conv_ext[CONV_HALO - 2:CONV_HALO - 2 + tm, :]
    y_conv = cv[:, D_CONV:2 * D_CONV] * yc * _silu(cv[:, 3 * D_CONV:4 * D_CONV])
    pc_ref[0, :, D_POOL:] = y_conv.astype(jnp.bfloat16)

    pool_ext[POOL_HALO:POOL_HALO + tm, :] = pin
    t1 = (i * tm + 1 + jax.lax.broadcasted_iota(jnp.int32, (tm, 1), 0)).astype(jnp.float32)
    ys = []
    for g, w in enumerate(POOL_WINDOWS):
        lo = g * POOL_GROUP
        acc = pin[:, lo:lo + POOL_GROUP]
        for s in range(1, w):
            acc = acc + pool_ext[POOL_HALO - s:POOL_HALO - s + tm, lo:lo + POOL_GROUP]
        pooled = acc / jnp.minimum(t1, float(w)) - pin[:, lo:lo + POOL_GROUP]
        ys.append(_dot(pooled.astype(jnp.bfloat16), w_pool_ref[0, g]))
    gm = _dot(hb, w_mix_ref[0, :, R_GMLA:R_PIN])
    y_pool = jnp.concatenate(ys, axis=1) * pscale_ref[0] * _silu(gpool)
    pc_ref[0, :, :D_POOL] = y_pool.astype(jnp.bfloat16)

    zeros_t = jnp.zeros((HEAD_W - NOPE - ROPE, tm), jnp.float32)
    for hh in range(N_HEADS):
        r0 = hh * (NOPE + ROPE)
        x1 = qt[r0 + NOPE:r0 + NOPE + HALF, :]
        x2 = qt[r0 + NOPE + HALF:r0 + NOPE + ROPE, :]
        roped = [x1 * cos_t - x2 * sin_t, x2 * cos_t + x1 * sin_t]
        qh = jnp.concatenate([qt[r0:r0 + NOPE, :]] + roped + [zeros_t], axis=0)
        qt_ref[0, hh, 0] = (qh * Q_PRESCALE).astype(jnp.bfloat16)

    gm_ref[0] = _silu(gm).astype(jnp.bfloat16)


def _const_spec(shape):
    nd = len(shape)
    return pl.BlockSpec(shape, lambda *_: (0,) * nd, pipeline_mode=pl.Buffered(1))


def _layer_spec(arr, layer):
    nd = arr.ndim
    return pl.BlockSpec((1,) + arr.shape[1:], lambda *_: (layer,) + (0,) * (nd - 1),
                        pipeline_mode=pl.Buffered(1))


def _proj_call(layer, x, posr, emb_g, emb_b, inv_freq_col, stacked):
    first = layer == 0
    B, S, D = x.shape
    tm = PROJ_TM
    grid = (B, S // tm)
    qsub = ATT_TQ // tm
    tok = lambda w: pl.BlockSpec((1, tm, w), lambda b, i: (b, i, 0))
    in_specs = [tok(D), pl.BlockSpec((1, 1, tm), lambda b, i: (b, 0, i))]
    args = [x, posr]
    if first:
        in_specs += [_const_spec((1, D)), _const_spec((1, D))]
        args += [emb_g, emb_b]
    in_specs += [_layer_spec(a, layer) for a in stacked] + [_const_spec(inv_freq_col.shape)]
    args += list(stacked) + [inv_freq_col]
    bf = jnp.bfloat16
    out_shape = [
        jax.ShapeDtypeStruct((B, N_HEADS, S // ATT_TQ, HEAD_W, ATT_TQ), bf),
        jax.ShapeDtypeStruct((B, N_HEADS, S, HEAD_W), bf),
        jax.ShapeDtypeStruct((B, N_HEADS, S // KT, V_DIM, KT), bf),
        jax.ShapeDtypeStruct((B, S, D_MLA), bf),
        jax.ShapeDtypeStruct((B, S, D_POOL + D_CONV), bf),
    ]
    out_specs = [
        pl.BlockSpec((1, N_HEADS, 1, HEAD_W, tm), lambda b, i: (b, 0, i // qsub, 0, i % qsub)),
        pl.BlockSpec((1, N_HEADS, tm, HEAD_W), lambda b, i: (b, 0, i, 0)),
        pl.BlockSpec((1, N_HEADS, 1, V_DIM, KT), lambda b, i: (b, 0, i, 0, 0)),
        tok(D_MLA), tok(D_POOL + D_CONV)]
    if first:
        out_shape.append(jax.ShapeDtypeStruct((B, S, D), jnp.float32))
        out_specs.append(tok(D))
    return pl.pallas_call(
        functools.partial(_proj_kernel, first),
        grid=grid,
        in_specs=in_specs,
        out_specs=out_specs,
        out_shape=out_shape,
        scratch_shapes=[
            pltpu.VMEM((POOL_HALO + tm, D_POOL), jnp.float32),
            pltpu.VMEM((CONV_HALO + tm, D_CONV), jnp.float32),
        ],
        compiler_params=pltpu.CompilerParams(
            dimension_semantics=("arbitrary", "arbitrary"),
            vmem_limit_bytes=VMEM_LIMIT_BYTES),
        name="proj_first" if first else "proj",
    )(*args)


def _rows8_reduce(x, op):
    acc = x[0:SUBLANES]
    for r in range(SUBLANES, x.shape[0], SUBLANES):
        acc = op(acc, x[r:r + SUBLANES])
    return acc


def _attn_kernel(qt_ref, k_ref, vt_ref, gm_ref, o_ref,
                 qb_sc, s_sc, p_sc, l8_sc, acc_sc, m_sc, l_sc):
    tq, tk, G, ch = ATT_TQ, ATT_TK, ATT_G, ATT_CH
    sub, nch, nqb = tk // KT, tk // ch, tq // LANES
    per_lane_block = LANES // ch
    nq = qt_ref.shape[2]

    def vt_tile(g, n):
        return jnp.concatenate([vt_ref[0, g, n * sub + c] for c in range(sub)], axis=1)

    def k_tile(g, n):
        return k_ref[0, g, pl.ds(pl.multiple_of(n * tk, tk), tk), :]

    def rows(c):
        return slice(c * ch, (c + 1) * ch)

    def lanes(j):
        return slice(j * LANES, (j + 1) * LANES)

    def exp_chunks(g, chunk_of, buf, lsum):
        for c in range(nch):
            p_c = jnp.exp2(chunk_of(c))
            lsum = lsum + _rows8_reduce(p_c, jnp.add)
            p_sc[buf, g, rows(c), :] = p_c.astype(jnp.bfloat16)
        return lsum

    def qk_exp(g, buf, n):
        s = _dot(k_tile(g, n), qb_sc[g])
        l8_sc[g] = exp_chunks(g, lambda c: s[rows(c)], buf, l8_sc[g])

    def pv(g, buf, n):
        acc_sc[g] = acc_sc[g] + _dot(vt_tile(g, n), p_sc[buf, g])

    def emit(i):
        q0 = pl.multiple_of(i * tq, tq)
        for g in range(G):
            out_t = acc_sc[g] / l_sc[g]
            gate = gm_ref[0, pl.ds(q0, tq), g * V_DIM:(g + 1) * V_DIM].astype(jnp.float32)
            o_ref[0, pl.ds(q0, tq), g * V_DIM:(g + 1) * V_DIM] = (out_t.T * gate).astype(o_ref.dtype)

    def finish(i):
        last = jnp.maximum(i - 1, 0)
        in_buf1 = jnp.logical_or(i == 0, last % 2 == 1)
        for g in range(G):
            p_last = jnp.where(in_buf1, p_sc[1, g], p_sc[0, g])
            acc_sc[g] = acc_sc[g] + _dot(vt_tile(g, last), p_last)
        emit(i)

    def diagonal(i):
        for g in range(G):
            s_sc[g] = _dot(k_tile(g, i), qt_ref[0, g, i])
            qb_sc[g] = qt_ref[0, g, i]

        def block(g, c, j):
            s_b = s_sc[g, rows(c), lanes(j)]
            if c < per_lane_block * j:
                return s_b
            key = c * ch + jax.lax.broadcasted_iota(jnp.int32, (ch, LANES), 0)
            qry = j * LANES + jax.lax.broadcasted_iota(jnp.int32, (ch, LANES), 1)
            return jnp.where(key <= qry, s_b, -jnp.inf)

        for g in range(G):
            for j in range(nqb):
                visible = range(per_lane_block * (j + 1))
                m8 = _rows8_reduce(block(g, 0, j), jnp.maximum)
                for c in visible[1:]:
                    m8 = jnp.maximum(m8, _rows8_reduce(block(g, c, j), jnp.maximum))
                m_d = jnp.max(m8, axis=0, keepdims=True)
                m_ref = m_d.astype(jnp.bfloat16).astype(jnp.float32)
                lsum = jnp.zeros((SUBLANES, LANES), jnp.float32)
                for c in range(nch):
                    if c in visible:
                        p_b = jnp.exp2(block(g, c, j) - m_ref)
                        lsum = lsum + _rows8_reduce(p_b, jnp.add)
                        p_sc[1, g, rows(c), lanes(j)] = p_b.astype(jnp.bfloat16)
                    else:
                        p_sc[1, g, rows(c), lanes(j)] = jnp.zeros((ch, LANES), jnp.bfloat16)
                l8_sc[g, :, lanes(j)] = lsum
                bias = jnp.broadcast_to(m_ref * (-1.0 / BIAS_ROWS), (BIAS_ROWS, LANES))
                qb_sc[g, BIAS_ROW0:BIAS_ROW0 + BIAS_ROWS, lanes(j)] = bias.astype(jnp.bfloat16)
        acc_sc[...] = jnp.zeros(acc_sc.shape, jnp.float32)

    def slow_path(i):
        m_sc[...] = jnp.full(m_sc.shape, M_INIT, jnp.float32)
        l_sc[...] = jnp.zeros(l_sc.shape, jnp.float32)
        acc_sc[...] = jnp.zeros(acc_sc.shape, jnp.float32)

        def chunk(g, c, masked):
            s_c = s_sc[g, rows(c), :]
            if not masked:
                return s_c
            key = c * ch + jax.lax.broadcasted_iota(jnp.int32, (ch, tq), 0)
            return jnp.where(key <= jax.lax.broadcasted_iota(jnp.int32, (ch, tq), 1), s_c, -jnp.inf)

        def tile(n, masked):
            for g in range(G):
                s_sc[g] = _dot(k_tile(g, n), qt_ref[0, g, i])
                m8 = _rows8_reduce(chunk(g, 0, masked), jnp.maximum)
                for c in range(1, nch):
                    m8 = jnp.maximum(m8, _rows8_reduce(chunk(g, c, masked), jnp.maximum))
                m_prev = m_sc[g]
                m_new = jnp.maximum(m_prev, jnp.max(m8, axis=0, keepdims=True))
                alpha = jnp.exp2(m_prev - m_new)
                lsum = exp_chunks(g, lambda c: chunk(g, c, masked) - m_new, 0,
                                  jnp.zeros((SUBLANES, tq), jnp.float32))
                l_sc[g] = alpha * l_sc[g] + jnp.sum(lsum, axis=0, keepdims=True)
                acc_sc[g] = alpha * acc_sc[g] + _dot(vt_tile(g, n), p_sc[0, g])
                m_sc[g] = m_new

        def full(n, carry):
            tile(n, False)
            return carry

        jax.lax.fori_loop(0, i, full, 0)
        tile(i, True)

    def denominators(l_range):
        l_lo, l_hi = l_range
        for g in range(G):
            l = jnp.sum(l8_sc[g], axis=0, keepdims=True)
            l_sc[g] = l
            l_lo = jnp.minimum(l_lo, jnp.min(l))
            l_hi = jnp.maximum(l_hi, jnp.max(l))
        return l_lo, l_hi

    def query_tile(i, l_range):
        l_range = denominators(l_range)
        finish(jnp.maximum(i - 1, 0))
        diagonal(i)

        def pair(t, c):
            n = 2 * t
            for g in range(G):
                qk_exp(g, 0, n)
                pv(g, 1, jnp.where(t == 0, i, n - 1))
            for g in range(G):
                qk_exp(g, 1, n + 1)
                pv(g, 0, n)
            return c

        jax.lax.fori_loop(0, i // 2, pair, 0)

        @pl.when(i % 2 == 1)
        def _():
            for g in range(G):
                qk_exp(g, 0, i - 1)
                pv(g, 1, jnp.where(i == 1, i, i - 2))

        return l_range

    acc_sc[...] = jnp.zeros(acc_sc.shape, jnp.float32)
    l8_sc[...] = jnp.ones(l8_sc.shape, jnp.float32)
    p_sc[1] = jnp.zeros(p_sc.shape[1:], jnp.bfloat16)
    l_range = jax.lax.fori_loop(0, nq, query_tile, (jnp.float32(jnp.inf), jnp.float32(0.0)))
    l_lo, l_hi = denominators(l_range)
    finish(nq - 1)

    in_range = jnp.logical_and(l_lo > SHIFT_L_MIN, l_hi < SHIFT_L_MAX)

    @pl.when(jnp.logical_not(in_range))
    def _():
        def redo(i, carry):
            slow_path(i)
            emit(i)
            return carry

        jax.lax.fori_loop(0, nq, redo, 0)


def _attn_call(qt, k, vt, gm):
    B, H, S, _ = k.shape
    tq, G = ATT_TQ, ATT_G
    assert ATT_TQ == ATT_TK and ATT_TK % KT == 0
    seq = lambda a: pl.BlockSpec((1, G) + a.shape[2:], lambda b, h: (b, h) + (0,) * (a.ndim - 2))
    tok = pl.BlockSpec((1, S, G * V_DIM), lambda b, h: (b, 0, h))
    return pl.pallas_call(
        _attn_kernel,
        grid=(B, H // G),
        in_specs=[seq(qt), seq(k), seq(vt), tok],
        out_specs=tok,
        out_shape=jax.ShapeDtypeStruct((B, S, D_MLA), jnp.bfloat16),
        scratch_shapes=[
            pltpu.VMEM((G, HEAD_W, tq), jnp.bfloat16),
            pltpu.VMEM((G, ATT_TK, tq), jnp.float32),
            pltpu.VMEM((2, G, ATT_TK, tq), jnp.bfloat16),
            pltpu.VMEM((G, SUBLANES, tq), jnp.float32),
            pltpu.VMEM((G, V_DIM, tq), jnp.float32),
            pltpu.VMEM((G, 1, tq), jnp.float32),
            pltpu.VMEM((G, 1, tq), jnp.float32),
        ],
        compiler_params=pltpu.CompilerParams(
            dimension_semantics=("arbitrary", "arbitrary"),
            vmem_limit_bytes=VMEM_LIMIT_BYTES),
        name="attn",
    )(qt, k, vt, gm)


def _out_kernel(o_ref, pc_ref, h_ref, w_ref, b_ref, g_ref, beta_ref, y_ref):
    for r in range(0, OUT_TM, OUT_CH):
        rows = slice(r, r + OUT_CH)
        mix = jnp.concatenate([o_ref[0, rows, :], pc_ref[0, rows, :]], axis=1)
        y = _dot(mix, w_ref[0]) + b_ref[0] + DEEPNORM_ALPHA * h_ref[0, rows, :]
        y_ref[0, rows, :] = _layernorm(y, g_ref[0], beta_ref[0])


def _out_call(layer, o, pc, h, w_out, b_out, ln_g, ln_b):
    B, S, D = h.shape
    tm = OUT_TM
    tok = lambda w: pl.BlockSpec((1, tm, w), lambda b, i: (b, i, 0))
    return pl.pallas_call(
        _out_kernel,
        grid=(B, S // tm),
        in_specs=[tok(D_MLA), tok(D_POOL + D_CONV), tok(D),
                  _layer_spec(w_out, layer), _layer_spec(b_out, layer), _layer_spec(ln_g, layer),
                  _layer_spec(ln_b, layer)],
        out_specs=tok(D),
        out_shape=jax.ShapeDtypeStruct((B, S, D), jnp.float32),
        compiler_params=pltpu.CompilerParams(
            dimension_semantics=("arbitrary", "arbitrary"),
            vmem_limit_bytes=VMEM_LIMIT_BYTES),
        name="out",
    )(o, pc, h, w_out, b_out, ln_g, ln_b)


def _rotate_half_cols(w):
    return jnp.concatenate([-w[..., HALF:], w[..., :HALF]], axis=-1)


def _wprep_lat_kernel(wt_ref, o_ref):
    x = wt_ref[0]
    kr = C_KR
    rot = jnp.concatenate([-x[kr + HALF:kr + ROPE], x[kr:kr + HALF]], axis=0)
    o_ref[0] = jnp.concatenate([x, rot], axis=0).T.astype(jnp.bfloat16)


def _wprep_mix_kernel(wt_ref, o_ref):
    o_ref[0] = wt_ref[0].T.astype(jnp.bfloat16)


def _prep_w_in(w):
    L, D, C = w.shape
    wt = jnp.swapaxes(w, 1, 2)
    n_lat = C_KR + ROPE
    params = pltpu.CompilerParams(vmem_limit_bytes=VMEM_LIMIT_BYTES)
    w_lat = pl.pallas_call(
        _wprep_lat_kernel,
        grid=(L,),
        in_specs=[pl.BlockSpec((pl.Element(1), pl.Element(n_lat), pl.Element(D)), lambda l: (l, 0, 0))],
        out_specs=pl.BlockSpec((1, D, C_LAT_END), lambda l: (l, 0, 0)),
        out_shape=jax.ShapeDtypeStruct((L, D, C_LAT_END), jnp.bfloat16),
        compiler_params=params,
        name="wprep_lat",
    )(wt)
    w_mix = pl.pallas_call(
        _wprep_mix_kernel,
        grid=(L, R_END // WPREP_COLS),
        in_specs=[pl.BlockSpec((pl.Element(1), pl.Element(WPREP_COLS), pl.Element(D)),
                               lambda l, j: (l, pl.multiple_of(n_lat + WPREP_COLS * j, SUBLANES), 0))],
        out_specs=pl.BlockSpec((1, D, WPREP_COLS), lambda l, j: (l, 0, j)),
        out_shape=jax.ShapeDtypeStruct((L, D, R_END), jnp.bfloat16),
        compiler_params=params,
        name="wprep_mix",
    )(wt)
    return w_lat, w_mix


def _prep_w_uq_t(w):
    return jnp.swapaxes(w, 1, 2).astype(jnp.bfloat16)


def _prep_w_ukv(w):
    L = w.shape[0]
    w = w.reshape(L, KV_LORA, N_HEADS, NOPE + V_DIM)
    wk = w[..., :NOPE].reshape(L, KV_LORA, N_HEADS * NOPE)
    wv = w[..., NOPE:].reshape(L, KV_LORA, N_HEADS * V_DIM)
    return wk.astype(jnp.bfloat16), jnp.swapaxes(wv, 1, 2).astype(jnp.bfloat16)


def kernel(x, positions, emb_ln_g, emb_ln_b, w_in, q_norm_g, kv_norm_g, w_uq, w_ukv, w_pool,
           pool_scale, conv_w, w_out, b_out, ln_g, ln_b):
    B, S, D = x.shape
    bf = jnp.bfloat16
    posr = positions.reshape(B, 1, S)
    inv_freq = ROPE_THETA ** (-jnp.arange(HALF, dtype=jnp.float32) / HALF)
    invfc = inv_freq.reshape(HALF, 1)
    row = lambda a: a.reshape(1, -1)

    w_lat, w_mix = _prep_w_in(w_in)
    w_uk, w_uvt = _prep_w_ukv(w_ukv)
    row3 = lambda a: a.reshape(a.shape[0], 1, -1)
    stacked = (w_lat, w_mix, row3(q_norm_g), row3(kv_norm_g), _prep_w_uq_t(w_uq), w_uk, w_uvt,
               w_pool.astype(bf), row3(pool_scale), conv_w)
    w_out_b = w_out.astype(bf)

    h = x
    for l in range(DEPTH):
        outs = _proj_call(l, h, posr, row(emb_ln_g), row(emb_ln_b), invfc, stacked)
        if l == 0:
            qt, k, vt, gm, pc, h = outs
        else:
            qt, k, vt, gm, pc = outs
        o = _attn_call(qt, k, vt, gm)
        h = _out_call(l, o, pc, h, w_out_b, row3(b_out), row3(ln_g), row3(ln_b))
    return h
```

```python
import functools
import math

import jax
import jax.numpy as jnp
import numpy as np
from jax.experimental import pallas as pl
from jax.experimental.pallas import tpu as pltpu

D_MODEL = 2048
DEPTH = 2
N_HEADS = 8
NOPE = 128
ROPE = 64
V_DIM = 128
Q_LORA = 512
KV_LORA = 256
D_MLA = N_HEADS * V_DIM
ROPE_THETA = 10000.0
POOL_WINDOWS = (2, 4, 8, 16)
POOL_GROUP = 128
D_POOL = 512
D_CONV = 512
CONV_WIDTH = 3
LN_EPS = 1e-5
RMS_EPS = 1e-6
DEEPNORM_ALPHA = (2 * DEPTH) ** 0.25

LANES = 128
SUBLANES = 8
VMEM_LIMIT_BYTES = 56 * 1024 * 1024

PROJ_TM = 256
ATT_TQ = 512
ATT_TK = 512
ATT_G = 2
ATT_CH = 64
WPREP_COLS = 1024
OUT_TM = 512
OUT_CH = 128
POOL_HALO = 16
CONV_HALO = 8

KT = PROJ_TM
HEAD_W = 2 * LANES
HALF = ROPE // 2

C_QLAT = 0
C_KVLAT = C_QLAT + Q_LORA
C_KR = C_KVLAT + KV_LORA
C_LAT_END = C_KR + LANES
R_GMLA = 0
R_PIN = R_GMLA + D_MLA
R_GPOOL = R_PIN + D_POOL
R_CH = R_GPOOL + D_POOL
R_END = R_CH + 4 * D_CONV

Q_PRESCALE = (NOPE + ROPE) ** -0.5 * math.log2(math.e)
M_INIT = -1e30
BIAS_ROW0 = NOPE + ROPE
BIAS_ROWS = 16
SHIFT_L_MIN = 2.0 ** -100
SHIFT_L_MAX = 2.0 ** 100


def _silu(g):
    return g * (1.0 / (1.0 + jnp.exp(-g)))


def _layernorm(x, g, b):
    mu = jnp.mean(x, axis=-1, keepdims=True)
    xc = x - mu
    var = jnp.mean(xc * xc, axis=-1, keepdims=True)
    return xc * jax.lax.rsqrt(var + LN_EPS) * g + b


def _rmsnorm(x, g):
    return x * jax.lax.rsqrt(jnp.mean(x * x, axis=-1, keepdims=True) + RMS_EPS) * g


def _dot(a, b):
    return jnp.dot(a, b, preferred_element_type=jnp.float32)


def _dot_nt(a, b):
    return jax.lax.dot_general(a, b, (((1,), (1,)), ((), ())), preferred_element_type=jnp.float32)


def _proj_kernel(first, *refs):
    if first:
        (x_ref, posr_ref, eg_ref, eb_ref, w_lat_ref, w_mix_ref, qg_ref, kvg_ref, w_uqt_ref, w_uk_ref,
         w_uvt_ref, w_pool_ref, pscale_ref, convw_ref, invfc_ref,
         qt_ref, k_ref, vt_ref, gm_ref, pc_ref, h_ref, pool_ext, conv_ext) = refs
    else:
        (x_ref, posr_ref, w_lat_ref, w_mix_ref, qg_ref, kvg_ref, w_uqt_ref, w_uk_ref,
         w_uvt_ref, w_pool_ref, pscale_ref, convw_ref, invfc_ref,
         qt_ref, k_ref, vt_ref, gm_ref, pc_ref, pool_ext, conv_ext) = refs
    tm = PROJ_TM
    i = pl.program_id(1)

    @pl.when(i == 0)
    def _():
        pool_ext[0:POOL_HALO, :] = jnp.zeros((POOL_HALO, D_POOL), jnp.float32)
        conv_ext[0:CONV_HALO, :] = jnp.zeros((CONV_HALO, D_CONV), jnp.float32)

    @pl.when(i > 0)
    def _():
        pool_ext[0:POOL_HALO, :] = pool_ext[tm:tm + POOL_HALO, :]
        conv_ext[0:CONV_HALO, :] = conv_ext[tm:tm + CONV_HALO, :]

    x = x_ref[0]
    if first:
        x = _layernorm(x, eg_ref[...], eb_ref[...])
        h_ref[0] = x
    hb = x.astype(jnp.bfloat16)

    lat = _dot(hb, w_lat_ref[0])
    pin = _dot(hb, w_mix_ref[0, :, R_PIN:R_GPOOL])
    cv = _dot(hb, w_mix_ref[0, :, R_CH:R_END])

    qn = _rmsnorm(lat[:, C_QLAT:C_KVLAT], qg_ref[0]).astype(jnp.bfloat16)
    kvn = _rmsnorm(lat[:, C_KVLAT:C_KR], kvg_ref[0]).astype(jnp.bfloat16)
    kr = lat[:, C_KR:C_LAT_END]
    angt = invfc_ref[...] * posr_ref[0].astype(jnp.float32)
    cos_t, sin_t = jnp.cos(angt), jnp.sin(angt)
    cs = jnp.concatenate([cos_t, cos_t, sin_t, sin_t], axis=0).T
    lane = jax.lax.broadcasted_iota(jnp.int32, (tm, LANES), 1)
    t = kr * cs
    ones = jnp.where(lane < ROPE + BIAS_ROWS, 1.0, 0.0)
    kroped = jnp.where(lane < ROPE, t + pltpu.roll(t, ROPE, axis=1), ones).astype(jnp.bfloat16)

    kn = _dot(kvn, w_uk_ref[0])
    vt = _dot_nt(w_uvt_ref[0], kvn)
    qt = _dot_nt(w_uqt_ref[0], qn)
    gpool = _dot(hb, w_mix_ref[0, :, R_GPOOL:R_CH])

    for hh in range(N_HEADS):
        k_ref[0, hh] = jnp.concatenate(
            [kn[:, hh * NOPE:(hh + 1) * NOPE].astype(jnp.bfloat16), kroped], axis=1)
        vt_ref[0, hh, 0] = vt[hh * V_DIM:(hh + 1) * V_DIM, :].astype(jnp.bfloat16)

    u = cv[:, 2 * D_CONV:3 * D_CONV] * cv[:, 0:D_CONV]
    conv_ext[CONV_HALO:CONV_HALO + tm, :] = u
    yc = convw_ref[0, 2:3, :] * u
    yc = yc + convw_ref[0, 1:2, :] * conv_ext[CONV_HALO - 1:CONV_HALO - 1 + tm, :]
    yc = yc + convw_ref[0, 0:1, :] * conv_ext[CONV_HALO - 2:CONV_HALO - 2 + tm, :]
    y_conv = cv[:, D_CONV:2 * D_CONV] * yc * _silu(cv[:, 3 * D_CONV:4 * D_CONV])
    pc_ref[0, :, D_POOL:] = y_conv.astype(jnp.bfloat16)

    pool_ext[POOL_HALO:POOL_HALO + tm, :] = pin
    t1 = (i * tm + 1 + jax.lax.broadcasted_iota(jnp.int32, (tm, 1), 0)).astype(jnp.float32)
    ys = []
    for g, w in enumerate(POOL_WINDOWS):
        lo = g * POOL_GROUP
        acc = pin[:, lo:lo + POOL_GROUP]
        for s in range(1, w):
            acc = acc + pool_ext[POOL_HALO - s:POOL_HALO - s + tm, lo:lo + POOL_GROUP]
        pooled = acc / jnp.minimum(t1, float(w)) - pin[:, lo:lo + POOL_GROUP]
        ys.append(_dot(pooled.astype(jnp.bfloat16), w_pool_ref[0, g]))
    gm = _dot(hb, w_mix_ref[0, :, R_GMLA:R_PIN])
    y_pool = jnp.concatenate(ys, axis=1) * pscale_ref[0] * _silu(gpool)
    pc_ref[0, :, :D_POOL] = y_pool.astype(jnp.bfloat16)

    cos_t = jnp.concatenate([cos_t, cos_t], axis=0)
    sin_t = jnp.concatenate([sin_t, sin_t], axis=0)
    zeros_t = jnp.zeros((HEAD_W - NOPE - ROPE, tm), jnp.float32)
    for hh in range(N_HEADS):
        r0 = hh * HEAD_W
        roped = qt[r0 + NOPE:r0 + NOPE + ROPE, :] * cos_t + qt[r0 + NOPE + ROPE:r0 + HEAD_W, :] * sin_t
        qh = jnp.concatenate([qt[r0:r0 + NOPE, :], roped, zeros_t], axis=0)
        qt_ref[0, hh, 0] = (qh * Q_PRESCALE).astype(jnp.bfloat16)

    gm_ref[0] = _silu(gm).astype(jnp.bfloat16)


def _const_spec(shape):
    nd = len(shape)
    return pl.BlockSpec(shape, lambda *_: (0,) * nd, pipeline_mode=pl.Buffered(1))


def _layer_spec(arr, layer):
    nd = arr.ndim
    return pl.BlockSpec((1,) + arr.shape[1:], lambda *_: (layer,) + (0,) * (nd - 1),
                        pipeline_mode=pl.Buffered(1))


def _proj_call(layer, x, posr, emb_g, emb_b, inv_freq_col, stacked):
    first = layer == 0
    B, S, D = x.shape
    tm = PROJ_TM
    grid = (B, S // tm)
    qsub = ATT_TQ // tm
    tok = lambda w: pl.BlockSpec((1, tm, w), lambda b, i: (b, i, 0))
    in_specs = [tok(D), pl.BlockSpec((1, 1, tm), lambda b, i: (b, 0, i))]
    args = [x, posr]
    if first:
        in_specs += [_const_spec((1, D)), _const_spec((1, D))]
        args += [emb_g, emb_b]
    in_specs += [_layer_spec(a, layer) for a in stacked] + [_const_spec(inv_freq_col.shape)]
    args += list(stacked) + [inv_freq_col]
    bf = jnp.bfloat16
    out_shape = [
        jax.ShapeDtypeStruct((B, N_HEADS, S // ATT_TQ, HEAD_W, ATT_TQ), bf),
        jax.ShapeDtypeStruct((B, N_HEADS, S, HEAD_W), bf),
        jax.ShapeDtypeStruct((B, N_HEADS, S // KT, V_DIM, KT), bf),
        jax.ShapeDtypeStruct((B, S, D_MLA), bf),
        jax.ShapeDtypeStruct((B, S, D_POOL + D_CONV), bf),
    ]
    out_specs = [
        pl.BlockSpec((1, N_HEADS, 1, HEAD_W, tm), lambda b, i: (b, 0, i // qsub, 0, i % qsub)),
        pl.BlockSpec((1, N_HEADS, tm, HEAD_W), lambda b, i: (b, 0, i, 0)),
        pl.BlockSpec((1, N_HEADS, 1, V_DIM, KT), lambda b, i: (b, 0, i, 0, 0)),
        tok(D_MLA), tok(D_POOL + D_CONV)]
    if first:
        out_shape.append(jax.ShapeDtypeStruct((B, S, D), jnp.float32))
        out_specs.append(tok(D))
    return pl.pallas_call(
        functools.partial(_proj_kernel, first),
        grid=grid,
        in_specs=in_specs,
        out_specs=out_specs,
        out_shape=out_shape,
        scratch_shapes=[
            pltpu.VMEM((POOL_HALO + tm, D_POOL), jnp.float32),
            pltpu.VMEM((CONV_HALO + tm, D_CONV), jnp.float32),
        ],
        compiler_params=pltpu.CompilerParams(
            dimension_semantics=("arbitrary", "arbitrary"),
            vmem_limit_bytes=VMEM_LIMIT_BYTES),
        name="proj_first" if first else "proj",
    )(*args)


def _rows8_reduce(x, op):
    acc = x[0:SUBLANES]
    for r in range(SUBLANES, x.shape[0], SUBLANES):
        acc = op(acc, x[r:r + SUBLANES])
    return acc


def _attn_kernel(qt_ref, k_ref, vt_ref, gm_ref, o_ref,
                 qb_sc, s_sc, p_sc, l8_sc, acc_sc, m_sc, l_sc):
    tq, tk, G, ch = ATT_TQ, ATT_TK, ATT_G, ATT_CH
    sub, nch, nqb = tk // KT, tk // ch, tq // LANES
    per_lane_block = LANES // ch
    nq = qt_ref.shape[2]

    def vt_tile(g, n):
        return jnp.concatenate([vt_ref[0, g, n * sub + c] for c in range(sub)], axis=1)

    def k_tile(g, n):
        return k_ref[0, g, pl.ds(pl.multiple_of(n * tk, tk), tk), :]

    def rows(c):
        return slice(c * ch, (c + 1) * ch)

    def lanes(j):
        return slice(j * LANES, (j + 1) * LANES)

    def exp_chunks(g, chunk_of, buf, lsum):
        for c in range(nch):
            p_c = jnp.exp2(chunk_of(c))
            lsum = lsum + _rows8_reduce(p_c, jnp.add)
            p_sc[buf, g, rows(c), :] = p_c.astype(jnp.bfloat16)
        return lsum

    def qk_exp(g, buf, n):
        s = _dot(k_tile(g, n), qb_sc[g])
        l8_sc[g] = exp_chunks(g, lambda c: s[rows(c)], buf, l8_sc[g])

    def pv(g, buf, n):
        acc_sc[g] = acc_sc[g] + _dot(vt_tile(g, n), p_sc[buf, g])

    def emit(i):
        q0 = pl.multiple_of(i * tq, tq)
        for g in range(G):
            out_t = acc_sc[g] / l_sc[g]
            gate = gm_ref[0, pl.ds(q0, tq), g * V_DIM:(g + 1) * V_DIM].astype(jnp.float32)
            o_ref[0, pl.ds(q0, tq), g * V_DIM:(g + 1) * V_DIM] = (out_t.T * gate).astype(o_ref.dtype)

    def finish(i):
        last = jnp.maximum(i - 1, 0)
        in_buf1 = jnp.logical_or(i == 0, last % 2 == 1)
        for g in range(G):
            p_last = jnp.where(in_buf1, p_sc[1, g], p_sc[0, g])
            acc_sc[g] = acc_sc[g] + _dot(vt_tile(g, last), p_last)
        emit(i)

    def diagonal(i):
        for g in range(G):
            s_sc[g] = _dot(k_tile(g, i), qt_ref[0, g, i])
            qb_sc[g] = qt_ref[0, g, i]

        def block(g, c, j):
            s_b = s_sc[g, rows(c), lanes(j)]
            if c < per_lane_block * j:
                return s_b
            key = c * ch + jax.lax.broadcasted_iota(jnp.int32, (ch, LANES), 0)
            qry = j * LANES + jax.lax.broadcasted_iota(jnp.int32, (ch, LANES), 1)
            return jnp.where(key <= qry, s_b, -jnp.inf)

        for g in range(G):
            for j in range(nqb):
                visible = range(per_lane_block * (j + 1))
                m8 = _rows8_reduce(block(g, 0, j), jnp.maximum)
                for c in visible[1:]:
                    m8 = jnp.maximum(m8, _rows8_reduce(block(g, c, j), jnp.maximum))
                m_d = jnp.max(m8, axis=0, keepdims=True)
                m_ref = m_d.astype(jnp.bfloat16).astype(jnp.float32)
                lsum = jnp.zeros((SUBLANES, LANES), jnp.float32)
                for c in range(nch):
                    if c in visible:
                        p_b = jnp.exp2(block(g, c, j) - m_ref)
                        lsum = lsum + _rows8_reduce(p_b, jnp.add)
                        p_sc[1, g, rows(c), lanes(j)] = p_b.astype(jnp.bfloat16)
                    else:
                        p_sc[1, g, rows(c), lanes(j)] = jnp.zeros((ch, LANES), jnp.bfloat16)
                l8_sc[g, :, lanes(j)] = lsum
                bias = jnp.broadcast_to(m_ref * (-1.0 / BIAS_ROWS), (BIAS_ROWS, LANES))
                qb_sc[g, BIAS_ROW0:BIAS_ROW0 + BIAS_ROWS, lanes(j)] = bias.astype(jnp.bfloat16)
        acc_sc[...] = jnp.zeros(acc_sc.shape, jnp.float32)

    def slow_path(i):
        m_sc[...] = jnp.full(m_sc.shape, M_INIT, jnp.float32)
        l_sc[...] = jnp.zeros(l_sc.shape, jnp.float32)
        acc_sc[...] = jnp.zeros(acc_sc.shape, jnp.float32)

        def chunk(g, c, masked):
            s_c = s_sc[g, rows(c), :]
            if not masked:
                return s_c
            key = c * ch + jax.lax.broadcasted_iota(jnp.int32, (ch, tq), 0)
            return jnp.where(key <= jax.lax.broadcasted_iota(jnp.int32, (ch, tq), 1), s_c, -jnp.inf)

        def tile(n, masked):
            for g in range(G):
                s_sc[g] = _dot(k_tile(g, n), qt_ref[0, g, i])
                m8 = _rows8_reduce(chunk(g, 0, masked), jnp.maximum)
                for c in range(1, nch):
                    m8 = jnp.maximum(m8, _rows8_reduce(chunk(g, c, masked), jnp.maximum))
                m_prev = m_sc[g]
                m_new = jnp.maximum(m_prev, jnp.max(m8, axis=0, keepdims=True))
                alpha = jnp.exp2(m_prev - m_new)
                lsum = exp_chunks(g, lambda c: chunk(g, c, masked) - m_new, 0,
                                  jnp.zeros((SUBLANES, tq), jnp.float32))
                l_sc[g] = alpha * l_sc[g] + jnp.sum(lsum, axis=0, keepdims=True)
                acc_sc[g] = alpha * acc_sc[g] + _dot(vt_tile(g, n), p_sc[0, g])
                m_sc[g] = m_new

        def full(n, carry):
            tile(n, False)
            return carry

        jax.lax.fori_loop(0, i, full, 0)
        tile(i, True)

    def denominators(l_range):
        l_lo, l_hi = l_range
        for g in range(G):
            l = jnp.sum(l8_sc[g], axis=0, keepdims=True)
            l_sc[g] = l
            l_lo = jnp.minimum(l_lo, jnp.min(l))
            l_hi = jnp.maximum(l_hi, jnp.max(l))
        return l_lo, l_hi

    def query_tile(i, l_range):
        l_range = denominators(l_range)
        finish(jnp.maximum(i - 1, 0))
        diagonal(i)

        def pair(t, c):
            n = 2 * t
            for g in range(G):
                qk_exp(g, 0, n)
                pv(g, 1, jnp.where(t == 0, i, n - 1))
            for g in range(G):
                qk_exp(g, 1, n + 1)
                pv(g, 0, n)
            return c

        jax.lax.fori_loop(0, i // 2, pair, 0)

        @pl.when(i % 2 == 1)
        def _():
            for g in range(G):
                qk_exp(g, 0, i - 1)
                pv(g, 1, jnp.where(i == 1, i, i - 2))

        return l_range

    acc_sc[...] = jnp.zeros(acc_sc.shape, jnp.float32)
    l8_sc[...] = jnp.ones(l8_sc.shape, jnp.float32)
    p_sc[1] = jnp.zeros(p_sc.shape[1:], jnp.bfloat16)
    l_range = jax.lax.fori_loop(0, nq, query_tile, (jnp.float32(jnp.inf), jnp.float32(0.0)))
    l_lo, l_hi = denominators(l_range)
    finish(nq - 1)

    in_range = jnp.logical_and(l_lo > SHIFT_L_MIN, l_hi < SHIFT_L_MAX)

    @pl.when(jnp.logical_not(in_range))
    def _():
        def redo(i, carry):
            slow_path(i)
            emit(i)
            return carry

        jax.lax.fori_loop(0, nq, redo, 0)


def _attn_call(qt, k, vt, gm):
    B, H, S, _ = k.shape
    tq, G = ATT_TQ, ATT_G
    assert ATT_TQ == ATT_TK and ATT_TK % KT == 0
    seq = lambda a: pl.BlockSpec((1, G) + a.shape[2:], lambda b, h: (b, h) + (0,) * (a.ndim - 2))
    tok = pl.BlockSpec((1, S, G * V_DIM), lambda b, h: (b, 0, h))
    return pl.pallas_call(
        _attn_kernel,
        grid=(B, H // G),
        in_specs=[seq(qt), seq(k), seq(vt), tok],
        out_specs=tok,
        out_shape=jax.ShapeDtypeStruct((B, S, D_MLA), jnp.bfloat16),
        scratch_shapes=[
            pltpu.VMEM((G, HEAD_W, tq), jnp.bfloat16),
            pltpu.VMEM((G, ATT_TK, tq), jnp.float32),
            pltpu.VMEM((2, G, ATT_TK, tq), jnp.bfloat16),
            pltpu.VMEM((G, SUBLANES, tq), jnp.float32),
            pltpu.VMEM((G, V_DIM, tq), jnp.float32),
            pltpu.VMEM((G, 1, tq), jnp.float32),
            pltpu.VMEM((G, 1, tq), jnp.float32),
        ],
        compiler_params=pltpu.CompilerParams(
            dimension_semantics=("arbitrary", "arbitrary"),
            vmem_limit_bytes=VMEM_LIMIT_BYTES),
        name="attn",
    )(qt, k, vt, gm)


def _out_kernel(o_ref, pc_ref, h_ref, w_ref, b_ref, g_ref, beta_ref, y_ref):
    for r in range(0, OUT_TM, OUT_CH):
        rows = slice(r, r + OUT_CH)
        mix = jnp.concatenate([o_ref[0, rows, :], pc_ref[0, rows, :]], axis=1)
        y = _dot(mix, w_ref[0]) + b_ref[0] + DEEPNORM_ALPHA * h_ref[0, rows, :]
        y_ref[0, rows, :] = _layernorm(y, g_ref[0], beta_ref[0])


def _out_call(layer, o, pc, h, w_out, b_out, ln_g, ln_b):
    B, S, D = h.shape
    tm = OUT_TM
    tok = lambda w: pl.BlockSpec((1, tm, w), lambda b, i: (b, i, 0))
    return pl.pallas_call(
        _out_kernel,
        grid=(B, S // tm),
        in_specs=[tok(D_MLA), tok(D_POOL + D_CONV), tok(D),
                  _layer_spec(w_out, layer), _layer_spec(b_out, layer), _layer_spec(ln_g, layer),
                  _layer_spec(ln_b, layer)],
        out_specs=tok(D),
        out_shape=jax.ShapeDtypeStruct((B, S, D), jnp.float32),
        compiler_params=pltpu.CompilerParams(
            dimension_semantics=("arbitrary", "arbitrary"),
            vmem_limit_bytes=VMEM_LIMIT_BYTES),
        name="out",
    )(o, pc, h, w_out, b_out, ln_g, ln_b)


def _rotate_half_cols(w):
    return jnp.concatenate([-w[..., HALF:], w[..., :HALF]], axis=-1)


def _wprep_lat_kernel(wt_ref, o_ref):
    x = wt_ref[0]
    kr = C_KR
    rot = jnp.concatenate([-x[kr + HALF:kr + ROPE], x[kr:kr + HALF]], axis=0)
    o_ref[0] = jnp.concatenate([x, rot], axis=0).T.astype(jnp.bfloat16)


def _wprep_mix_kernel(wt_ref, o_ref):
    o_ref[0] = wt_ref[0].T.astype(jnp.bfloat16)


def _prep_w_in(w):
    L, D, C = w.shape
    wt = jnp.swapaxes(w, 1, 2)
    n_lat = C_KR + ROPE
    params = pltpu.CompilerParams(vmem_limit_bytes=VMEM_LIMIT_BYTES)
    w_lat = pl.pallas_call(
        _wprep_lat_kernel,
        grid=(L,),
        in_specs=[pl.BlockSpec((pl.Element(1), pl.Element(n_lat), pl.Element(D)), lambda l: (l, 0, 0))],
        out_specs=pl.BlockSpec((1, D, C_LAT_END), lambda l: (l, 0, 0)),
        out_shape=jax.ShapeDtypeStruct((L, D, C_LAT_END), jnp.bfloat16),
        compiler_params=params,
        name="wprep_lat",
    )(wt)
    w_mix = pl.pallas_call(
        _wprep_mix_kernel,
        grid=(L, R_END // WPREP_COLS),
        in_specs=[pl.BlockSpec((pl.Element(1), pl.Element(WPREP_COLS), pl.Element(D)),
                               lambda l, j: (l, pl.multiple_of(n_lat + WPREP_COLS * j, SUBLANES), 0))],
        out_specs=pl.BlockSpec((1, D, WPREP_COLS), lambda l, j: (l, 0, j)),
        out_shape=jax.ShapeDtypeStruct((L, D, R_END), jnp.bfloat16),
        compiler_params=params,
        name="wprep_mix",
    )(wt)
    return w_lat, w_mix


def _prep_w_uq_t(w):
    L = w.shape[0]
    w = w.reshape(L, Q_LORA, N_HEADS, NOPE + ROPE)
    w = jnp.concatenate([w, _rotate_half_cols(w[..., NOPE:])], axis=-1)
    return jnp.swapaxes(w.reshape(L, Q_LORA, N_HEADS * HEAD_W), 1, 2).astype(jnp.bfloat16)


def _prep_w_ukv(w):
    L = w.shape[0]
    w = w.reshape(L, KV_LORA, N_HEADS, NOPE + V_DIM)
    wk = w[..., :NOPE].reshape(L, KV_LORA, N_HEADS * NOPE)
    wv = w[..., NOPE:].reshape(L, KV_LORA, N_HEADS * V_DIM)
    return wk.astype(jnp.bfloat16), jnp.swapaxes(wv, 1, 2).astype(jnp.bfloat16)


def kernel(x, positions, emb_ln_g, emb_ln_b, w_in, q_norm_g, kv_norm_g, w_uq, w_ukv, w_pool,
           pool_scale, conv_w, w_out, b_out, ln_g, ln_b):
    B, S, D = x.shape
    bf = jnp.bfloat16
    posr = positions.reshape(B, 1, S)
    inv_freq = ROPE_THETA ** (-jnp.arange(HALF, dtype=jnp.float32) / HALF)
    invfc = inv_freq.reshape(HALF, 1)
    row = lambda a: a.reshape(1, -1)

    w_lat, w_mix = _prep_w_in(w_in)
    w_uk, w_uvt = _prep_w_ukv(w_ukv)
    row3 = lambda a: a.reshape(a.shape[0], 1, -1)
    stacked = (w_lat, w_mix, row3(q_norm_g), row3(kv_norm_g), _prep_w_uq_t(w_uq), w_uk, w_uvt,
               w_pool.astype(bf), row3(pool_scale), conv_w)
    w_out_b = w_out.astype(bf)

    h = x
    for l in range(DEPTH):
        outs = _proj_call(l, h, posr, row(emb_ln_g), row(emb_ln_b), invfc, stacked)
        if l == 0:
            qt, k, vt, gm, pc, h = outs
        else:
            qt, k, vt, gm, pc = outs
        o = _attn_call(qt, k, vt, gm)
        h = _out_call(l, o, pc, h, w_out_b, row3(b_out), row3(ln_g), row3(ln_b))
    return h
```

```python
import functools
import math

import jax
import jax.numpy as jnp
from jax.experimental import pallas as pl
from jax.experimental.pallas import tpu as pltpu

D_MODEL = 2048
DEPTH = 2
N_HEADS = 8
NOPE = 128
ROPE = 64
V_DIM = 128
Q_LORA = 512
KV_LORA = 256
D_MLA = N_HEADS * V_DIM
ROPE_THETA = 10000.0
POOL_WINDOWS = (2, 4, 8, 16)
POOL_GROUP = 128
D_POOL = 512
D_CONV = 512
CONV_WIDTH = 3
LN_EPS = 1e-5
RMS_EPS = 1e-6
DEEPNORM_ALPHA = (2 * DEPTH) ** 0.25

LANES = 128
SUBLANES = 8
VMEM_LIMIT_BYTES = 56 * 1024 * 1024

PROJ_TM = 256
ATT_TQ = 512
ATT_TK = 512
ATT_G = 2
ATT_CH = 64
WPREP_COLS = 1024
OUT_TM = 512
OUT_CH = 128
POOL_HALO = 16
CONV_HALO = 8

KT = PROJ_TM
HEAD_W = 2 * LANES
HALF = ROPE // 2

C_QLAT = 0
C_KVLAT = C_QLAT + Q_LORA
C_KR = C_KVLAT + KV_LORA
C_LAT_END = C_KR + LANES
R_GMLA = 0
R_PIN = R_GMLA + D_MLA
R_GPOOL = R_PIN + D_POOL
R_CH = R_GPOOL + D_POOL
R_END = R_CH + 4 * D_CONV

Q_PRESCALE = (NOPE + ROPE) ** -0.5 * math.log2(math.e)
M_INIT = -1e30
BIAS_ROW0 = NOPE + ROPE
BIAS_ROWS = 16
SHIFT_L_MIN = 2.0 ** -100
SHIFT_L_MAX = 2.0 ** 100


def _silu(g):
    return g * (1.0 / (1.0 + jnp.exp(-g)))


def _layernorm(x, g, b):
    mu = jnp.mean(x, axis=-1, keepdims=True)
    xc = x - mu
    var = jnp.mean(xc * xc, axis=-1, keepdims=True)
    return xc * jax.lax.rsqrt(var + LN_EPS) * g + b


def _rmsnorm(x, g):
    return x * jax.lax.rsqrt(jnp.mean(x * x, axis=-1, keepdims=True) + RMS_EPS) * g


def _dot(a, b):
    return jnp.dot(a, b, preferred_element_type=jnp.float32)


def _dot_nt(a, b):
    return jax.lax.dot_general(a, b, (((1,), (1,)), ((), ())), preferred_element_type=jnp.float32)


def _proj_kernel(first, *refs):
    if first:
        (x_ref, posr_ref, eg_ref, eb_ref, w_lat_ref, w_mix_ref, qg_ref, kvg_ref, w_uqt_ref, w_uk_ref,
         w_uvt_ref, w_pool_ref, pscale_ref, convw_ref, invfc_ref,
         qt_ref, k_ref, vt_ref, gm_ref, pc_ref, h_ref, pool_ext, conv_ext) = refs
    else:
        (x_ref, posr_ref, w_lat_ref, w_mix_ref, qg_ref, kvg_ref, w_uqt_ref, w_uk_ref,
         w_uvt_ref, w_pool_ref, pscale_ref, convw_ref, invfc_ref,
         qt_ref, k_ref, vt_ref, gm_ref, pc_ref, pool_ext, conv_ext) = refs
    tm = PROJ_TM
    i = pl.program_id(1)

    @pl.when(i == 0)
    def _():
        pool_ext[0:POOL_HALO, :] = jnp.zeros((POOL_HALO, D_POOL), jnp.float32)
        conv_ext[0:CONV_HALO, :] = jnp.zeros((CONV_HALO, D_CONV), jnp.float32)

    @pl.when(i > 0)
    def _():
        pool_ext[0:POOL_HALO, :] = pool_ext[tm:tm + POOL_HALO, :]
        conv_ext[0:CONV_HALO, :] = conv_ext[tm:tm + CONV_HALO, :]

    x = x_ref[0]
    if first:
        x = _layernorm(x, eg_ref[...], eb_ref[...])
        h_ref[0] = x
    hb = x.astype(jnp.bfloat16)

    lat = _dot(hb, w_lat_ref[0])
    pin = _dot(hb, w_mix_ref[0, :, R_PIN:R_GPOOL])
    cv = _dot(hb, w_mix_ref[0, :, R_CH:R_END])

    qn = _rmsnorm(lat[:, C_QLAT:C_KVLAT], qg_ref[0]).astype(jnp.bfloat16)
    kvn = _rmsnorm(lat[:, C_KVLAT:C_KR], kvg_ref[0]).astype(jnp.bfloat16)
    kr = lat[:, C_KR:C_LAT_END]
    angt = invfc_ref[...] * posr_ref[0].astype(jnp.float32)
    cos_t, sin_t = jnp.cos(angt), jnp.sin(angt)
    cs = jnp.concatenate([cos_t, cos_t, sin_t, sin_t], axis=0).T
    lane = jax.lax.broadcasted_iota(jnp.int32, (tm, LANES), 1)
    t = kr * cs
    ones = jnp.where(lane < ROPE + BIAS_ROWS, 1.0, 0.0)
    kroped = jnp.where(lane < ROPE, t + pltpu.roll(t, ROPE, axis=1), ones).astype(jnp.bfloat16)

    kn = _dot(kvn, w_uk_ref[0])
    vt = _dot_nt(w_uvt_ref[0], kvn)
    qt = _dot_nt(w_uqt_ref[0], qn)
    gpool = _dot(hb, w_mix_ref[0, :, R_GPOOL:R_CH])

    for hh in range(N_HEADS):
        k_ref[0, hh] = jnp.concatenate(
            [kn[:, hh * NOPE:(hh + 1) * NOPE].astype(jnp.bfloat16), kroped], axis=1)
        vt_ref[0, hh, 0] = vt[hh * V_DIM:(hh + 1) * V_DIM, :].astype(jnp.bfloat16)

    u = cv[:, 2 * D_CONV:3 * D_CONV] * cv[:, 0:D_CONV]
    conv_ext[CONV_HALO:CONV_HALO + tm, :] = u
    yc = convw_ref[0, 2:3, :] * u
    yc = yc + convw_ref[0, 1:2, :] * conv_ext[CONV_HALO - 1:CONV_HALO - 1 + tm, :]
    yc = yc + convw_ref[0, 0:1, :] * conv_ext[CONV_HALO - 2:CONV_HALO - 2 + tm, :]
    y_conv = cv[:, D_CONV:2 * D_CONV] * yc * _silu(cv[:, 3 * D_CONV:4 * D_CONV])
    pc_ref[0, :, D_POOL:] = y_conv.astype(jnp.bfloat16)

    pool_ext[POOL_HALO:POOL_HALO + tm, :] = pin
    t1 = (i * tm + 1 + jax.lax.broadcasted_iota(jnp.int32, (tm, 1), 0)).astype(jnp.float32)
    ys = []
    for g, w in enumerate(POOL_WINDOWS):
        lo = g * POOL_GROUP
        acc = pin[:, lo:lo + POOL_GROUP]
        for s in range(1, w):
            acc = acc + pool_ext[POOL_HALO - s:POOL_HALO - s + tm, lo:lo + POOL_GROUP]
        pooled = acc / jnp.minimum(t1, float(w)) - pin[:, lo:lo + POOL_GROUP]
        ys.append(_dot(pooled.astype(jnp.bfloat16), w_pool_ref[0, g]))
    gm = _dot(hb, w_mix_ref[0, :, R_GMLA:R_PIN])
    y_pool = jnp.concatenate(ys, axis=1) * pscale_ref[0] * _silu(gpool)
    pc_ref[0, :, :D_POOL] = y_pool.astype(jnp.bfloat16)

    cos_t = jnp.concatenate([cos_t, cos_t], axis=0)
    sin_t = jnp.concatenate([sin_t, sin_t], axis=0)
    zeros_t = jnp.zeros((HEAD_W - NOPE - ROPE, tm), jnp.float32)
    for hh in range(N_HEADS):
        r0 = hh * HEAD_W
        roped = qt[r0 + NOPE:r0 + NOPE + ROPE, :] * cos_t + qt[r0 + NOPE + ROPE:r0 + HEAD_W, :] * sin_t
        qh = jnp.concatenate([qt[r0:r0 + NOPE, :], roped, zeros_t], axis=0)
        qt_ref[0, hh, 0] = (qh * Q_PRESCALE).astype(jnp.bfloat16)

    gm_ref[0] = _silu(gm).astype(jnp.bfloat16)


def _const_spec(shape):
    nd = len(shape)
    return pl.BlockSpec(shape, lambda *_: (0,) * nd, pipeline_mode=pl.Buffered(1))


def _layer_spec(arr, layer):
    nd = arr.ndim
    return pl.BlockSpec((1,) + arr.shape[1:], lambda *_: (layer,) + (0,) * (nd - 1),
                        pipeline_mode=pl.Buffered(1))


def _proj_call(layer, x, posr, emb_g, emb_b, inv_freq_col, stacked):
    first = layer == 0
    B, S, D = x.shape
    tm = PROJ_TM
    grid = (B, S // tm)
    qsub = ATT_TQ // tm
    tok = lambda w: pl.BlockSpec((1, tm, w), lambda b, i: (b, i, 0))
    in_specs = [tok(D), pl.BlockSpec((1, 1, tm), lambda b, i: (b, 0, i))]
    args = [x, posr]
    if first:
        in_specs += [_const_spec((1, D)), _const_spec((1, D))]
        args += [emb_g, emb_b]
    in_specs += [_layer_spec(a, layer) for a in stacked] + [_const_spec(inv_freq_col.shape)]
    args += list(stacked) + [inv_freq_col]
    bf = jnp.bfloat16
    out_shape = [
        jax.ShapeDtypeStruct((B, N_HEADS, S // ATT_TQ, HEAD_W, ATT_TQ), bf),
        jax.ShapeDtypeStruct((B, N_HEADS, S, HEAD_W), bf),
        jax.ShapeDtypeStruct((B, N_HEADS, S // KT, V_DIM, KT), bf),
        jax.ShapeDtypeStruct((B, S, D_MLA), bf),
        jax.ShapeDtypeStruct((B, S, D_POOL + D_CONV), bf),
    ]
    out_specs = [
        pl.BlockSpec((1, N_HEADS, 1, HEAD_W, tm), lambda b, i: (b, 0, i // qsub, 0, i % qsub)),
        pl.BlockSpec((1, N_HEADS, tm, HEAD_W), lambda b, i: (b, 0, i, 0)),
        pl.BlockSpec((1, N_HEADS, 1, V_DIM, KT), lambda b, i: (b, 0, i, 0, 0)),
        tok(D_MLA), tok(D_POOL + D_CONV)]
    if first:
        out_shape.append(jax.ShapeDtypeStruct((B, S, D), jnp.float32))
        out_specs.append(tok(D))
    return pl.pallas_call(
        functools.partial(_proj_kernel, first),
        grid=grid,
        in_specs=in_specs,
        out_specs=out_specs,
        out_shape=out_shape,
        scratch_shapes=[
            pltpu.VMEM((POOL_HALO + tm, D_POOL), jnp.float32),
            pltpu.VMEM((CONV_HALO + tm, D_CONV), jnp.float32),
        ],
        compiler_params=pltpu.CompilerParams(
            dimension_semantics=("arbitrary", "arbitrary"),
            vmem_limit_bytes=VMEM_LIMIT_BYTES),
        name="proj_first" if first else "proj",
    )(*args)


def _rows8_reduce(x, op):
    acc = x[0:SUBLANES]
    for r in range(SUBLANES, x.shape[0], SUBLANES):
        acc = op(acc, x[r:r + SUBLANES])
    return acc


def _attn_kernel(qt_ref, k_ref, vt_ref, gm_ref, o_ref,
                 qb_sc, s_sc, p_sc, l8_sc, acc_sc, m_sc, l_sc):
    tq, tk, G, ch = ATT_TQ, ATT_TK, ATT_G, ATT_CH
    sub, nch, nqb = tk // KT, tk // ch, tq // LANES
    per_lane_block = LANES // ch
    nq = qt_ref.shape[2]

    def vt_tile(g, n):
        return jnp.concatenate([vt_ref[0, g, n * sub + c] for c in range(sub)], axis=1)

    def k_tile(g, n):
        start = n * tk if isinstance(n, int) else pl.multiple_of(n * tk, tk)
        return k_ref[0, g, pl.ds(start, tk), :]

    def rows(c):
        return slice(c * ch, (c + 1) * ch)

    def lanes(j):
        return slice(j * LANES, (j + 1) * LANES)

    def exp_chunks(g, chunk_of, buf, lsum):
        for c in range(nch):
            p_c = jnp.exp2(chunk_of(c))
            lsum = lsum + _rows8_reduce(p_c, jnp.add)
            p_sc[buf, g, rows(c), :] = p_c.astype(jnp.bfloat16)
        return lsum

    def qk_exp(g, buf, n):
        s = _dot(k_tile(g, n), qb_sc[g])
        l8_sc[g] = exp_chunks(g, lambda c: s[rows(c)], buf, l8_sc[g])

    def pv(g, buf, n):
        acc_sc[g] = acc_sc[g] + _dot(vt_tile(g, n), p_sc[buf, g])

    def emit(i):
        q0 = i * tq if isinstance(i, int) else pl.multiple_of(i * tq, tq)
        for g in range(G):
            out_t = acc_sc[g] / l_sc[g]
            gate = gm_ref[0, pl.ds(q0, tq), g * V_DIM:(g + 1) * V_DIM].astype(jnp.float32)
            o_ref[0, pl.ds(q0, tq), g * V_DIM:(g + 1) * V_DIM] = (out_t.T * gate).astype(o_ref.dtype)

    def finish(i):
        last = max(i - 1, 0)
        for g in range(G):
            pv(g, 1 if i == 0 else last % 2, last)
        emit(i)

    def diagonal(i):
        for g in range(G):
            s_sc[g] = _dot(k_tile(g, i), qt_ref[0, g, i])
            qb_sc[g] = qt_ref[0, g, i]

        def block(g, c, j):
            s_b = s_sc[g, rows(c), lanes(j)]
            if c < per_lane_block * j:
                return s_b
            key = c * ch + jax.lax.broadcasted_iota(jnp.int32, (ch, LANES), 0)
            qry = j * LANES + jax.lax.broadcasted_iota(jnp.int32, (ch, LANES), 1)
            return jnp.where(key <= qry, s_b, -jnp.inf)

        for g in range(G):
            for j in range(nqb):
                visible = range(per_lane_block * (j + 1))
                m8 = _rows8_reduce(block(g, 0, j), jnp.maximum)
                for c in visible[1:]:
                    m8 = jnp.maximum(m8, _rows8_reduce(block(g, c, j), jnp.maximum))
                m_d = jnp.max(m8, axis=0, keepdims=True)
                m_ref = m_d.astype(jnp.bfloat16).astype(jnp.float32)
                lsum = jnp.zeros((SUBLANES, LANES), jnp.float32)
                for c in range(nch):
                    if c in visible:
                        p_b = jnp.exp2(block(g, c, j) - m_ref)
                        lsum = lsum + _rows8_reduce(p_b, jnp.add)
                        p_sc[1, g, rows(c), lanes(j)] = p_b.astype(jnp.bfloat16)
                    else:
                        p_sc[1, g, rows(c), lanes(j)] = jnp.zeros((ch, LANES), jnp.bfloat16)
                l8_sc[g, :, lanes(j)] = lsum
                bias = jnp.broadcast_to(m_ref * (-1.0 / BIAS_ROWS), (BIAS_ROWS, LANES))
                qb_sc[g, BIAS_ROW0:BIAS_ROW0 + BIAS_ROWS, lanes(j)] = bias.astype(jnp.bfloat16)
        acc_sc[...] = jnp.zeros(acc_sc.shape, jnp.float32)

    def slow_path(i):
        m_sc[...] = jnp.full(m_sc.shape, M_INIT, jnp.float32)
        l_sc[...] = jnp.zeros(l_sc.shape, jnp.float32)
        acc_sc[...] = jnp.zeros(acc_sc.shape, jnp.float32)

        def chunk(g, c, masked):
            s_c = s_sc[g, rows(c), :]
            if not masked:
                return s_c
            key = c * ch + jax.lax.broadcasted_iota(jnp.int32, (ch, tq), 0)
            return jnp.where(key <= jax.lax.broadcasted_iota(jnp.int32, (ch, tq), 1), s_c, -jnp.inf)

        def tile(n, masked):
            for g in range(G):
                s_sc[g] = _dot(k_tile(g, n), qt_ref[0, g, i])
                m8 = _rows8_reduce(chunk(g, 0, masked), jnp.maximum)
                for c in range(1, nch):
                    m8 = jnp.maximum(m8, _rows8_reduce(chunk(g, c, masked), jnp.maximum))
                m_prev = m_sc[g]
                m_new = jnp.maximum(m_prev, jnp.max(m8, axis=0, keepdims=True))
                alpha = jnp.exp2(m_prev - m_new)
                lsum = exp_chunks(g, lambda c: chunk(g, c, masked) - m_new, 0,
                                  jnp.zeros((SUBLANES, tq), jnp.float32))
                l_sc[g] = alpha * l_sc[g] + jnp.sum(lsum, axis=0, keepdims=True)
                acc_sc[g] = alpha * acc_sc[g] + _dot(vt_tile(g, n), p_sc[0, g])
                m_sc[g] = m_new

        def full(n, carry):
            tile(n, False)
            return carry

        jax.lax.fori_loop(0, i, full, 0)
        tile(i, True)

    def denominators(l_range):
        l_lo, l_hi = l_range
        for g in range(G):
            l = jnp.sum(l8_sc[g], axis=0, keepdims=True)
            l_sc[g] = l
            l_lo = jnp.minimum(l_lo, jnp.min(l))
            l_hi = jnp.maximum(l_hi, jnp.max(l))
        return l_lo, l_hi

    l_range = (jnp.float32(jnp.inf), jnp.float32(0.0))
    for i in range(nq):
        if i > 0:
            l_range = denominators(l_range)
            finish(i - 1)
        diagonal(i)
        for n in range(i):
            for g in range(G):
                qk_exp(g, n % 2, n)
                pv(g, 1 - n % 2, i if n == 0 else n - 1)
    l_lo, l_hi = denominators(l_range)
    finish(nq - 1)

    in_range = jnp.logical_and(l_lo > SHIFT_L_MIN, l_hi < SHIFT_L_MAX)

    @pl.when(jnp.logical_not(in_range))
    def _():
        def redo(i, carry):
            slow_path(i)
            emit(i)
            return carry

        jax.lax.fori_loop(0, nq, redo, 0)


def _attn_call(qt, k, vt, gm):
    B, H, S, _ = k.shape
    tq, G = ATT_TQ, ATT_G
    assert ATT_TQ == ATT_TK and ATT_TK % KT == 0
    seq = lambda a: pl.BlockSpec((1, G) + a.shape[2:], lambda b, h: (b, h) + (0,) * (a.ndim - 2))
    tok = pl.BlockSpec((1, S, G * V_DIM), lambda b, h: (b, 0, h))
    return pl.pallas_call(
        _attn_kernel,
        grid=(B, H // G),
        in_specs=[seq(qt), seq(k), seq(vt), tok],
        out_specs=tok,
        out_shape=jax.ShapeDtypeStruct((B, S, D_MLA), jnp.bfloat16),
        scratch_shapes=[
            pltpu.VMEM((G, HEAD_W, tq), jnp.bfloat16),
            pltpu.VMEM((G, ATT_TK, tq), jnp.float32),
            pltpu.VMEM((2, G, ATT_TK, tq), jnp.bfloat16),
            pltpu.VMEM((G, SUBLANES, tq), jnp.float32),
            pltpu.VMEM((G, V_DIM, tq), jnp.float32),
            pltpu.VMEM((G, 1, tq), jnp.float32),
            pltpu.VMEM((G, 1, tq), jnp.float32),
        ],
        compiler_params=pltpu.CompilerParams(
            dimension_semantics=("arbitrary", "arbitrary"),
            vmem_limit_bytes=VMEM_LIMIT_BYTES),
        name="attn",
    )(qt, k, vt, gm)


def _out_kernel(o_ref, pc_ref, h_ref, w_ref, b_ref, g_ref, beta_ref, y_ref):
    for r in range(0, OUT_TM, OUT_CH):
        rows = slice(r, r + OUT_CH)
        mix = jnp.concatenate([o_ref[0, rows, :], pc_ref[0, rows, :]], axis=1)
        y = _dot(mix, w_ref[0]) + b_ref[0] + DEEPNORM_ALPHA * h_ref[0, rows, :]
        y_ref[0, rows, :] = _layernorm(y, g_ref[0], beta_ref[0])


def _out_call(layer, o, pc, h, w_out, b_out, ln_g, ln_b):
    B, S, D = h.shape
    tm = OUT_TM
    tok = lambda w: pl.BlockSpec((1, tm, w), lambda b, i: (b, i, 0))
    return pl.pallas_call(
        _out_kernel,
        grid=(B, S // tm),
        in_specs=[tok(D_MLA), tok(D_POOL + D_CONV), tok(D),
                  _layer_spec(w_out, layer), _layer_spec(b_out, layer), _layer_spec(ln_g, layer),
                  _layer_spec(ln_b, layer)],
        out_specs=tok(D),
        out_shape=jax.ShapeDtypeStruct((B, S, D), jnp.float32),
        compiler_params=pltpu.CompilerParams(
            dimension_semantics=("arbitrary", "arbitrary"),
            vmem_limit_bytes=VMEM_LIMIT_BYTES),
        name="out",
    )(o, pc, h, w_out, b_out, ln_g, ln_b)


def _rotate_half_cols(w):
    return jnp.concatenate([-w[..., HALF:], w[..., :HALF]], axis=-1)


def _wprep_lat_kernel(wt_ref, o_ref):
    x = wt_ref[0]
    kr = C_KR
    rot = jnp.concatenate([-x[kr + HALF:kr + ROPE], x[kr:kr + HALF]], axis=0)
    o_ref[0] = jnp.concatenate([x, rot], axis=0).T.astype(jnp.bfloat16)


def _wprep_mix_kernel(wt_ref, o_ref):
    o_ref[0] = wt_ref[0].T.astype(jnp.bfloat16)


def _prep_w_in(w):
    L, D, C = w.shape
    wt = jnp.swapaxes(w, 1, 2)
    n_lat = C_KR + ROPE
    params = pltpu.CompilerParams(vmem_limit_bytes=VMEM_LIMIT_BYTES)
    w_lat = pl.pallas_call(
        _wprep_lat_kernel,
        grid=(L,),
        in_specs=[pl.BlockSpec((pl.Element(1), pl.Element(n_lat), pl.Element(D)), lambda l: (l, 0, 0))],
        out_specs=pl.BlockSpec((1, D, C_LAT_END), lambda l: (l, 0, 0)),
        out_shape=jax.ShapeDtypeStruct((L, D, C_LAT_END), jnp.bfloat16),
        compiler_params=params,
        name="wprep_lat",
    )(wt)
    w_mix = pl.pallas_call(
        _wprep_mix_kernel,
        grid=(L, R_END // WPREP_COLS),
        in_specs=[pl.BlockSpec((pl.Element(1), pl.Element(WPREP_COLS), pl.Element(D)),
                               lambda l, j: (l, pl.multiple_of(n_lat + WPREP_COLS * j, SUBLANES), 0))],
        out_specs=pl.BlockSpec((1, D, WPREP_COLS), lambda l, j: (l, 0, j)),
        out_shape=jax.ShapeDtypeStruct((L, D, R_END), jnp.bfloat16),
        compiler_params=params,
        name="wprep_mix",
    )(wt)
    return w_lat, w_mix


def _prep_w_uq_t(w):
    L = w.shape[0]
    w = w.reshape(L, Q_LORA, N_HEADS, NOPE + ROPE)
    w = jnp.concatenate([w, _rotate_half_cols(w[..., NOPE:])], axis=-1)
    return jnp.swapaxes(w.reshape(L, Q_LORA, N_HEADS * HEAD_W), 1, 2).astype(jnp.bfloat16)


def _prep_w_ukv(w):
    L = w.shape[0]
    w = w.reshape(L, KV_LORA, N_HEADS, NOPE + V_DIM)
    wk = w[..., :NOPE].reshape(L, KV_LORA, N_HEADS * NOPE)
    wv = w[..., NOPE:].reshape(L, KV_LORA, N_HEADS * V_DIM)
    return wk.astype(jnp.bfloat16), jnp.swapaxes(wv, 1, 2).astype(jnp.bfloat16)


def kernel(x, positions, emb_ln_g, emb_ln_b, w_in, q_norm_g, kv_norm_g, w_uq, w_ukv, w_pool,
           pool_scale, conv_w, w_out, b_out, ln_g, ln_b):
    B, S, D = x.shape
    assert D == D_MODEL and w_in.shape == (DEPTH, D_MODEL, C_KR + ROPE + R_END)
    assert conv_w.shape == (DEPTH, CONV_WIDTH, D_CONV) and S % ATT_TQ == 0 and S % OUT_TM == 0
    bf = jnp.bfloat16
    posr = positions.reshape(B, 1, S)
    inv_freq = ROPE_THETA ** (-jnp.arange(HALF, dtype=jnp.float32) / HALF)
    invfc = inv_freq.reshape(HALF, 1)
    row = lambda a: a.reshape(1, -1)

    w_lat, w_mix = _prep_w_in(w_in)
    w_uk, w_uvt = _prep_w_ukv(w_ukv)
    row3 = lambda a: a.reshape(a.shape[0], 1, -1)
    stacked = (w_lat, w_mix, row3(q_norm_g), row3(kv_norm_g), _prep_w_uq_t(w_uq), w_uk, w_uvt,
               w_pool.astype(bf), row3(pool_scale), conv_w)
    w_out_b = w_out.astype(bf)

    h = x
    for l in range(DEPTH):
        outs = _proj_call(l, h, posr, row(emb_ln_g), row(emb_ln_b), invfc, stacked)
        if l == 0:
            qt, k, vt, gm, pc, h = outs
        else:
            qt, k, vt, gm, pc = outs
        o = _attn_call(qt, k, vt, gm)
        h = _out_call(l, o, pc, h, w_out_b, row3(b_out), row3(ln_g), row3(ln_b))
    return h
```

```python
import functools
import math

import jax
import jax.numpy as jnp
from jax.experimental import pallas as pl
from jax.experimental.pallas import tpu as pltpu

D_MODEL = 2048
DEPTH = 2
N_HEADS = 8
NOPE = 128
ROPE = 64
V_DIM = 128
Q_LORA = 512
KV_LORA = 256
D_MLA = N_HEADS * V_DIM
ROPE_THETA = 10000.0
POOL_WINDOWS = (2, 4, 8, 16)
POOL_GROUP = 128
D_POOL = 512
D_CONV = 512
CONV_WIDTH = 3
LN_EPS = 1e-5
RMS_EPS = 1e-6
DEEPNORM_ALPHA = (2 * DEPTH) ** 0.25

LANES = 128
SUBLANES = 8
VMEM_LIMIT_BYTES = 56 * 1024 * 1024

PROJ_TM = 256
ATT_TQ = 512
ATT_TK = 512
ATT_G = 2
ATT_CH = 64
WPREP_COLS = 1024
OUT_TM = 512
OUT_CH = 128
POOL_HALO = 16
CONV_HALO = 8

KT = PROJ_TM
HEAD_W = 2 * LANES
HALF = ROPE // 2

C_QLAT = 0
C_KVLAT = C_QLAT + Q_LORA
C_KR = C_KVLAT + KV_LORA
C_LAT_END = C_KR + LANES
R_GMLA = 0
R_PIN = R_GMLA + D_MLA
R_GPOOL = R_PIN + D_POOL
R_CH = R_GPOOL + D_POOL
R_END = R_CH + 4 * D_CONV

Q_PRESCALE = (NOPE + ROPE) ** -0.5 * math.log2(math.e)
M_INIT = -1e30
BIAS_ROW0 = NOPE + ROPE
BIAS_ROWS = 16
SHIFT_L_MIN = 2.0 ** -100
SHIFT_L_MAX = 2.0 ** 100


def _silu(g):
    return g * (1.0 / (1.0 + jnp.exp(-g)))


def _layernorm(x, g, b):
    mu = jnp.mean(x, axis=-1, keepdims=True)
    xc = x - mu
    var = jnp.mean(xc * xc, axis=-1, keepdims=True)
    return xc * jax.lax.rsqrt(var + LN_EPS) * g + b


def _rmsnorm(x, g):
    return x * jax.lax.rsqrt(jnp.mean(x * x, axis=-1, keepdims=True) + RMS_EPS) * g


def _dot(a, b):
    return jnp.dot(a, b, preferred_element_type=jnp.float32)


def _dot_nt(a, b):
    return jax.lax.dot_general(a, b, (((1,), (1,)), ((), ())), preferred_element_type=jnp.float32)


def _proj_kernel(first, *refs):
    if first:
        (x_ref, posr_ref, eg_ref, eb_ref, w_lat_ref, w_mix_ref, qg_ref, kvg_ref, w_uqt_ref, w_uk_ref,
         w_uvt_ref, w_pool_ref, pscale_ref, convw_ref, invfc_ref,
         qt_ref, k_ref, vt_ref, gm_ref, pc_ref, h_ref, pool_ext, conv_ext) = refs
    else:
        (x_ref, posr_ref, w_lat_ref, w_mix_ref, qg_ref, kvg_ref, w_uqt_ref, w_uk_ref,
         w_uvt_ref, w_pool_ref, pscale_ref, convw_ref, invfc_ref,
         qt_ref, k_ref, vt_ref, gm_ref, pc_ref, pool_ext, conv_ext) = refs
    tm = PROJ_TM
    i = pl.program_id(1)

    @pl.when(i == 0)
    def _():
        pool_ext[0:POOL_HALO, :] = jnp.zeros((POOL_HALO, D_POOL), jnp.float32)
        conv_ext[0:CONV_HALO, :] = jnp.zeros((CONV_HALO, D_CONV), jnp.float32)

    @pl.when(i > 0)
    def _():
        pool_ext[0:POOL_HALO, :] = pool_ext[tm:tm + POOL_HALO, :]
        conv_ext[0:CONV_HALO, :] = conv_ext[tm:tm + CONV_HALO, :]

    x = x_ref[0]
    if first:
        x = _layernorm(x, eg_ref[...], eb_ref[...])
        h_ref[0] = x
    hb = x.astype(jnp.bfloat16)

    lat = _dot(hb, w_lat_ref[0])
    pin = _dot(hb, w_mix_ref[0, :, R_PIN:R_GPOOL])
    cv = _dot(hb, w_mix_ref[0, :, R_CH:R_END])

    qn = _rmsnorm(lat[:, C_QLAT:C_KVLAT], qg_ref[0]).astype(jnp.bfloat16)
    kvn = _rmsnorm(lat[:, C_KVLAT:C_KR], kvg_ref[0]).astype(jnp.bfloat16)
    kr = lat[:, C_KR:C_LAT_END]
    angt = invfc_ref[...] * posr_ref[0].astype(jnp.float32)
    cos_t, sin_t = jnp.cos(angt), jnp.sin(angt)
    cs = jnp.concatenate([cos_t, cos_t, sin_t, sin_t], axis=0).T
    lane = jax.lax.broadcasted_iota(jnp.int32, (tm, LANES), 1)
    t = kr * cs
    ones = jnp.where(lane < ROPE + BIAS_ROWS, 1.0, 0.0)
    kroped = jnp.where(lane < ROPE, t + pltpu.roll(t, ROPE, axis=1), ones).astype(jnp.bfloat16)

    kn = _dot(kvn, w_uk_ref[0])
    vt = _dot_nt(w_uvt_ref[0], kvn)
    qt = _dot_nt(w_uqt_ref[0], qn)
    gpool = _dot(hb, w_mix_ref[0, :, R_GPOOL:R_CH])

    for hh in range(N_HEADS):
        k_ref[0, hh] = jnp.concatenate(
            [kn[:, hh * NOPE:(hh + 1) * NOPE].astype(jnp.bfloat16), kroped], axis=1)
        vt_ref[0, hh, 0] = vt[hh * V_DIM:(hh + 1) * V_DIM, :].astype(jnp.bfloat16)

    u = cv[:, 2 * D_CONV:3 * D_CONV] * cv[:, 0:D_CONV]
    conv_ext[CONV_HALO:CONV_HALO + tm, :] = u
    yc = convw_ref[0, 2:3, :] * u
    yc = yc + convw_ref[0, 1:2, :] * conv_ext[CONV_HALO - 1:CONV_HALO - 1 + tm, :]
    yc = yc + convw_ref[0, 0:1, :] * conv_ext[CONV_HALO - 2:CONV_HALO - 2 + tm, :]
    y_conv = cv[:, D_CONV:2 * D_CONV] * yc * _silu(cv[:, 3 * D_CONV:4 * D_CONV])
    pc_ref[0, :, D_POOL:] = y_conv.astype(jnp.bfloat16)

    pool_ext[POOL_HALO:POOL_HALO + tm, :] = pin
    t1 = (i * tm + 1 + jax.lax.broadcasted_iota(jnp.int32, (tm, 1), 0)).astype(jnp.float32)
    ys = []
    for g, w in enumerate(POOL_WINDOWS):
        lo = g * POOL_GROUP
        acc = pin[:, lo:lo + POOL_GROUP]
        for s in range(1, w):
            acc = acc + pool_ext[POOL_HALO - s:POOL_HALO - s + tm, lo:lo + POOL_GROUP]
        pooled = acc / jnp.minimum(t1, float(w)) - pin[:, lo:lo + POOL_GROUP]
        ys.append(_dot(pooled.astype(jnp.bfloat16), w_pool_ref[0, g]))
    gm = _dot(hb, w_mix_ref[0, :, R_GMLA:R_PIN])
    y_pool = jnp.concatenate(ys, axis=1) * pscale_ref[0] * _silu(gpool)
    pc_ref[0, :, :D_POOL] = y_pool.astype(jnp.bfloat16)

    cos_t = jnp.concatenate([cos_t, cos_t], axis=0)
    sin_t = jnp.concatenate([sin_t, sin_t], axis=0)
    zeros_t = jnp.zeros((HEAD_W - NOPE - ROPE, tm), jnp.float32)
    for hh in range(N_HEADS):
        r0 = hh * HEAD_W
        roped = qt[r0 + NOPE:r0 + NOPE + ROPE, :] * cos_t + qt[r0 + NOPE + ROPE:r0 + HEAD_W, :] * sin_t
        qh = jnp.concatenate([qt[r0:r0 + NOPE, :], roped, zeros_t], axis=0)
        qt_ref[0, hh, 0] = (qh * Q_PRESCALE).astype(jnp.bfloat16)

    gm_ref[0] = _silu(gm).astype(jnp.bfloat16)


def _const_spec(shape):
    nd = len(shape)
    return pl.BlockSpec(shape, lambda *_: (0,) * nd, pipeline_mode=pl.Buffered(1))


def _layer_spec(arr, layer):
    nd = arr.ndim
    return pl.BlockSpec((1,) + arr.shape[1:], lambda *_: (layer,) + (0,) * (nd - 1),
                        pipeline_mode=pl.Buffered(1))


def _proj_call(layer, x, posr, emb_g, emb_b, inv_freq_col, stacked):
    first = layer == 0
    B, S, D = x.shape
    tm = PROJ_TM
    grid = (B, S // tm)
    qsub = ATT_TQ // tm
    tok = lambda w: pl.BlockSpec((1, tm, w), lambda b, i: (b, i, 0))
    in_specs = [tok(D), pl.BlockSpec((1, 1, tm), lambda b, i: (b, 0, i))]
    args = [x, posr]
    if first:
        in_specs += [_const_spec((1, D)), _const_spec((1, D))]
        args += [emb_g, emb_b]
    in_specs += [_layer_spec(a, layer) for a in stacked] + [_const_spec(inv_freq_col.shape)]
    args += list(stacked) + [inv_freq_col]
    bf = jnp.bfloat16
    out_shape = [
        jax.ShapeDtypeStruct((B, N_HEADS, S // ATT_TQ, HEAD_W, ATT_TQ), bf),
        jax.ShapeDtypeStruct((B, N_HEADS, S, HEAD_W), bf),
        jax.ShapeDtypeStruct((B, N_HEADS, S // KT, V_DIM, KT), bf),
        jax.ShapeDtypeStruct((B, S, D_MLA), bf),
        jax.ShapeDtypeStruct((B, S, D_POOL + D_CONV), bf),
    ]
    out_specs = [
        pl.BlockSpec((1, N_HEADS, 1, HEAD_W, tm), lambda b, i: (b, 0, i // qsub, 0, i % qsub)),
        pl.BlockSpec((1, N_HEADS, tm, HEAD_W), lambda b, i: (b, 0, i, 0)),
        pl.BlockSpec((1, N_HEADS, 1, V_DIM, KT), lambda b, i: (b, 0, i, 0, 0)),
        tok(D_MLA), tok(D_POOL + D_CONV)]
    if first:
        out_shape.append(jax.ShapeDtypeStruct((B, S, D), jnp.float32))
        out_specs.append(tok(D))
    return pl.pallas_call(
        functools.partial(_proj_kernel, first),
        grid=grid,
        in_specs=in_specs,
        out_specs=out_specs,
        out_shape=out_shape,
        scratch_shapes=[
            pltpu.VMEM((POOL_HALO + tm, D_POOL), jnp.float32),
            pltpu.VMEM((CONV_HALO + tm, D_CONV), jnp.float32),
        ],
        compiler_params=pltpu.CompilerParams(
            dimension_semantics=("arbitrary", "arbitrary"),
            vmem_limit_bytes=VMEM_LIMIT_BYTES),
        name="proj_first" if first else "proj",
    )(*args)


def _rows8_reduce(x, op):
    acc = x[0:SUBLANES]
    for r in range(SUBLANES, x.shape[0], SUBLANES):
        acc = op(acc, x[r:r + SUBLANES])
    return acc


def _attn_kernel(qt_ref, k_ref, vt_ref, gm_ref, o_ref,
                 qb_sc, s_sc, p_sc, l8_sc, acc_sc, m_sc, l_sc):
    tq, tk, G, ch = ATT_TQ, ATT_TK, ATT_G, ATT_CH
    sub, nch, nqb = tk // KT, tk // ch, tq // LANES
    per_lane_block = LANES // ch
    nq = qt_ref.shape[2]

    def vt_tile(g, n):
        return jnp.concatenate([vt_ref[0, g, n * sub + c] for c in range(sub)], axis=1)

    def k_tile(g, n):
        return k_ref[0, g, pl.ds(pl.multiple_of(n * tk, tk), tk), :]

    def rows(c):
        return slice(c * ch, (c + 1) * ch)

    def lanes(j):
        return slice(j * LANES, (j + 1) * LANES)

    def exp_chunks(g, chunk_of, buf, lsum):
        for c in range(nch):
            p_c = jnp.exp2(chunk_of(c))
            lsum = lsum + _rows8_reduce(p_c, jnp.add)
            p_sc[buf, g, rows(c), :] = p_c.astype(jnp.bfloat16)
        return lsum

    def qk_exp(g, buf, n):
        s = _dot(k_tile(g, n), qb_sc[g])
        l8_sc[g] = exp_chunks(g, lambda c: s[rows(c)], buf, l8_sc[g])

    def pv(g, buf, n):
        acc_sc[g] = acc_sc[g] + _dot(vt_tile(g, n), p_sc[buf, g])

    def emit(i):
        q0 = pl.multiple_of(i * tq, tq)
        for g in range(G):
            out_t = acc_sc[g] / l_sc[g]
            gate = gm_ref[0, pl.ds(q0, tq), g * V_DIM:(g + 1) * V_DIM].astype(jnp.float32)
            o_ref[0, pl.ds(q0, tq), g * V_DIM:(g + 1) * V_DIM] = (out_t.T * gate).astype(o_ref.dtype)

    def finish(i):
        last = jnp.maximum(i - 1, 0)
        in_buf1 = jnp.logical_or(i == 0, last % 2 == 1)
        for g in range(G):
            p_last = jnp.where(in_buf1, p_sc[1, g], p_sc[0, g])
            acc_sc[g] = acc_sc[g] + _dot(vt_tile(g, last), p_last)
        emit(i)

    def diagonal(i):
        for g in range(G):
            s_sc[g] = _dot(k_tile(g, i), qt_ref[0, g, i])
            qb_sc[g] = qt_ref[0, g, i]

        def block(g, c, j):
            s_b = s_sc[g, rows(c), lanes(j)]
            if c < per_lane_block * j:
                return s_b
            key = c * ch + jax.lax.broadcasted_iota(jnp.int32, (ch, LANES), 0)
            qry = j * LANES + jax.lax.broadcasted_iota(jnp.int32, (ch, LANES), 1)
            return jnp.where(key <= qry, s_b, -jnp.inf)

        for g in range(G):
            for j in range(nqb):
                visible = range(per_lane_block * (j + 1))
                m8 = _rows8_reduce(block(g, 0, j), jnp.maximum)
                for c in visible[1:]:
                    m8 = jnp.maximum(m8, _rows8_reduce(block(g, c, j), jnp.maximum))
                m_d = jnp.max(m8, axis=0, keepdims=True)
                m_ref = m_d.astype(jnp.bfloat16).astype(jnp.float32)
                lsum = jnp.zeros((SUBLANES, LANES), jnp.float32)
                for c in range(nch):
                    if c in visible:
                        p_b = jnp.exp2(block(g, c, j) - m_ref)
                        lsum = lsum + _rows8_reduce(p_b, jnp.add)
                        p_sc[1, g, rows(c), lanes(j)] = p_b.astype(jnp.bfloat16)
                    else:
                        p_sc[1, g, rows(c), lanes(j)] = jnp.zeros((ch, LANES), jnp.bfloat16)
                l8_sc[g, :, lanes(j)] = lsum
                bias = jnp.broadcast_to(m_ref * (-1.0 / BIAS_ROWS), (BIAS_ROWS, LANES))
                qb_sc[g, BIAS_ROW0:BIAS_ROW0 + BIAS_ROWS, lanes(j)] = bias.astype(jnp.bfloat16)
        acc_sc[...] = jnp.zeros(acc_sc.shape, jnp.float32)

    def slow_path(i):
        m_sc[...] = jnp.full(m_sc.shape, M_INIT, jnp.float32)
        l_sc[...] = jnp.zeros(l_sc.shape, jnp.float32)
        acc_sc[...] = jnp.zeros(acc_sc.shape, jnp.float32)

        def chunk(g, c, masked):
            s_c = s_sc[g, rows(c), :]
            if not masked:
                return s_c
            key = c * ch + jax.lax.broadcasted_iota(jnp.int32, (ch, tq), 0)
            return jnp.where(key <= jax.lax.broadcasted_iota(jnp.int32, (ch, tq), 1), s_c, -jnp.inf)

        def tile(n, masked):
            for g in range(G):
                s_sc[g] = _dot(k_tile(g, n), qt_ref[0, g, i])
                m8 = _rows8_reduce(chunk(g, 0, masked), jnp.maximum)
                for c in range(1, nch):
                    m8 = jnp.maximum(m8, _rows8_reduce(chunk(g, c, masked), jnp.maximum))
                m_prev = m_sc[g]
                m_new = jnp.maximum(m_prev, jnp.max(m8, axis=0, keepdims=True))
                alpha = jnp.exp2(m_prev - m_new)
                lsum = exp_chunks(g, lambda c: chunk(g, c, masked) - m_new, 0,
                                  jnp.zeros((SUBLANES, tq), jnp.float32))
                l_sc[g] = alpha * l_sc[g] + jnp.sum(lsum, axis=0, keepdims=True)
                acc_sc[g] = alpha * acc_sc[g] + _dot(vt_tile(g, n), p_sc[0, g])
                m_sc[g] = m_new

        def full(n, carry):
            tile(n, False)
            return carry

        jax.lax.fori_loop(0, i, full, 0)
        tile(i, True)

    def denominators(l_range):
        l_lo, l_hi = l_range
        for g in range(G):
            l = jnp.sum(l8_sc[g], axis=0, keepdims=True)
            l_sc[g] = l
            l_lo = jnp.minimum(l_lo, jnp.min(l))
            l_hi = jnp.maximum(l_hi, jnp.max(l))
        return l_lo, l_hi

    def query_tile(i, l_range):
        l_range = denominators(l_range)
        finish(jnp.maximum(i - 1, 0))
        diagonal(i)

        def pair(t, c):
            n = 2 * t
            for g in range(G):
                qk_exp(g, 0, n)
                pv(g, 1, jnp.where(t == 0, i, n - 1))
            for g in range(G):
                qk_exp(g, 1, n + 1)
                pv(g, 0, n)
            return c

        jax.lax.fori_loop(0, i // 2, pair, 0)

        @pl.when(i % 2 == 1)
        def _():
            for g in range(G):
                qk_exp(g, 0, i - 1)
                pv(g, 1, jnp.where(i == 1, i, i - 2))

        return l_range

    acc_sc[...] = jnp.zeros(acc_sc.shape, jnp.float32)
    l8_sc[...] = jnp.ones(l8_sc.shape, jnp.float32)
    p_sc[1] = jnp.zeros(p_sc.shape[1:], jnp.bfloat16)
    l_range = jax.lax.fori_loop(0, nq, query_tile, (jnp.float32(jnp.inf), jnp.float32(0.0)))
    l_lo, l_hi = denominators(l_range)
    finish(nq - 1)

    in_range = jnp.logical_and(l_lo > SHIFT_L_MIN, l_hi < SHIFT_L_MAX)

    @pl.when(jnp.logical_not(in_range))
    def _():
        def redo(i, carry):
            slow_path(i)
            emit(i)
            return carry

        jax.lax.fori_loop(0, nq, redo, 0)


def _attn_call(qt, k, vt, gm):
    B, H, S, _ = k.shape
    tq, G = ATT_TQ, ATT_G
    assert ATT_TQ == ATT_TK and ATT_TK % KT == 0
    seq = lambda a: pl.BlockSpec((1, G) + a.shape[2:], lambda b, h: (b, h) + (0,) * (a.ndim - 2))
    tok = pl.BlockSpec((1, S, G * V_DIM), lambda b, h: (b, 0, h))
    return pl.pallas_call(
        _attn_kernel,
        grid=(B, H // G),
        in_specs=[seq(qt), seq(k), seq(vt), tok],
        out_specs=tok,
        out_shape=jax.ShapeDtypeStruct((B, S, D_MLA), jnp.bfloat16),
        scratch_shapes=[
            pltpu.VMEM((G, HEAD_W, tq), jnp.bfloat16),
            pltpu.VMEM((G, ATT_TK, tq), jnp.float32),
            pltpu.VMEM((2, G, ATT_TK, tq), jnp.bfloat16),
            pltpu.VMEM((G, SUBLANES, tq), jnp.float32),
            pltpu.VMEM((G, V_DIM, tq), jnp.float32),
            pltpu.VMEM((G, 1, tq), jnp.float32),
            pltpu.VMEM((G, 1, tq), jnp.float32),
        ],
        compiler_params=pltpu.CompilerParams(
            dimension_semantics=("arbitrary", "arbitrary"),
            vmem_limit_bytes=VMEM_LIMIT_BYTES),
        name="attn",
    )(qt, k, vt, gm)


def _out_kernel(o_ref, pc_ref, h_ref, w_ref, b_ref, g_ref, beta_ref, y_ref):
    for r in range(0, OUT_TM, OUT_CH):
        rows = slice(r, r + OUT_CH)
        mix = jnp.concatenate([o_ref[0, rows, :], pc_ref[0, rows, :]], axis=1)
        y = _dot(mix, w_ref[0]) + b_ref[0] + DEEPNORM_ALPHA * h_ref[0, rows, :]
        y_ref[0, rows, :] = _layernorm(y, g_ref[0], beta_ref[0])


def _out_call(layer, o, pc, h, w_out, b_out, ln_g, ln_b):
    B, S, D = h.shape
    tm = OUT_TM
    tok = lambda w: pl.BlockSpec((1, tm, w), lambda b, i: (b, i, 0))
    return pl.pallas_call(
        _out_kernel,
        grid=(B, S // tm),
        in_specs=[tok(D_MLA), tok(D_POOL + D_CONV), tok(D),
                  _layer_spec(w_out, layer), _layer_spec(b_out, layer), _layer_spec(ln_g, layer),
                  _layer_spec(ln_b, layer)],
        out_specs=tok(D),
        out_shape=jax.ShapeDtypeStruct((B, S, D), jnp.float32),
        compiler_params=pltpu.CompilerParams(
            dimension_semantics=("arbitrary", "arbitrary"),
            vmem_limit_bytes=VMEM_LIMIT_BYTES),
        name="out",
    )(o, pc, h, w_out, b_out, ln_g, ln_b)


def _rotate_half_cols(w):
    return jnp.concatenate([-w[..., HALF:], w[..., :HALF]], axis=-1)


def _wprep_lat_kernel(wt_ref, o_ref):
    x = wt_ref[0]
    kr = C_KR
    rot = jnp.concatenate([-x[kr + HALF:kr + ROPE], x[kr:kr + HALF]], axis=0)
    o_ref[0] = jnp.concatenate([x, rot], axis=0).T.astype(jnp.bfloat16)


def _wprep_mix_kernel(wt_ref, o_ref):
    o_ref[0] = wt_ref[0].T.astype(jnp.bfloat16)


def _prep_w_in(w):
    L, D, C = w.shape
    wt = jnp.swapaxes(w, 1, 2)
    n_lat = C_KR + ROPE
    params = pltpu.CompilerParams(vmem_limit_bytes=VMEM_LIMIT_BYTES)
    w_lat = pl.pallas_call(
        _wprep_lat_kernel,
        grid=(L,),
        in_specs=[pl.BlockSpec((pl.Element(1), pl.Element(n_lat), pl.Element(D)), lambda l: (l, 0, 0))],
        out_specs=pl.BlockSpec((1, D, C_LAT_END), lambda l: (l, 0, 0)),
        out_shape=jax.ShapeDtypeStruct((L, D, C_LAT_END), jnp.bfloat16),
        compiler_params=params,
        name="wprep_lat",
    )(wt)
    w_mix = pl.pallas_call(
        _wprep_mix_kernel,
        grid=(L, R_END // WPREP_COLS),
        in_specs=[pl.BlockSpec((pl.Element(1), pl.Element(WPREP_COLS), pl.Element(D)),
                               lambda l, j: (l, pl.multiple_of(n_lat + WPREP_COLS * j, SUBLANES), 0))],
        out_specs=pl.BlockSpec((1, D, WPREP_COLS), lambda l, j: (l, 0, j)),
        out_shape=jax.ShapeDtypeStruct((L, D, R_END), jnp.bfloat16),
        compiler_params=params,
        name="wprep_mix",
    )(wt)
    return w_lat, w_mix


def _prep_w_uq_t(w):
    L = w.shape[0]
    w = w.reshape(L, Q_LORA, N_HEADS, NOPE + ROPE)
    w = jnp.concatenate([w, _rotate_half_cols(w[..., NOPE:])], axis=-1)
    return jnp.swapaxes(w.reshape(L, Q_LORA, N_HEADS * HEAD_W), 1, 2).astype(jnp.bfloat16)


def _prep_w_ukv(w):
    L = w.shape[0]
    w = w.reshape(L, KV_LORA, N_HEADS, NOPE + V_DIM)
    wk = w[..., :NOPE].reshape(L, KV_LORA, N_HEADS * NOPE)
    wv = w[..., NOPE:].reshape(L, KV_LORA, N_HEADS * V_DIM)
    return wk.astype(jnp.bfloat16), jnp.swapaxes(wv, 1, 2).astype(jnp.bfloat16)


def kernel(x, positions, emb_ln_g, emb_ln_b, w_in, q_norm_g, kv_norm_g, w_uq, w_ukv, w_pool,
           pool_scale, conv_w, w_out, b_out, ln_g, ln_b):
    B, S, D = x.shape
    assert D == D_MODEL and w_in.shape == (DEPTH, D_MODEL, C_KR + ROPE + R_END)
    assert conv_w.shape == (DEPTH, CONV_WIDTH, D_CONV) and S % ATT_TQ == 0 and S % OUT_TM == 0
    bf = jnp.bfloat16
    posr = positions.reshape(B, 1, S)
    inv_freq = ROPE_THETA ** (-jnp.arange(HALF, dtype=jnp.float32) / HALF)
    invfc = inv_freq.reshape(HALF, 1)
    row = lambda a: a.reshape(1, -1)

    w_lat, w_mix = _prep_w_in(w_in)
    w_uk, w_uvt = _prep_w_ukv(w_ukv)
    row3 = lambda a: a.reshape(a.shape[0], 1, -1)
    stacked = (w_lat, w_mix, row3(q_norm_g), row3(kv_norm_g), _prep_w_uq_t(w_uq), w_uk, w_uvt,
               w_pool.astype(bf), row3(pool_scale), conv_w)
    w_out_b = w_out.astype(bf)

    h = x
    for l in range(DEPTH):
        outs = _proj_call(l, h, posr, row(emb_ln_g), row(emb_ln_b), invfc, stacked)
        if l == 0:
            qt, k, vt, gm, pc, h = outs
        else:
            qt, k, vt, gm, pc = outs
        o = _attn_call(qt, k, vt, gm)
        h = _out_call(l, o, pc, h, w_out_b, row3(b_out), row3(ln_g), row3(ln_b))
    return h
```

```python
import functools
import math

import jax
import jax.numpy as jnp
from jax.experimental import pallas as pl
from jax.experimental.pallas import tpu as pltpu

D_MODEL = 2048
DEPTH = 2
N_HEADS = 8
NOPE = 128
ROPE = 64
V_DIM = 128
Q_LORA = 512
KV_LORA = 256
D_MLA = N_HEADS * V_DIM
ROPE_THETA = 10000.0
POOL_WINDOWS = (2, 4, 8, 16)
POOL_GROUP = 128
D_POOL = 512
D_CONV = 512
CONV_WIDTH = 3
LN_EPS = 1e-5
RMS_EPS = 1e-6
DEEPNORM_ALPHA = (2 * DEPTH) ** 0.25

LANES = 128
SUBLANES = 8
VMEM_LIMIT_BYTES = 56 * 1024 * 1024

PROJ_TM = 256
ATT_TQ = 512
ATT_TK = 512
ATT_G = 2
ATT_CH = 64
WPREP_COLS = 1024
OUT_TM = 512
OUT_CH = 128
POOL_HALO = 16
CONV_HALO = 8

KT = PROJ_TM
HEAD_W = 2 * LANES
HALF = ROPE // 2

C_QLAT = 0
C_KVLAT = C_QLAT + Q_LORA
C_KR = C_KVLAT + KV_LORA
C_LAT_END = C_KR + LANES
R_GMLA = 0
R_PIN = R_GMLA + D_MLA
R_GPOOL = R_PIN + D_POOL
R_CH = R_GPOOL + D_POOL
R_END = R_CH + 4 * D_CONV

Q_PRESCALE = (NOPE + ROPE) ** -0.5 * math.log2(math.e)
M_INIT = -1e30
SHIFT_L_MIN = 2.0 ** -100
SHIFT_L_MAX = 2.0 ** 100


def _silu(g):
    return g * (1.0 / (1.0 + jnp.exp(-g)))


def _layernorm(x, g, b):
    mu = jnp.mean(x, axis=-1, keepdims=True)
    xc = x - mu
    var = jnp.mean(xc * xc, axis=-1, keepdims=True)
    return xc * jax.lax.rsqrt(var + LN_EPS) * g + b


def _rmsnorm(x, g):
    return x * jax.lax.rsqrt(jnp.mean(x * x, axis=-1, keepdims=True) + RMS_EPS) * g


def _dot(a, b):
    return jnp.dot(a, b, preferred_element_type=jnp.float32)


def _dot_nt(a, b):
    return jax.lax.dot_general(a, b, (((1,), (1,)), ((), ())), preferred_element_type=jnp.float32)


def _proj_kernel(first, *refs):
    if first:
        (x_ref, posr_ref, eg_ref, eb_ref, w_lat_ref, w_mix_ref, qg_ref, kvg_ref, w_uqt_ref, w_uk_ref,
         w_uvt_ref, w_pool_ref, pscale_ref, convw_ref, invfc_ref,
         qt_ref, k_ref, vt_ref, gm_ref, pc_ref, h_ref, pool_ext, conv_ext) = refs
    else:
        (x_ref, posr_ref, w_lat_ref, w_mix_ref, qg_ref, kvg_ref, w_uqt_ref, w_uk_ref,
         w_uvt_ref, w_pool_ref, pscale_ref, convw_ref, invfc_ref,
         qt_ref, k_ref, vt_ref, gm_ref, pc_ref, pool_ext, conv_ext) = refs
    tm = PROJ_TM
    i = pl.program_id(1)

    @pl.when(i == 0)
    def _():
        pool_ext[0:POOL_HALO, :] = jnp.zeros((POOL_HALO, D_POOL), jnp.float32)
        conv_ext[0:CONV_HALO, :] = jnp.zeros((CONV_HALO, D_CONV), jnp.float32)

    @pl.when(i > 0)
    def _():
        pool_ext[0:POOL_HALO, :] = pool_ext[tm:tm + POOL_HALO, :]
        conv_ext[0:CONV_HALO, :] = conv_ext[tm:tm + CONV_HALO, :]

    x = x_ref[0]
    if first:
        x = _layernorm(x, eg_ref[...], eb_ref[...])
        h_ref[0] = x
    hb = x.astype(jnp.bfloat16)

    lat = _dot(hb, w_lat_ref[0])
    pin = _dot(hb, w_mix_ref[0, :, R_PIN:R_GPOOL])
    cv = _dot(hb, w_mix_ref[0, :, R_CH:R_END])

    qn = _rmsnorm(lat[:, C_QLAT:C_KVLAT], qg_ref[0]).astype(jnp.bfloat16)
    kvn = _rmsnorm(lat[:, C_KVLAT:C_KR], kvg_ref[0]).astype(jnp.bfloat16)
    kr = lat[:, C_KR:C_LAT_END]
    angt = invfc_ref[...] * posr_ref[0].astype(jnp.float32)
    cos_t, sin_t = jnp.cos(angt), jnp.sin(angt)
    cs = jnp.concatenate([cos_t, cos_t, sin_t, sin_t], axis=0).T
    lane = jax.lax.broadcasted_iota(jnp.int32, (tm, LANES), 1)
    t = kr * cs
    kroped = jnp.where(lane < ROPE, t + pltpu.roll(t, ROPE, axis=1), 0.0).astype(jnp.bfloat16)

    kn = _dot(kvn, w_uk_ref[0])
    vt = _dot_nt(w_uvt_ref[0], kvn)
    qt = _dot_nt(w_uqt_ref[0], qn)
    gpool = _dot(hb, w_mix_ref[0, :, R_GPOOL:R_CH])

    for hh in range(N_HEADS):
        k_ref[0, hh] = jnp.concatenate(
            [kn[:, hh * NOPE:(hh + 1) * NOPE].astype(jnp.bfloat16), kroped], axis=1)
        vt_ref[0, hh, 0] = vt[hh * V_DIM:(hh + 1) * V_DIM, :].astype(jnp.bfloat16)

    u = cv[:, 2 * D_CONV:3 * D_CONV] * cv[:, 0:D_CONV]
    conv_ext[CONV_HALO:CONV_HALO + tm, :] = u
    yc = convw_ref[0, 2:3, :] * u
    yc = yc + convw_ref[0, 1:2, :] * conv_ext[CONV_HALO - 1:CONV_HALO - 1 + tm, :]
    yc = yc + convw_ref[0, 0:1, :] * conv_ext[CONV_HALO - 2:CONV_HALO - 2 + tm, :]
    y_conv = cv[:, D_CONV:2 * D_CONV] * yc * _silu(cv[:, 3 * D_CONV:4 * D_CONV])
    pc_ref[0, :, D_POOL:] = y_conv.astype(jnp.bfloat16)

    pool_ext[POOL_HALO:POOL_HALO + tm, :] = pin
    t1 = (i * tm + 1 + jax.lax.broadcasted_iota(jnp.int32, (tm, 1), 0)).astype(jnp.float32)
    ys = []
    for g, w in enumerate(POOL_WINDOWS):
        lo = g * POOL_GROUP
        acc = pin[:, lo:lo + POOL_GROUP]
        for s in range(1, w):
            acc = acc + pool_ext[POOL_HALO - s:POOL_HALO - s + tm, lo:lo + POOL_GROUP]
        pooled = acc / jnp.minimum(t1, float(w)) - pin[:, lo:lo + POOL_GROUP]
        ys.append(_dot(pooled.astype(jnp.bfloat16), w_pool_ref[0, g]))
    gm = _dot(hb, w_mix_ref[0, :, R_GMLA:R_PIN])
    y_pool = jnp.concatenate(ys, axis=1) * pscale_ref[0] * _silu(gpool)
    pc_ref[0, :, :D_POOL] = y_pool.astype(jnp.bfloat16)

    cos_t = jnp.concatenate([cos_t, cos_t], axis=0)
    sin_t = jnp.concatenate([sin_t, sin_t], axis=0)
    zeros_t = jnp.zeros((HEAD_W - NOPE - ROPE, tm), jnp.float32)
    for hh in range(N_HEADS):
        r0 = hh * HEAD_W
        roped = qt[r0 + NOPE:r0 + NOPE + ROPE, :] * cos_t + qt[r0 + NOPE + ROPE:r0 + HEAD_W, :] * sin_t
        qh = jnp.concatenate([qt[r0:r0 + NOPE, :], roped, zeros_t], axis=0)
        qt_ref[0, hh, 0] = (qh * Q_PRESCALE).astype(jnp.bfloat16)

    gm_ref[0] = _silu(gm).astype(jnp.bfloat16)


def _const_spec(shape):
    nd = len(shape)
    return pl.BlockSpec(shape, lambda *_: (0,) * nd, pipeline_mode=pl.Buffered(1))


def _layer_spec(arr, layer):
    nd = arr.ndim
    return pl.BlockSpec((1,) + arr.shape[1:], lambda *_: (layer,) + (0,) * (nd - 1),
                        pipeline_mode=pl.Buffered(1))


def _proj_call(layer, x, posr, emb_g, emb_b, inv_freq_col, stacked):
    first = layer == 0
    B, S, D = x.shape
    tm = PROJ_TM
    grid = (B, S // tm)
    qsub = ATT_TQ // tm
    tok = lambda w: pl.BlockSpec((1, tm, w), lambda b, i: (b, i, 0))
    in_specs = [tok(D), pl.BlockSpec((1, 1, tm), lambda b, i: (b, 0, i))]
    args = [x, posr]
    if first:
        in_specs += [_const_spec((1, D)), _const_spec((1, D))]
        args += [emb_g, emb_b]
    in_specs += [_layer_spec(a, layer) for a in stacked] + [_const_spec(inv_freq_col.shape)]
    args += list(stacked) + [inv_freq_col]
    bf = jnp.bfloat16
    out_shape = [
        jax.ShapeDtypeStruct((B, N_HEADS, S // ATT_TQ, HEAD_W, ATT_TQ), bf),
        jax.ShapeDtypeStruct((B, N_HEADS, S, HEAD_W), bf),
        jax.ShapeDtypeStruct((B, N_HEADS, S // KT, V_DIM, KT), bf),
        jax.ShapeDtypeStruct((B, S, D_MLA), bf),
        jax.ShapeDtypeStruct((B, S, D_POOL + D_CONV), bf),
    ]
    out_specs = [
        pl.BlockSpec((1, N_HEADS, 1, HEAD_W, tm), lambda b, i: (b, 0, i // qsub, 0, i % qsub)),
        pl.BlockSpec((1, N_HEADS, tm, HEAD_W), lambda b, i: (b, 0, i, 0)),
        pl.BlockSpec((1, N_HEADS, 1, V_DIM, KT), lambda b, i: (b, 0, i, 0, 0)),
        tok(D_MLA), tok(D_POOL + D_CONV)]
    if first:
        out_shape.append(jax.ShapeDtypeStruct((B, S, D), jnp.float32))
        out_specs.append(tok(D))
    return pl.pallas_call(
        functools.partial(_proj_kernel, first),
        grid=grid,
        in_specs=in_specs,
        out_specs=out_specs,
        out_shape=out_shape,
        scratch_shapes=[
            pltpu.VMEM((POOL_HALO + tm, D_POOL), jnp.float32),
            pltpu.VMEM((CONV_HALO + tm, D_CONV), jnp.float32),
        ],
        compiler_params=pltpu.CompilerParams(
            dimension_semantics=("arbitrary", "arbitrary"),
            vmem_limit_bytes=VMEM_LIMIT_BYTES),
        name="proj_first" if first else "proj",
    )(*args)


def _rows8_reduce(x, op):
    acc = x[0:SUBLANES]
    for r in range(SUBLANES, x.shape[0], SUBLANES):
        acc = op(acc, x[r:r + SUBLANES])
    return acc


def _attn_kernel(qt_ref, k_ref, vt_ref, gm_ref, o_ref,
                 s_sc, p_sc, l8_sc, acc_sc, m_sc, l_sc):
    tq, tk, G, ch = ATT_TQ, ATT_TK, ATT_G, ATT_CH
    sub, nch, nqb = tk // KT, tk // ch, tq // LANES
    per_lane_block = LANES // ch
    nq = qt_ref.shape[2]

    def vt_tile(g, n):
        return jnp.concatenate([vt_ref[0, g, n * sub + c] for c in range(sub)], axis=1)

    def k_tile(g, n):
        start = n * tk if isinstance(n, int) else pl.multiple_of(n * tk, tk)
        return k_ref[0, g, pl.ds(start, tk), :]

    def rows(c):
        return slice(c * ch, (c + 1) * ch)

    def lanes(j):
        return slice(j * LANES, (j + 1) * LANES)

    def exp_chunks(g, chunk_of, buf, lsum):
        for c in range(nch):
            p_c = jnp.exp2(chunk_of(c))
            lsum = lsum + _rows8_reduce(p_c, jnp.add)
            p_sc[buf, g, rows(c), :] = p_c.astype(jnp.bfloat16)
        return lsum

    def qk_exp(g, buf, n, qi):
        s = _dot(k_tile(g, n), qt_ref[0, g, qi])
        m_ref = m_sc[g]
        l8_sc[g] = exp_chunks(g, lambda c: s[rows(c)] - m_ref, buf, l8_sc[g])

    def pv(g, buf, n):
        acc_sc[g] = acc_sc[g] + _dot(vt_tile(g, n), p_sc[buf, g])

    def emit(i):
        q0 = i * tq if isinstance(i, int) else pl.multiple_of(i * tq, tq)
        for g in range(G):
            out_t = acc_sc[g] / l_sc[g]
            gate = gm_ref[0, pl.ds(q0, tq), g * V_DIM:(g + 1) * V_DIM].astype(jnp.float32)
            o_ref[0, pl.ds(q0, tq), g * V_DIM:(g + 1) * V_DIM] = (out_t.T * gate).astype(o_ref.dtype)

    def finish(i):
        last = max(i - 1, 0)
        for g in range(G):
            pv(g, 1 if i == 0 else last % 2, last)
        emit(i)

    def diagonal(i):
        for g in range(G):
            s_sc[g] = _dot(k_tile(g, i), qt_ref[0, g, i])

        def block(g, c, j):
            s_b = s_sc[g, rows(c), lanes(j)]
            if c < per_lane_block * j:
                return s_b
            key = c * ch + jax.lax.broadcasted_iota(jnp.int32, (ch, LANES), 0)
            qry = j * LANES + jax.lax.broadcasted_iota(jnp.int32, (ch, LANES), 1)
            return jnp.where(key <= qry, s_b, -jnp.inf)

        for g in range(G):
            for j in range(nqb):
                visible = range(per_lane_block * (j + 1))
                m8 = _rows8_reduce(block(g, 0, j), jnp.maximum)
                for c in visible[1:]:
                    m8 = jnp.maximum(m8, _rows8_reduce(block(g, c, j), jnp.maximum))
                m_d = jnp.max(m8, axis=0, keepdims=True)
                m_ref = m_d.astype(jnp.bfloat16).astype(jnp.float32)
                lsum = jnp.zeros((SUBLANES, LANES), jnp.float32)
                for c in range(nch):
                    if c in visible:
                        p_b = jnp.exp2(block(g, c, j) - m_ref)
                        lsum = lsum + _rows8_reduce(p_b, jnp.add)
                        p_sc[1, g, rows(c), lanes(j)] = p_b.astype(jnp.bfloat16)
                    else:
                        p_sc[1, g, rows(c), lanes(j)] = jnp.zeros((ch, LANES), jnp.bfloat16)
                l8_sc[g, :, lanes(j)] = lsum
                m_sc[g, :, lanes(j)] = m_ref
        acc_sc[...] = jnp.zeros(acc_sc.shape, jnp.float32)

    def slow_path(i):
        m_sc[...] = jnp.full(m_sc.shape, M_INIT, jnp.float32)
        l_sc[...] = jnp.zeros(l_sc.shape, jnp.float32)
        acc_sc[...] = jnp.zeros(acc_sc.shape, jnp.float32)

        def chunk(g, c, masked):
            s_c = s_sc[g, rows(c), :]
            if not masked:
                return s_c
            key = c * ch + jax.lax.broadcasted_iota(jnp.int32, (ch, tq), 0)
            return jnp.where(key <= jax.lax.broadcasted_iota(jnp.int32, (ch, tq), 1), s_c, -jnp.inf)

        def tile(n, masked):
            for g in range(G):
                s_sc[g] = _dot(k_tile(g, n), qt_ref[0, g, i])
                m8 = _rows8_reduce(chunk(g, 0, masked), jnp.maximum)
                for c in range(1, nch):
                    m8 = jnp.maximum(m8, _rows8_reduce(chunk(g, c, masked), jnp.maximum))
                m_prev = m_sc[g]
                m_new = jnp.maximum(m_prev, jnp.max(m8, axis=0, keepdims=True))
                alpha = jnp.exp2(m_prev - m_new)
                lsum = exp_chunks(g, lambda c: chunk(g, c, masked) - m_new, 0,
                                  jnp.zeros((SUBLANES, tq), jnp.float32))
                l_sc[g] = alpha * l_sc[g] + jnp.sum(lsum, axis=0, keepdims=True)
                acc_sc[g] = alpha * acc_sc[g] + _dot(vt_tile(g, n), p_sc[0, g])
                m_sc[g] = m_new

        def full(n, carry):
            tile(n, False)
            return carry

        jax.lax.fori_loop(0, i, full, 0)
        tile(i, True)

    def denominators(l_range):
        l_lo, l_hi = l_range
        for g in range(G):
            l = jnp.sum(l8_sc[g], axis=0, keepdims=True)
            l_sc[g] = l
            l_lo = jnp.minimum(l_lo, jnp.min(l))
            l_hi = jnp.maximum(l_hi, jnp.max(l))
        return l_lo, l_hi

    l_range = (jnp.float32(jnp.inf), jnp.float32(0.0))
    for i in range(nq):
        if i > 0:
            l_range = denominators(l_range)
            finish(i - 1)
        diagonal(i)
        for n in range(i):
            for g in range(G):
                qk_exp(g, n % 2, n, i)
                pv(g, 1 - n % 2, i if n == 0 else n - 1)
    l_lo, l_hi = denominators(l_range)
    finish(nq - 1)

    in_range = jnp.logical_and(l_lo > SHIFT_L_MIN, l_hi < SHIFT_L_MAX)

    @pl.when(jnp.logical_not(in_range))
    def _():
        def redo(i, carry):
            slow_path(i)
            emit(i)
            return carry

        jax.lax.fori_loop(0, nq, redo, 0)


def _attn_call(qt, k, vt, gm):
    B, H, S, _ = k.shape
    tq, G = ATT_TQ, ATT_G
    assert ATT_TQ == ATT_TK and ATT_TK % KT == 0
    seq = lambda a: pl.BlockSpec((1, G) + a.shape[2:], lambda b, h: (b, h) + (0,) * (a.ndim - 2))
    tok = pl.BlockSpec((1, S, G * V_DIM), lambda b, h: (b, 0, h))
    return pl.pallas_call(
        _attn_kernel,
        grid=(B, H // G),
        in_specs=[seq(qt), seq(k), seq(vt), tok],
        out_specs=tok,
        out_shape=jax.ShapeDtypeStruct((B, S, D_MLA), jnp.bfloat16),
        scratch_shapes=[
            pltpu.VMEM((G, ATT_TK, tq), jnp.float32),
            pltpu.VMEM((2, G, ATT_TK, tq), jnp.bfloat16),
            pltpu.VMEM((G, SUBLANES, tq), jnp.float32),
            pltpu.VMEM((G, V_DIM, tq), jnp.float32),
            pltpu.VMEM((G, 1, tq), jnp.float32),
            pltpu.VMEM((G, 1, tq), jnp.float32),
        ],
        compiler_params=pltpu.CompilerParams(
            dimension_semantics=("arbitrary", "arbitrary"),
            vmem_limit_bytes=VMEM_LIMIT_BYTES),
        name="attn",
    )(qt, k, vt, gm)


def _out_kernel(o_ref, pc_ref, h_ref, w_ref, b_ref, g_ref, beta_ref, y_ref):
    for r in range(0, OUT_TM, OUT_CH):
        rows = slice(r, r + OUT_CH)
        mix = jnp.concatenate([o_ref[0, rows, :], pc_ref[0, rows, :]], axis=1)
        y = _dot(mix, w_ref[0]) + b_ref[0] + DEEPNORM_ALPHA * h_ref[0, rows, :]
        y_ref[0, rows, :] = _layernorm(y, g_ref[0], beta_ref[0])


def _out_call(layer, o, pc, h, w_out, b_out, ln_g, ln_b):
    B, S, D = h.shape
    tm = OUT_TM
    tok = lambda w: pl.BlockSpec((1, tm, w), lambda b, i: (b, i, 0))
    return pl.pallas_call(
        _out_kernel,
        grid=(B, S // tm),
        in_specs=[tok(D_MLA), tok(D_POOL + D_CONV), tok(D),
                  _layer_spec(w_out, layer), _layer_spec(b_out, layer), _layer_spec(ln_g, layer),
                  _layer_spec(ln_b, layer)],
        out_specs=tok(D),
        out_shape=jax.ShapeDtypeStruct((B, S, D), jnp.float32),
        compiler_params=pltpu.CompilerParams(
            dimension_semantics=("arbitrary", "arbitrary"),
            vmem_limit_bytes=VMEM_LIMIT_BYTES),
        name="out",
    )(o, pc, h, w_out, b_out, ln_g, ln_b)


def _rotate_half_cols(w):
    return jnp.concatenate([-w[..., HALF:], w[..., :HALF]], axis=-1)


def _wprep_lat_kernel(wt_ref, o_ref):
    x = wt_ref[0]
    kr = C_KR
    rot = jnp.concatenate([-x[kr + HALF:kr + ROPE], x[kr:kr + HALF]], axis=0)
    o_ref[0] = jnp.concatenate([x, rot], axis=0).T.astype(jnp.bfloat16)


def _wprep_mix_kernel(wt_ref, o_ref):
    o_ref[0] = wt_ref[0].T.astype(jnp.bfloat16)


def _prep_w_in(w):
    L, D, C = w.shape
    wt = jnp.swapaxes(w, 1, 2)
    n_lat = C_KR + ROPE
    params = pltpu.CompilerParams(vmem_limit_bytes=VMEM_LIMIT_BYTES)
    w_lat = pl.pallas_call(
        _wprep_lat_kernel,
        grid=(L,),
        in_specs=[pl.BlockSpec((pl.Element(1), pl.Element(n_lat), pl.Element(D)), lambda l: (l, 0, 0))],
        out_specs=pl.BlockSpec((1, D, C_LAT_END), lambda l: (l, 0, 0)),
        out_shape=jax.ShapeDtypeStruct((L, D, C_LAT_END), jnp.bfloat16),
        compiler_params=params,
        name="wprep_lat",
    )(wt)
    w_mix = pl.pallas_call(
        _wprep_mix_kernel,
        grid=(L, R_END // WPREP_COLS),
        in_specs=[pl.BlockSpec((pl.Element(1), pl.Element(WPREP_COLS), pl.Element(D)),
                               lambda l, j: (l, pl.multiple_of(n_lat + WPREP_COLS * j, SUBLANES), 0))],
        out_specs=pl.BlockSpec((1, D, WPREP_COLS), lambda l, j: (l, 0, j)),
        out_shape=jax.ShapeDtypeStruct((L, D, R_END), jnp.bfloat16),
        compiler_params=params,
        name="wprep_mix",
    )(wt)
    return w_lat, w_mix


def _prep_w_uq_t(w):
    L = w.shape[0]
    w = w.reshape(L, Q_LORA, N_HEADS, NOPE + ROPE)
    w = jnp.concatenate([w, _rotate_half_cols(w[..., NOPE:])], axis=-1)
    return jnp.swapaxes(w.reshape(L, Q_LORA, N_HEADS * HEAD_W), 1, 2).astype(jnp.bfloat16)


def _prep_w_ukv(w):
    L = w.shape[0]
    w = w.reshape(L, KV_LORA, N_HEADS, NOPE + V_DIM)
    wk = w[..., :NOPE].reshape(L, KV_LORA, N_HEADS * NOPE)
    wv = w[..., NOPE:].reshape(L, KV_LORA, N_HEADS * V_DIM)
    return wk.astype(jnp.bfloat16), jnp.swapaxes(wv, 1, 2).astype(jnp.bfloat16)


def kernel(x, positions, emb_ln_g, emb_ln_b, w_in, q_norm_g, kv_norm_g, w_uq, w_ukv, w_pool,
           pool_scale, conv_w, w_out, b_out, ln_g, ln_b):
    B, S, D = x.shape
    assert D == D_MODEL and w_in.shape == (DEPTH, D_MODEL, C_KR + ROPE + R_END)
    assert conv_w.shape == (DEPTH, CONV_WIDTH, D_CONV) and S % ATT_TQ == 0 and S % OUT_TM == 0
    bf = jnp.bfloat16
    posr = positions.reshape(B, 1, S)
    inv_freq = ROPE_THETA ** (-jnp.arange(HALF, dtype=jnp.float32) / HALF)
    invfc = inv_freq.reshape(HALF, 1)
    row = lambda a: a.reshape(1, -1)

    w_lat, w_mix = _prep_w_in(w_in)
    w_uk, w_uvt = _prep_w_ukv(w_ukv)
    row3 = lambda a: a.reshape(a.shape[0], 1, -1)
    stacked = (w_lat, w_mix, row3(q_norm_g), row3(kv_norm_g), _prep_w_uq_t(w_uq), w_uk, w_uvt,
               w_pool.astype(bf), row3(pool_scale), conv_w)
    w_out_b = w_out.astype(bf)

    h = x
    for l in range(DEPTH):
        outs = _proj_call(l, h, posr, row(emb_ln_g), row(emb_ln_b), invfc, stacked)
        if l == 0:
            qt, k, vt, gm, pc, h = outs
        else:
            qt, k, vt, gm, pc = outs
        o = _attn_call(qt, k, vt, gm)
        h = _out_call(l, o, pc, h, w_out_b, row3(b_out), row3(ln_g), row3(ln_b))
    return h
```

```python
import functools
import math

import jax
import jax.numpy as jnp
from jax.experimental import pallas as pl
from jax.experimental.pallas import tpu as pltpu

D_MODEL = 2048
DEPTH = 2
N_HEADS = 8
NOPE = 128
ROPE = 64
V_DIM = 128
Q_LORA = 512
KV_LORA = 256
D_MLA = N_HEADS * V_DIM
ROPE_THETA = 10000.0
POOL_WINDOWS = (2, 4, 8, 16)
POOL_GROUP = 128
D_POOL = 512
D_CONV = 512
CONV_WIDTH = 3
LN_EPS = 1e-5
RMS_EPS = 1e-6
DEEPNORM_ALPHA = (2 * DEPTH) ** 0.25

LANES = 128
SUBLANES = 8
VMEM_LIMIT_BYTES = 56 * 1024 * 1024

PROJ_TM = 256
ATT_TQ = 512
ATT_TK = 512
ATT_G = 2
ATT_CH = 64
WPREP_COLS = 1024
OUT_TM = 512
OUT_CH = 128
POOL_HALO = 16
CONV_HALO = 8

KT = PROJ_TM
HEAD_W = 2 * LANES
HALF = ROPE // 2

C_QLAT = 0
C_KVLAT = C_QLAT + Q_LORA
C_KR = C_KVLAT + KV_LORA
C_LAT_END = C_KR + LANES
R_GMLA = 0
R_PIN = R_GMLA + D_MLA
R_GPOOL = R_PIN + D_POOL
R_CH = R_GPOOL + D_POOL
R_END = R_CH + 4 * D_CONV

Q_PRESCALE = (NOPE + ROPE) ** -0.5 * math.log2(math.e)
M_INIT = -1e30
SHIFT_L_MIN = 2.0 ** -100
SHIFT_L_MAX = 2.0 ** 100


def _silu(g):
    return g * (1.0 / (1.0 + jnp.exp(-g)))


def _layernorm(x, g, b):
    mu = jnp.mean(x, axis=-1, keepdims=True)
    xc = x - mu
    var = jnp.mean(xc * xc, axis=-1, keepdims=True)
    return xc * jax.lax.rsqrt(var + LN_EPS) * g + b


def _rmsnorm(x, g):
    return x * jax.lax.rsqrt(jnp.mean(x * x, axis=-1, keepdims=True) + RMS_EPS) * g


def _dot(a, b):
    return jnp.dot(a, b, preferred_element_type=jnp.float32)


def _dot_nt(a, b):
    return jax.lax.dot_general(a, b, (((1,), (1,)), ((), ())), preferred_element_type=jnp.float32)


def _proj_kernel(first, *refs):
    if first:
        (x_ref, posr_ref, eg_ref, eb_ref, w_lat_ref, w_mix_ref, qg_ref, kvg_ref, w_uqt_ref, w_uk_ref,
         w_uvt_ref, w_pool_ref, pscale_ref, convw_ref, invfc_ref,
         qt_ref, k_ref, vt_ref, gm_ref, pc_ref, h_ref, pool_ext, conv_ext) = refs
    else:
        (x_ref, posr_ref, w_lat_ref, w_mix_ref, qg_ref, kvg_ref, w_uqt_ref, w_uk_ref,
         w_uvt_ref, w_pool_ref, pscale_ref, convw_ref, invfc_ref,
         qt_ref, k_ref, vt_ref, gm_ref, pc_ref, pool_ext, conv_ext) = refs
    tm = PROJ_TM
    i = pl.program_id(1)

    @pl.when(i == 0)
    def _():
        pool_ext[0:POOL_HALO, :] = jnp.zeros((POOL_HALO, D_POOL), jnp.float32)
        conv_ext[0:CONV_HALO, :] = jnp.zeros((CONV_HALO, D_CONV), jnp.float32)

    @pl.when(i > 0)
    def _():
        pool_ext[0:POOL_HALO, :] = pool_ext[tm:tm + POOL_HALO, :]
        conv_ext[0:CONV_HALO, :] = conv_ext[tm:tm + CONV_HALO, :]

    x = x_ref[0]
    if first:
        x = _layernorm(x, eg_ref[...], eb_ref[...])
        h_ref[0] = x
    hb = x.astype(jnp.bfloat16)

    lat = _dot(hb, w_lat_ref[0])
    pin = _dot(hb, w_mix_ref[0, :, R_PIN:R_GPOOL])
    cv = _dot(hb, w_mix_ref[0, :, R_CH:R_END])

    qn = _rmsnorm(lat[:, C_QLAT:C_KVLAT], qg_ref[0]).astype(jnp.bfloat16)
    kvn = _rmsnorm(lat[:, C_KVLAT:C_KR], kvg_ref[0]).astype(jnp.bfloat16)
    kr = lat[:, C_KR:C_LAT_END]
    angt = invfc_ref[...] * posr_ref[0].astype(jnp.float32)
    cos_t, sin_t = jnp.cos(angt), jnp.sin(angt)
    cs = jnp.concatenate([cos_t, cos_t, sin_t, sin_t], axis=0).T
    lane = jax.lax.broadcasted_iota(jnp.int32, (tm, LANES), 1)
    t = kr * cs
    kroped = jnp.where(lane < ROPE, t + pltpu.roll(t, ROPE, axis=1), 0.0).astype(jnp.bfloat16)

    kn = _dot(kvn, w_uk_ref[0])
    vt = _dot_nt(w_uvt_ref[0], kvn)
    qt = _dot_nt(w_uqt_ref[0], qn)
    gpool = _dot(hb, w_mix_ref[0, :, R_GPOOL:R_CH])

    for hh in range(N_HEADS):
        k_ref[0, hh] = jnp.concatenate(
            [kn[:, hh * NOPE:(hh + 1) * NOPE].astype(jnp.bfloat16), kroped], axis=1)
        vt_ref[0, hh, 0] = vt[hh * V_DIM:(hh + 1) * V_DIM, :].astype(jnp.bfloat16)

    u = cv[:, 2 * D_CONV:3 * D_CONV] * cv[:, 0:D_CONV]
    conv_ext[CONV_HALO:CONV_HALO + tm, :] = u
    yc = convw_ref[0, 2:3, :] * u
    yc = yc + convw_ref[0, 1:2, :] * conv_ext[CONV_HALO - 1:CONV_HALO - 1 + tm, :]
    yc = yc + convw_ref[0, 0:1, :] * conv_ext[CONV_HALO - 2:CONV_HALO - 2 + tm, :]
    y_conv = cv[:, D_CONV:2 * D_CONV] * yc * _silu(cv[:, 3 * D_CONV:4 * D_CONV])
    pc_ref[0, :, D_POOL:] = y_conv.astype(jnp.bfloat16)

    pool_ext[POOL_HALO:POOL_HALO + tm, :] = pin
    t1 = (i * tm + 1 + jax.lax.broadcasted_iota(jnp.int32, (tm, 1), 0)).astype(jnp.float32)
    ys = []
    for g, w in enumerate(POOL_WINDOWS):
        lo = g * POOL_GROUP
        acc = pin[:, lo:lo + POOL_GROUP]
        for s in range(1, w):
            acc = acc + pool_ext[POOL_HALO - s:POOL_HALO - s + tm, lo:lo + POOL_GROUP]
        pooled = acc / jnp.minimum(t1, float(w)) - pin[:, lo:lo + POOL_GROUP]
        ys.append(_dot(pooled.astype(jnp.bfloat16), w_pool_ref[0, g]))
    gm = _dot(hb, w_mix_ref[0, :, R_GMLA:R_PIN])
    y_pool = jnp.concatenate(ys, axis=1) * pscale_ref[0] * _silu(gpool)
    pc_ref[0, :, :D_POOL] = y_pool.astype(jnp.bfloat16)

    cos_t = jnp.concatenate([cos_t, cos_t], axis=0)
    sin_t = jnp.concatenate([sin_t, sin_t], axis=0)
    zeros_t = jnp.zeros((HEAD_W - NOPE - ROPE, tm), jnp.float32)
    for hh in range(N_HEADS):
        r0 = hh * HEAD_W
        roped = qt[r0 + NOPE:r0 + NOPE + ROPE, :] * cos_t + qt[r0 + NOPE + ROPE:r0 + HEAD_W, :] * sin_t
        qh = jnp.concatenate([qt[r0:r0 + NOPE, :], roped, zeros_t], axis=0)
        qt_ref[0, hh, 0] = (qh * Q_PRESCALE).astype(jnp.bfloat16)

    gm_ref[0] = _silu(gm).astype(jnp.bfloat16)


def _const_spec(shape):
    nd = len(shape)
    return pl.BlockSpec(shape, lambda *_: (0,) * nd, pipeline_mode=pl.Buffered(1))


def _layer_spec(arr, layer):
    nd = arr.ndim
    return pl.BlockSpec((1,) + arr.shape[1:], lambda *_: (layer,) + (0,) * (nd - 1),
                        pipeline_mode=pl.Buffered(1))


def _proj_call(layer, x, posr, emb_g, emb_b, inv_freq_col, stacked):
    first = layer == 0
    B, S, D = x.shape
    tm = PROJ_TM
    grid = (B, S // tm)
    qsub = ATT_TQ // tm
    tok = lambda w: pl.BlockSpec((1, tm, w), lambda b, i: (b, i, 0))
    in_specs = [tok(D), pl.BlockSpec((1, 1, tm), lambda b, i: (b, 0, i))]
    args = [x, posr]
    if first:
        in_specs += [_const_spec((1, D)), _const_spec((1, D))]
        args += [emb_g, emb_b]
    in_specs += [_layer_spec(a, layer) for a in stacked] + [_const_spec(inv_freq_col.shape)]
    args += list(stacked) + [inv_freq_col]
    bf = jnp.bfloat16
    out_shape = [
        jax.ShapeDtypeStruct((B, N_HEADS, S // ATT_TQ, HEAD_W, ATT_TQ), bf),
        jax.ShapeDtypeStruct((B, N_HEADS, S, HEAD_W), bf),
        jax.ShapeDtypeStruct((B, N_HEADS, S // KT, V_DIM, KT), bf),
        jax.ShapeDtypeStruct((B, S, D_MLA), bf),
        jax.ShapeDtypeStruct((B, S, D_POOL + D_CONV), bf),
    ]
    out_specs = [
        pl.BlockSpec((1, N_HEADS, 1, HEAD_W, tm), lambda b, i: (b, 0, i // qsub, 0, i % qsub)),
        pl.BlockSpec((1, N_HEADS, tm, HEAD_W), lambda b, i: (b, 0, i, 0)),
        pl.BlockSpec((1, N_HEADS, 1, V_DIM, KT), lambda b, i: (b, 0, i, 0, 0)),
        tok(D_MLA), tok(D_POOL + D_CONV)]
    if first:
        out_shape.append(jax.ShapeDtypeStruct((B, S, D), jnp.float32))
        out_specs.append(tok(D))
    return pl.pallas_call(
        functools.partial(_proj_kernel, first),
        grid=grid,
        in_specs=in_specs,
        out_specs=out_specs,
        out_shape=out_shape,
        scratch_shapes=[
            pltpu.VMEM((POOL_HALO + tm, D_POOL), jnp.float32),
            pltpu.VMEM((CONV_HALO + tm, D_CONV), jnp.float32),
        ],
        compiler_params=pltpu.CompilerParams(
            dimension_semantics=("arbitrary", "arbitrary"),
            vmem_limit_bytes=VMEM_LIMIT_BYTES),
        name="proj_first" if first else "proj",
    )(*args)


def _rows8_reduce(x, op):
    acc = x[0:SUBLANES]
    for r in range(SUBLANES, x.shape[0], SUBLANES):
        acc = op(acc, x[r:r + SUBLANES])
    return acc


def _attn_kernel(qt_ref, k_ref, vt_ref, gm_ref, o_ref,
                 s_sc, p_sc, l8_sc, acc_sc, m_sc, l_sc):
    tq, tk, G, ch = ATT_TQ, ATT_TK, ATT_G, ATT_CH
    sub, nch, nqb = tk // KT, tk // ch, tq // LANES
    per_lane_block = LANES // ch
    nq = qt_ref.shape[2]

    def vt_tile(g, n):
        return jnp.concatenate([vt_ref[0, g, n * sub + c] for c in range(sub)], axis=1)

    def k_tile(g, n):
        start = n * tk if isinstance(n, int) else pl.multiple_of(n * tk, tk)
        return k_ref[0, g, pl.ds(start, tk), :]

    def rows(c):
        return slice(c * ch, (c + 1) * ch)

    def lanes(j):
        return slice(j * LANES, (j + 1) * LANES)

    def exp_chunks(g, chunk_of, buf, lsum):
        for c in range(nch):
            p_c = jnp.exp2(chunk_of(c))
            lsum = lsum + _rows8_reduce(p_c, jnp.add)
            p_sc[buf, g, rows(c), :] = p_c.astype(jnp.bfloat16)
        return lsum

    def qk_exp(g, buf, n, qi):
        s = _dot(k_tile(g, n), qt_ref[0, g, qi])
        m_ref = m_sc[g]
        l8_sc[g] = exp_chunks(g, lambda c: s[rows(c)] - m_ref, buf, l8_sc[g])

    def pv(g, buf, n):
        acc_sc[g] = acc_sc[g] + _dot(vt_tile(g, n), p_sc[buf, g])

    def emit(i):
        q0 = i * tq if isinstance(i, int) else pl.multiple_of(i * tq, tq)
        for g in range(G):
            out_t = acc_sc[g] / l_sc[g]
            gate = gm_ref[0, pl.ds(q0, tq), g * V_DIM:(g + 1) * V_DIM].astype(jnp.float32)
            o_ref[0, pl.ds(q0, tq), g * V_DIM:(g + 1) * V_DIM] = (out_t.T * gate).astype(o_ref.dtype)

    def finish(i):
        last = max(i - 1, 0)
        for g in range(G):
            pv(g, 1 if i == 0 else last % 2, last)
        emit(i)

    def diagonal(i):
        for g in range(G):
            s_sc[g] = _dot(k_tile(g, i), qt_ref[0, g, i])

        def block(g, c, j):
            s_b = s_sc[g, rows(c), lanes(j)]
            if c < per_lane_block * j:
                return s_b
            key = c * ch + jax.lax.broadcasted_iota(jnp.int32, (ch, LANES), 0)
            qry = j * LANES + jax.lax.broadcasted_iota(jnp.int32, (ch, LANES), 1)
            return jnp.where(key <= qry, s_b, -jnp.inf)

        for g in range(G):
            for j in range(nqb):
                visible = range(per_lane_block * (j + 1))
                m8 = _rows8_reduce(block(g, 0, j), jnp.maximum)
                for c in visible[1:]:
                    m8 = jnp.maximum(m8, _rows8_reduce(block(g, c, j), jnp.maximum))
                m_ref = jnp.max(m8, axis=0, keepdims=True)
                lsum = jnp.zeros((SUBLANES, LANES), jnp.float32)
                for c in range(nch):
                    if c in visible:
                        p_b = jnp.exp2(block(g, c, j) - m_ref)
                        lsum = lsum + _rows8_reduce(p_b, jnp.add)
                        p_sc[1, g, rows(c), lanes(j)] = p_b.astype(jnp.bfloat16)
                    else:
                        p_sc[1, g, rows(c), lanes(j)] = jnp.zeros((ch, LANES), jnp.bfloat16)
                l8_sc[g, :, lanes(j)] = lsum
                m_sc[g, :, lanes(j)] = m_ref
        acc_sc[...] = jnp.zeros(acc_sc.shape, jnp.float32)

    def slow_path(i):
        m_sc[...] = jnp.full(m_sc.shape, M_INIT, jnp.float32)
        l_sc[...] = jnp.zeros(l_sc.shape, jnp.float32)
        acc_sc[...] = jnp.zeros(acc_sc.shape, jnp.float32)

        def chunk(g, c, masked):
            s_c = s_sc[g, rows(c), :]
            if not masked:
                return s_c
            key = c * ch + jax.lax.broadcasted_iota(jnp.int32, (ch, tq), 0)
            return jnp.where(key <= jax.lax.broadcasted_iota(jnp.int32, (ch, tq), 1), s_c, -jnp.inf)

        def tile(n, masked):
            for g in range(G):
                s_sc[g] = _dot(k_tile(g, n), qt_ref[0, g, i])
                m8 = _rows8_reduce(chunk(g, 0, masked), jnp.maximum)
                for c in range(1, nch):
                    m8 = jnp.maximum(m8, _rows8_reduce(chunk(g, c, masked), jnp.maximum))
                m_prev = m_sc[g]
                m_new = jnp.maximum(m_prev, jnp.max(m8, axis=0, keepdims=True))
                alpha = jnp.exp2(m_prev - m_new)
                lsum = exp_chunks(g, lambda c: chunk(g, c, masked) - m_new, 0,
                                  jnp.zeros((SUBLANES, tq), jnp.float32))
                l_sc[g] = alpha * l_sc[g] + jnp.sum(lsum, axis=0, keepdims=True)
                acc_sc[g] = alpha * acc_sc[g] + _dot(vt_tile(g, n), p_sc[0, g])
                m_sc[g] = m_new

        def full(n, carry):
            tile(n, False)
            return carry

        jax.lax.fori_loop(0, i, full, 0)
        tile(i, True)

    def denominators(l_range):
        l_lo, l_hi = l_range
        for g in range(G):
            l = jnp.sum(l8_sc[g], axis=0, keepdims=True)
            l_sc[g] = l
            l_lo = jnp.minimum(l_lo, jnp.min(l))
            l_hi = jnp.maximum(l_hi, jnp.max(l))
        return l_lo, l_hi

    l_range = (jnp.float32(jnp.inf), jnp.float32(0.0))
    for i in range(nq):
        if i > 0:
            l_range = denominators(l_range)
            finish(i - 1)
        diagonal(i)
        for n in range(i):
            for g in range(G):
                qk_exp(g, n % 2, n, i)
                pv(g, 1 - n % 2, i if n == 0 else n - 1)
    l_lo, l_hi = denominators(l_range)
    finish(nq - 1)

    in_range = jnp.logical_and(l_lo > SHIFT_L_MIN, l_hi < SHIFT_L_MAX)

    @pl.when(jnp.logical_not(in_range))
    def _():
        def redo(i, carry):
            slow_path(i)
            emit(i)
            return carry

        jax.lax.fori_loop(0, nq, redo, 0)


def _attn_call(qt, k, vt, gm):
    B, H, S, _ = k.shape
    tq, G = ATT_TQ, ATT_G
    assert ATT_TQ == ATT_TK and ATT_TK % KT == 0
    seq = lambda a: pl.BlockSpec((1, G) + a.shape[2:], lambda b, h: (b, h) + (0,) * (a.ndim - 2))
    tok = pl.BlockSpec((1, S, G * V_DIM), lambda b, h: (b, 0, h))
    return pl.pallas_call(
        _attn_kernel,
        grid=(B, H // G),
        in_specs=[seq(qt), seq(k), seq(vt), tok],
        out_specs=tok,
        out_shape=jax.ShapeDtypeStruct((B, S, D_MLA), jnp.bfloat16),
        scratch_shapes=[
            pltpu.VMEM((G, ATT_TK, tq), jnp.float32),
            pltpu.VMEM((2, G, ATT_TK, tq), jnp.bfloat16),
            pltpu.VMEM((G, SUBLANES, tq), jnp.float32),
            pltpu.VMEM((G, V_DIM, tq), jnp.float32),
            pltpu.VMEM((G, 1, tq), jnp.float32),
            pltpu.VMEM((G, 1, tq), jnp.float32),
        ],
        compiler_params=pltpu.CompilerParams(
            dimension_semantics=("arbitrary", "arbitrary"),
            vmem_limit_bytes=VMEM_LIMIT_BYTES),
        name="attn",
    )(qt, k, vt, gm)


def _out_kernel(o_ref, pc_ref, h_ref, w_ref, b_ref, g_ref, beta_ref, y_ref):
    for r in range(0, OUT_TM, OUT_CH):
        rows = slice(r, r + OUT_CH)
        mix = jnp.concatenate([o_ref[0, rows, :], pc_ref[0, rows, :]], axis=1)
        y = _dot(mix, w_ref[0]) + b_ref[0] + DEEPNORM_ALPHA * h_ref[0, rows, :]
        y_ref[0, rows, :] = _layernorm(y, g_ref[0], beta_ref[0])


def _out_call(layer, o, pc, h, w_out, b_out, ln_g, ln_b):
    B, S, D = h.shape
    tm = OUT_TM
    tok = lambda w: pl.BlockSpec((1, tm, w), lambda b, i: (b, i, 0))
    return pl.pallas_call(
        _out_kernel,
        grid=(B, S // tm),
        in_specs=[tok(D_MLA), tok(D_POOL + D_CONV), tok(D),
                  _layer_spec(w_out, layer), _layer_spec(b_out, layer), _layer_spec(ln_g, layer),
                  _layer_spec(ln_b, layer)],
        out_specs=tok(D),
        out_shape=jax.ShapeDtypeStruct((B, S, D), jnp.float32),
        compiler_params=pltpu.CompilerParams(
            dimension_semantics=("arbitrary", "arbitrary"),
            vmem_limit_bytes=VMEM_LIMIT_BYTES),
        name="out",
    )(o, pc, h, w_out, b_out, ln_g, ln_b)


def _rotate_half_cols(w):
    return jnp.concatenate([-w[..., HALF:], w[..., :HALF]], axis=-1)


def _wprep_lat_kernel(wt_ref, o_ref):
    x = wt_ref[0]
    kr = C_KR
    rot = jnp.concatenate([-x[kr + HALF:kr + ROPE], x[kr:kr + HALF]], axis=0)
    o_ref[0] = jnp.concatenate([x, rot], axis=0).T.astype(jnp.bfloat16)


def _wprep_mix_kernel(wt_ref, o_ref):
    o_ref[0] = wt_ref[0].T.astype(jnp.bfloat16)


def _prep_w_in(w):
    L, D, C = w.shape
    wt = jnp.swapaxes(w, 1, 2)
    n_lat = C_KR + ROPE
    params = pltpu.CompilerParams(vmem_limit_bytes=VMEM_LIMIT_BYTES)
    w_lat = pl.pallas_call(
        _wprep_lat_kernel,
        grid=(L,),
        in_specs=[pl.BlockSpec((pl.Element(1), pl.Element(n_lat), pl.Element(D)), lambda l: (l, 0, 0))],
        out_specs=pl.BlockSpec((1, D, C_LAT_END), lambda l: (l, 0, 0)),
        out_shape=jax.ShapeDtypeStruct((L, D, C_LAT_END), jnp.bfloat16),
        compiler_params=params,
        name="wprep_lat",
    )(wt)
    w_mix = pl.pallas_call(
        _wprep_mix_kernel,
        grid=(L, R_END // WPREP_COLS),
        in_specs=[pl.BlockSpec((pl.Element(1), pl.Element(WPREP_COLS), pl.Element(D)),
                               lambda l, j: (l, pl.multiple_of(n_lat + WPREP_COLS * j, SUBLANES), 0))],
        out_specs=pl.BlockSpec((1, D, WPREP_COLS), lambda l, j: (l, 0, j)),
        out_shape=jax.ShapeDtypeStruct((L, D, R_END), jnp.bfloat16),
        compiler_params=params,
        name="wprep_mix",
    )(wt)
    return w_lat, w_mix


def _prep_w_uq_t(w):
    L = w.shape[0]
    w = w.reshape(L, Q_LORA, N_HEADS, NOPE + ROPE)
    w = jnp.concatenate([w, _rotate_half_cols(w[..., NOPE:])], axis=-1)
    return jnp.swapaxes(w.reshape(L, Q_LORA, N_HEADS * HEAD_W), 1, 2).astype(jnp.bfloat16)


def _prep_w_ukv(w):
    L = w.shape[0]
    w = w.reshape(L, KV_LORA, N_HEADS, NOPE + V_DIM)
    wk = w[..., :NOPE].reshape(L, KV_LORA, N_HEADS * NOPE)
    wv = w[..., NOPE:].reshape(L, KV_LORA, N_HEADS * V_DIM)
    return wk.astype(jnp.bfloat16), jnp.swapaxes(wv, 1, 2).astype(jnp.bfloat16)


def kernel(x, positions, emb_ln_g, emb_ln_b, w_in, q_norm_g, kv_norm_g, w_uq, w_ukv, w_pool,
           pool_scale, conv_w, w_out, b_out, ln_g, ln_b):
    B, S, D = x.shape
    assert D == D_MODEL and w_in.shape == (DEPTH, D_MODEL, C_KR + ROPE + R_END)
    assert conv_w.shape == (DEPTH, CONV_WIDTH, D_CONV) and S % ATT_TQ == 0 and S % OUT_TM == 0
    bf = jnp.bfloat16
    posr = positions.reshape(B, 1, S)
    inv_freq = ROPE_THETA ** (-jnp.arange(HALF, dtype=jnp.float32) / HALF)
    invfc = inv_freq.reshape(HALF, 1)
    row = lambda a: a.reshape(1, -1)

    w_lat, w_mix = _prep_w_in(w_in)
    w_uk, w_uvt = _prep_w_ukv(w_ukv)
    row3 = lambda a: a.reshape(a.shape[0], 1, -1)
    stacked = (w_lat, w_mix, row3(q_norm_g), row3(kv_norm_g), _prep_w_uq_t(w_uq), w_uk, w_uvt,
               w_pool.astype(bf), row3(pool_scale), conv_w)
    w_out_b = w_out.astype(bf)

    h = x
    for l in range(DEPTH):
        outs = _proj_call(l, h, posr, row(emb_ln_g), row(emb_ln_b), invfc, stacked)
        if l == 0:
            qt, k, vt, gm, pc, h = outs
        else:
            qt, k, vt, gm, pc = outs
        o = _attn_call(qt, k, vt, gm)
        h = _out_call(l, o, pc, h, w_out_b, row3(b_out), row3(ln_g), row3(ln_b))
    return h
```

```python
import functools
import math

import jax
import jax.numpy as jnp
from jax.experimental import pallas as pl
from jax.experimental.pallas import tpu as pltpu

D_MODEL = 2048
DEPTH = 2
N_HEADS = 8
NOPE = 128
ROPE = 64
V_DIM = 128
Q_LORA = 512
KV_LORA = 256
D_MLA = N_HEADS * V_DIM
ROPE_THETA = 10000.0
POOL_WINDOWS = (2, 4, 8, 16)
POOL_GROUP = 128
D_POOL = 512
D_CONV = 512
CONV_WIDTH = 3
LN_EPS = 1e-5
RMS_EPS = 1e-6
DEEPNORM_ALPHA = (2 * DEPTH) ** 0.25

LANES = 128
SUBLANES = 8
VMEM_LIMIT_BYTES = 56 * 1024 * 1024

PROJ_TM = 256
ATT_TQ = 512
ATT_TK = 512
ATT_G = 2
ATT_CH = 64
WPREP_COLS = 1024
OUT_TM = 1024
OUT_CH = 128
POOL_HALO = 16
CONV_HALO = 8

KT = PROJ_TM
HEAD_W = 2 * LANES
HALF = ROPE // 2

C_QLAT = 0
C_KVLAT = C_QLAT + Q_LORA
C_KR = C_KVLAT + KV_LORA
C_LAT_END = C_KR + LANES
R_GMLA = 0
R_PIN = R_GMLA + D_MLA
R_GPOOL = R_PIN + D_POOL
R_CH = R_GPOOL + D_POOL
R_END = R_CH + 4 * D_CONV

Q_PRESCALE = (NOPE + ROPE) ** -0.5 * math.log2(math.e)
M_INIT = -1e30
SHIFT_L_MIN = 2.0 ** -100
SHIFT_L_MAX = 2.0 ** 100


def _silu(g):
    return g * (1.0 / (1.0 + jnp.exp(-g)))


def _layernorm(x, g, b):
    mu = jnp.mean(x, axis=-1, keepdims=True)
    xc = x - mu
    var = jnp.mean(xc * xc, axis=-1, keepdims=True)
    return xc * jax.lax.rsqrt(var + LN_EPS) * g + b


def _rmsnorm(x, g):
    return x * jax.lax.rsqrt(jnp.mean(x * x, axis=-1, keepdims=True) + RMS_EPS) * g


def _dot(a, b):
    return jnp.dot(a, b, preferred_element_type=jnp.float32)


def _dot_nt(a, b):
    return jax.lax.dot_general(a, b, (((1,), (1,)), ((), ())), preferred_element_type=jnp.float32)


def _proj_kernel(first, *refs):
    if first:
        (x_ref, posr_ref, eg_ref, eb_ref, w_lat_ref, w_mix_ref, qg_ref, kvg_ref, w_uqt_ref, w_uk_ref,
         w_uvt_ref, w_pool_ref, pscale_ref, convw_ref, invfc_ref,
         qt_ref, k_ref, vt_ref, gm_ref, pc_ref, h_ref, pool_ext, conv_ext) = refs
    else:
        (x_ref, posr_ref, w_lat_ref, w_mix_ref, qg_ref, kvg_ref, w_uqt_ref, w_uk_ref,
         w_uvt_ref, w_pool_ref, pscale_ref, convw_ref, invfc_ref,
         qt_ref, k_ref, vt_ref, gm_ref, pc_ref, pool_ext, conv_ext) = refs
    tm = PROJ_TM
    i = pl.program_id(1)

    @pl.when(i == 0)
    def _():
        pool_ext[0:POOL_HALO, :] = jnp.zeros((POOL_HALO, D_POOL), jnp.float32)
        conv_ext[0:CONV_HALO, :] = jnp.zeros((CONV_HALO, D_CONV), jnp.float32)

    @pl.when(i > 0)
    def _():
        pool_ext[0:POOL_HALO, :] = pool_ext[tm:tm + POOL_HALO, :]
        conv_ext[0:CONV_HALO, :] = conv_ext[tm:tm + CONV_HALO, :]

    x = x_ref[0]
    if first:
        x = _layernorm(x, eg_ref[...], eb_ref[...])
        h_ref[0] = x
    hb = x.astype(jnp.bfloat16)

    lat = _dot(hb, w_lat_ref[0])
    pin = _dot(hb, w_mix_ref[0, :, R_PIN:R_GPOOL])
    cv = _dot(hb, w_mix_ref[0, :, R_CH:R_END])

    qn = _rmsnorm(lat[:, C_QLAT:C_KVLAT], qg_ref[0]).astype(jnp.bfloat16)
    kvn = _rmsnorm(lat[:, C_KVLAT:C_KR], kvg_ref[0]).astype(jnp.bfloat16)
    kr = lat[:, C_KR:C_LAT_END]
    angt = invfc_ref[...] * posr_ref[0].astype(jnp.float32)
    cos_t, sin_t = jnp.cos(angt), jnp.sin(angt)
    cs = jnp.concatenate([cos_t, cos_t, sin_t, sin_t], axis=0).T
    lane = jax.lax.broadcasted_iota(jnp.int32, (tm, LANES), 1)
    t = kr * cs
    kroped = jnp.where(lane < ROPE, t + pltpu.roll(t, ROPE, axis=1), 0.0).astype(jnp.bfloat16)

    kn = _dot(kvn, w_uk_ref[0])
    vt = _dot_nt(w_uvt_ref[0], kvn)
    qt = _dot_nt(w_uqt_ref[0], qn)
    gpool = _dot(hb, w_mix_ref[0, :, R_GPOOL:R_CH])

    for hh in range(N_HEADS):
        k_ref[0, hh] = jnp.concatenate(
            [kn[:, hh * NOPE:(hh + 1) * NOPE].astype(jnp.bfloat16), kroped], axis=1)
        vt_ref[0, hh, 0] = vt[hh * V_DIM:(hh + 1) * V_DIM, :].astype(jnp.bfloat16)

    u = cv[:, 2 * D_CONV:3 * D_CONV] * cv[:, 0:D_CONV]
    conv_ext[CONV_HALO:CONV_HALO + tm, :] = u
    yc = convw_ref[0, 2:3, :] * u
    yc = yc + convw_ref[0, 1:2, :] * conv_ext[CONV_HALO - 1:CONV_HALO - 1 + tm, :]
    yc = yc + convw_ref[0, 0:1, :] * conv_ext[CONV_HALO - 2:CONV_HALO - 2 + tm, :]
    y_conv = cv[:, D_CONV:2 * D_CONV] * yc * _silu(cv[:, 3 * D_CONV:4 * D_CONV])
    pc_ref[0, :, D_POOL:] = y_conv.astype(jnp.bfloat16)

    pool_ext[POOL_HALO:POOL_HALO + tm, :] = pin
    t1 = (i * tm + 1 + jax.lax.broadcasted_iota(jnp.int32, (tm, 1), 0)).astype(jnp.float32)
    ys = []
    for g, w in enumerate(POOL_WINDOWS):
        lo = g * POOL_GROUP
        acc = pin[:, lo:lo + POOL_GROUP]
        for s in range(1, w):
            acc = acc + pool_ext[POOL_HALO - s:POOL_HALO - s + tm, lo:lo + POOL_GROUP]
        pooled = acc / jnp.minimum(t1, float(w)) - pin[:, lo:lo + POOL_GROUP]
        ys.append(_dot(pooled.astype(jnp.bfloat16), w_pool_ref[0, g]))
    gm = _dot(hb, w_mix_ref[0, :, R_GMLA:R_PIN])
    y_pool = jnp.concatenate(ys, axis=1) * pscale_ref[0] * _silu(gpool)
    pc_ref[0, :, :D_POOL] = y_pool.astype(jnp.bfloat16)

    cos_t = jnp.concatenate([cos_t, cos_t], axis=0)
    sin_t = jnp.concatenate([sin_t, sin_t], axis=0)
    zeros_t = jnp.zeros((HEAD_W - NOPE - ROPE, tm), jnp.float32)
    for hh in range(N_HEADS):
        r0 = hh * HEAD_W
        roped = qt[r0 + NOPE:r0 + NOPE + ROPE, :] * cos_t + qt[r0 + NOPE + ROPE:r0 + HEAD_W, :] * sin_t
        qh = jnp.concatenate([qt[r0:r0 + NOPE, :], roped, zeros_t], axis=0)
        qt_ref[0, hh, 0] = (qh * Q_PRESCALE).astype(jnp.bfloat16)

    gm_ref[0] = _silu(gm).astype(jnp.bfloat16)


def _const_spec(shape):
    nd = len(shape)
    return pl.BlockSpec(shape, lambda *_: (0,) * nd, pipeline_mode=pl.Buffered(1))


def _layer_spec(arr, layer):
    nd = arr.ndim
    return pl.BlockSpec((1,) + arr.shape[1:], lambda *_: (layer,) + (0,) * (nd - 1),
                        pipeline_mode=pl.Buffered(1))


def _proj_call(layer, x, posr, emb_g, emb_b, inv_freq_col, stacked):
    first = layer == 0
    B, S, D = x.shape
    tm = PROJ_TM
    grid = (B, S // tm)
    qsub = ATT_TQ // tm
    tok = lambda w: pl.BlockSpec((1, tm, w), lambda b, i: (b, i, 0))
    in_specs = [tok(D), pl.BlockSpec((1, 1, tm), lambda b, i: (b, 0, i))]
    args = [x, posr]
    if first:
        in_specs += [_const_spec((1, D)), _const_spec((1, D))]
        args += [emb_g, emb_b]
    in_specs += [_layer_spec(a, layer) for a in stacked] + [_const_spec(inv_freq_col.shape)]
    args += list(stacked) + [inv_freq_col]
    bf = jnp.bfloat16
    out_shape = [
        jax.ShapeDtypeStruct((B, N_HEADS, S // ATT_TQ, HEAD_W, ATT_TQ), bf),
        jax.ShapeDtypeStruct((B, N_HEADS, S, HEAD_W), bf),
        jax.ShapeDtypeStruct((B, N_HEADS, S // KT, V_DIM, KT), bf),
        jax.ShapeDtypeStruct((B, S, D_MLA), bf),
        jax.ShapeDtypeStruct((B, S, D_POOL + D_CONV), bf),
    ]
    out_specs = [
        pl.BlockSpec((1, N_HEADS, 1, HEAD_W, tm), lambda b, i: (b, 0, i // qsub, 0, i % qsub)),
        pl.BlockSpec((1, N_HEADS, tm, HEAD_W), lambda b, i: (b, 0, i, 0)),
        pl.BlockSpec((1, N_HEADS, 1, V_DIM, KT), lambda b, i: (b, 0, i, 0, 0)),
        tok(D_MLA), tok(D_POOL + D_CONV)]
    if first:
        out_shape.append(jax.ShapeDtypeStruct((B, S, D), jnp.float32))
        out_specs.append(tok(D))
    return pl.pallas_call(
        functools.partial(_proj_kernel, first),
        grid=grid,
        in_specs=in_specs,
        out_specs=out_specs,
        out_shape=out_shape,
        scratch_shapes=[
            pltpu.VMEM((POOL_HALO + tm, D_POOL), jnp.float32),
            pltpu.VMEM((CONV_HALO + tm, D_CONV), jnp.float32),
        ],
        compiler_params=pltpu.CompilerParams(
            dimension_semantics=("arbitrary", "arbitrary"),
            vmem_limit_bytes=VMEM_LIMIT_BYTES),
        name="proj_first" if first else "proj",
    )(*args)


def _rows8_reduce(x, op):
    acc = x[0:SUBLANES]
    for r in range(SUBLANES, x.shape[0], SUBLANES):
        acc = op(acc, x[r:r + SUBLANES])
    return acc


def _attn_kernel(qt_ref, k_ref, vt_ref, gm_ref, o_ref,
                 s_sc, p_sc, l8_sc, acc_sc, m_sc, l_sc):
    tq, tk, G, ch = ATT_TQ, ATT_TK, ATT_G, ATT_CH
    sub, nch, nqb = tk // KT, tk // ch, tq // LANES
    per_lane_block = LANES // ch
    nq = qt_ref.shape[2]

    def vt_tile(g, n):
        return jnp.concatenate([vt_ref[0, g, n * sub + c] for c in range(sub)], axis=1)

    def k_tile(g, n):
        start = n * tk if isinstance(n, int) else pl.multiple_of(n * tk, tk)
        return k_ref[0, g, pl.ds(start, tk), :]

    def rows(c):
        return slice(c * ch, (c + 1) * ch)

    def lanes(j):
        return slice(j * LANES, (j + 1) * LANES)

    def exp_chunks(g, chunk_of, buf, lsum):
        for c in range(nch):
            p_c = jnp.exp2(chunk_of(c))
            lsum = lsum + _rows8_reduce(p_c, jnp.add)
            p_sc[buf, g, rows(c), :] = p_c.astype(jnp.bfloat16)
        return lsum

    def qk_exp(g, buf, n, qi):
        s = _dot(k_tile(g, n), qt_ref[0, g, qi])
        m_ref = m_sc[g]
        l8_sc[g] = exp_chunks(g, lambda c: s[rows(c)] - m_ref, buf, l8_sc[g])

    def pv(g, buf, n):
        acc_sc[g] = acc_sc[g] + _dot(vt_tile(g, n), p_sc[buf, g])

    def emit(i):
        q0 = i * tq if isinstance(i, int) else pl.multiple_of(i * tq, tq)
        for g in range(G):
            out_t = acc_sc[g] / l_sc[g]
            gate = gm_ref[0, pl.ds(q0, tq), g * V_DIM:(g + 1) * V_DIM].astype(jnp.float32)
            o_ref[0, pl.ds(q0, tq), g * V_DIM:(g + 1) * V_DIM] = (out_t.T * gate).astype(o_ref.dtype)

    def finish(i):
        last = max(i - 1, 0)
        for g in range(G):
            pv(g, 1 if i == 0 else last % 2, last)
        emit(i)

    def diagonal(i):
        for g in range(G):
            s_sc[g] = _dot(k_tile(g, i), qt_ref[0, g, i])

        def block(g, c, j):
            s_b = s_sc[g, rows(c), lanes(j)]
            if c < per_lane_block * j:
                return s_b
            key = c * ch + jax.lax.broadcasted_iota(jnp.int32, (ch, LANES), 0)
            qry = j * LANES + jax.lax.broadcasted_iota(jnp.int32, (ch, LANES), 1)
            return jnp.where(key <= qry, s_b, -jnp.inf)

        for g in range(G):
            for j in range(nqb):
                visible = range(per_lane_block * (j + 1))
                m8 = _rows8_reduce(block(g, 0, j), jnp.maximum)
                for c in visible[1:]:
                    m8 = jnp.maximum(m8, _rows8_reduce(block(g, c, j), jnp.maximum))
                m_ref = jnp.max(m8, axis=0, keepdims=True)
                lsum = jnp.zeros((SUBLANES, LANES), jnp.float32)
                for c in range(nch):
                    if c in visible:
                        p_b = jnp.exp2(block(g, c, j) - m_ref)
                        lsum = lsum + _rows8_reduce(p_b, jnp.add)
                        p_sc[1, g, rows(c), lanes(j)] = p_b.astype(jnp.bfloat16)
                    else:
                        p_sc[1, g, rows(c), lanes(j)] = jnp.zeros((ch, LANES), jnp.bfloat16)
                l8_sc[g, :, lanes(j)] = lsum
                m_sc[g, :, lanes(j)] = m_ref
        acc_sc[...] = jnp.zeros(acc_sc.shape, jnp.float32)

    def slow_path(i):
        m_sc[...] = jnp.full(m_sc.shape, M_INIT, jnp.float32)
        l_sc[...] = jnp.zeros(l_sc.shape, jnp.float32)
        acc_sc[...] = jnp.zeros(acc_sc.shape, jnp.float32)

        def chunk(g, c, masked):
            s_c = s_sc[g, rows(c), :]
            if not masked:
                return s_c
            key = c * ch + jax.lax.broadcasted_iota(jnp.int32, (ch, tq), 0)
            return jnp.where(key <= jax.lax.broadcasted_iota(jnp.int32, (ch, tq), 1), s_c, -jnp.inf)

        def tile(n, masked):
            for g in range(G):
                s_sc[g] = _dot(k_tile(g, n), qt_ref[0, g, i])
                m8 = _rows8_reduce(chunk(g, 0, masked), jnp.maximum)
                for c in range(1, nch):
                    m8 = jnp.maximum(m8, _rows8_reduce(chunk(g, c, masked), jnp.maximum))
                m_prev = m_sc[g]
                m_new = jnp.maximum(m_prev, jnp.max(m8, axis=0, keepdims=True))
                alpha = jnp.exp2(m_prev - m_new)
                lsum = exp_chunks(g, lambda c: chunk(g, c, masked) - m_new, 0,
                                  jnp.zeros((SUBLANES, tq), jnp.float32))
                l_sc[g] = alpha * l_sc[g] + jnp.sum(lsum, axis=0, keepdims=True)
                acc_sc[g] = alpha * acc_sc[g] + _dot(vt_tile(g, n), p_sc[0, g])
                m_sc[g] = m_new

        def full(n, carry):
            tile(n, False)
            return carry

        jax.lax.fori_loop(0, i, full, 0)
        tile(i, True)

    def denominators(l_range):
        l_lo, l_hi = l_range
        for g in range(G):
            l = jnp.sum(l8_sc[g], axis=0, keepdims=True)
            l_sc[g] = l
            l_lo = jnp.minimum(l_lo, jnp.min(l))
            l_hi = jnp.maximum(l_hi, jnp.max(l))
        return l_lo, l_hi

    l_range = (jnp.float32(jnp.inf), jnp.float32(0.0))
    for i in range(nq):
        if i > 0:
            l_range = denominators(l_range)
            finish(i - 1)
        diagonal(i)
        for n in range(i):
            for g in range(G):
                qk_exp(g, n % 2, n, i)
                pv(g, 1 - n % 2, i if n == 0 else n - 1)
    l_lo, l_hi = denominators(l_range)
    finish(nq - 1)

    in_range = jnp.logical_and(l_lo > SHIFT_L_MIN, l_hi < SHIFT_L_MAX)

    @pl.when(jnp.logical_not(in_range))
    def _():
        def redo(i, carry):
            slow_path(i)
            emit(i)
            return carry

        jax.lax.fori_loop(0, nq, redo, 0)


def _attn_call(qt, k, vt, gm):
    B, H, S, _ = k.shape
    tq, G = ATT_TQ, ATT_G
    assert ATT_TQ == ATT_TK and ATT_TK % KT == 0
    seq = lambda a: pl.BlockSpec((1, G) + a.shape[2:], lambda b, h: (b, h) + (0,) * (a.ndim - 2))
    tok = pl.BlockSpec((1, S, G * V_DIM), lambda b, h: (b, 0, h))
    return pl.pallas_call(
        _attn_kernel,
        grid=(B, H // G),
        in_specs=[seq(qt), seq(k), seq(vt), tok],
        out_specs=tok,
        out_shape=jax.ShapeDtypeStruct((B, S, D_MLA), jnp.bfloat16),
        scratch_shapes=[
            pltpu.VMEM((G, ATT_TK, tq), jnp.float32),
            pltpu.VMEM((2, G, ATT_TK, tq), jnp.bfloat16),
            pltpu.VMEM((G, SUBLANES, tq), jnp.float32),
            pltpu.VMEM((G, V_DIM, tq), jnp.float32),
            pltpu.VMEM((G, 1, tq), jnp.float32),
            pltpu.VMEM((G, 1, tq), jnp.float32),
        ],
        compiler_params=pltpu.CompilerParams(
            dimension_semantics=("arbitrary", "arbitrary"),
            vmem_limit_bytes=VMEM_LIMIT_BYTES),
        name="attn",
    )(qt, k, vt, gm)


def _out_kernel(o_ref, pc_ref, h_ref, w_ref, b_ref, g_ref, beta_ref, y_ref):
    for r in range(0, OUT_TM, OUT_CH):
        rows = slice(r, r + OUT_CH)
        mix = jnp.concatenate([o_ref[0, rows, :], pc_ref[0, rows, :]], axis=1)
        y = _dot(mix, w_ref[0]) + b_ref[0] + DEEPNORM_ALPHA * h_ref[0, rows, :]
        y_ref[0, rows, :] = _layernorm(y, g_ref[0], beta_ref[0])


def _out_call(layer, o, pc, h, w_out, b_out, ln_g, ln_b):
    B, S, D = h.shape
    tm = OUT_TM
    tok = lambda w: pl.BlockSpec((1, tm, w), lambda b, i: (b, i, 0))
    return pl.pallas_call(
        _out_kernel,
        grid=(B, S // tm),
        in_specs=[tok(D_MLA), tok(D_POOL + D_CONV), tok(D),
                  _layer_spec(w_out, layer), _layer_spec(b_out, layer), _layer_spec(ln_g, layer),
                  _layer_spec(ln_b, layer)],
        out_specs=tok(D),
        out_shape=jax.ShapeDtypeStruct((B, S, D), jnp.float32),
        compiler_params=pltpu.CompilerParams(
            dimension_semantics=("arbitrary", "arbitrary"),
            vmem_limit_bytes=VMEM_LIMIT_BYTES),
        name="out",
    )(o, pc, h, w_out, b_out, ln_g, ln_b)


def _rotate_half_cols(w):
    return jnp.concatenate([-w[..., HALF:], w[..., :HALF]], axis=-1)


def _wprep_lat_kernel(wt_ref, o_ref):
    x = wt_ref[0]
    kr = C_KR
    rot = jnp.concatenate([-x[kr + HALF:kr + ROPE], x[kr:kr + HALF]], axis=0)
    o_ref[0] = jnp.concatenate([x, rot], axis=0).T.astype(jnp.bfloat16)


def _wprep_mix_kernel(wt_ref, o_ref):
    o_ref[0] = wt_ref[0].T.astype(jnp.bfloat16)


def _prep_w_in(w):
    L, D, C = w.shape
    wt = jnp.swapaxes(w, 1, 2)
    n_lat = C_KR + ROPE
    params = pltpu.CompilerParams(vmem_limit_bytes=VMEM_LIMIT_BYTES)
    w_lat = pl.pallas_call(
        _wprep_lat_kernel,
        grid=(L,),
        in_specs=[pl.BlockSpec((pl.Element(1), pl.Element(n_lat), pl.Element(D)), lambda l: (l, 0, 0))],
        out_specs=pl.BlockSpec((1, D, C_LAT_END), lambda l: (l, 0, 0)),
        out_shape=jax.ShapeDtypeStruct((L, D, C_LAT_END), jnp.bfloat16),
        compiler_params=params,
        name="wprep_lat",
    )(wt)
    w_mix = pl.pallas_call(
        _wprep_mix_kernel,
        grid=(L, R_END // WPREP_COLS),
        in_specs=[pl.BlockSpec((pl.Element(1), pl.Element(WPREP_COLS), pl.Element(D)),
                               lambda l, j: (l, pl.multiple_of(n_lat + WPREP_COLS * j, SUBLANES), 0))],
        out_specs=pl.BlockSpec((1, D, WPREP_COLS), lambda l, j: (l, 0, j)),
        out_shape=jax.ShapeDtypeStruct((L, D, R_END), jnp.bfloat16),
        compiler_params=params,
        name="wprep_mix",
    )(wt)
    return w_lat, w_mix


def _prep_w_uq_t(w):
    L = w.shape[0]
    w = w.reshape(L, Q_LORA, N_HEADS, NOPE + ROPE)
    w = jnp.concatenate([w, _rotate_half_cols(w[..., NOPE:])], axis=-1)
    return jnp.swapaxes(w.reshape(L, Q_LORA, N_HEADS * HEAD_W), 1, 2).astype(jnp.bfloat16)


def _prep_w_ukv(w):
    L = w.shape[0]
    w = w.reshape(L, KV_LORA, N_HEADS, NOPE + V_DIM)
    wk = w[..., :NOPE].reshape(L, KV_LORA, N_HEADS * NOPE)
    wv = w[..., NOPE:].reshape(L, KV_LORA, N_HEADS * V_DIM)
    return wk.astype(jnp.bfloat16), jnp.swapaxes(wv, 1, 2).astype(jnp.bfloat16)


def kernel(x, positions, emb_ln_g, emb_ln_b, w_in, q_norm_g, kv_norm_g, w_uq, w_ukv, w_pool,
           pool_scale, conv_w, w_out, b_out, ln_g, ln_b):
    B, S, D = x.shape
    assert D == D_MODEL and w_in.shape == (DEPTH, D_MODEL, C_KR + ROPE + R_END)
    assert conv_w.shape == (DEPTH, CONV_WIDTH, D_CONV) and S % ATT_TQ == 0 and S % OUT_TM == 0
    bf = jnp.bfloat16
    posr = positions.reshape(B, 1, S)
    inv_freq = ROPE_THETA ** (-jnp.arange(HALF, dtype=jnp.float32) / HALF)
    invfc = inv_freq.reshape(HALF, 1)
    row = lambda a: a.reshape(1, -1)

    w_lat, w_mix = _prep_w_in(w_in)
    w_uk, w_uvt = _prep_w_ukv(w_ukv)
    row3 = lambda a: a.reshape(a.shape[0], 1, -1)
    stacked = (w_lat, w_mix, row3(q_norm_g), row3(kv_norm_g), _prep_w_uq_t(w_uq), w_uk, w_uvt,
               w_pool.astype(bf), row3(pool_scale), conv_w)
    w_out_b = w_out.astype(bf)

    h = x
    for l in range(DEPTH):
        outs = _proj_call(l, h, posr, row(emb_ln_g), row(emb_ln_b), invfc, stacked)
        if l == 0:
            qt, k, vt, gm, pc, h = outs
        else:
            qt, k, vt, gm, pc = outs
        o = _attn_call(qt, k, vt, gm)
        h = _out_call(l, o, pc, h, w_out_b, row3(b_out), row3(ln_g), row3(ln_b))
    return h
```
